```python
import jax
import jax.numpy as jnp
from jax import lax
import numpy as np

D_MODEL = 2048
BATCH = 4
SEQ = 2048
DEPTH = 4
DEC_BATCH = 8
DEC_SEQ = 4
PAST_LEN = 16384
PAGE_SIZE = 128

HEAD_DIM = 128
N_HEADS = D_MODEL // HEAD_DIM
H_FOX = N_HEADS // 2
H_MLSTM = N_HEADS - H_FOX
H_DIL = N_HEADS
FOX_W = H_FOX * HEAD_DIM
ML_W = H_MLSTM * HEAD_DIM
AB_SIZES = (FOX_W, FOX_W, FOX_W, H_FOX, ML_W, ML_W, ML_W, H_MLSTM, H_MLSTM, ML_W)
AB_WIDTH = 3 * FOX_W + H_FOX + 4 * ML_W + 2 * H_MLSTM
DIL_PATTERNS = ((128, 1), (512, 4), (2048, 16))
WINDOW_MAX = 2048
DIL_BLOCK = 128
Q_BLOCK = 128
MLSTM_CHUNK = 128
D_FF = ((8 * D_MODEL // 3 + 127) // 128) * 128
N_AB = (DEPTH + 1) // 2
N_C = DEPTH // 2
N_MOD = 9
FFN_RES = 0.5
EPS = 1e-6
CACHE_LOGF_BIAS = 8.0
F32 = jnp.float32

kernel_name = 'fox_mlstm_dilated_macaron_adaln_step'


def rmsnorm(x, g):
    xf = x.astype(F32)
    y = xf * lax.rsqrt(jnp.mean(xf * xf, axis=-1, keepdims=True) + EPS)
    return (y * g.astype(F32)).astype(x.dtype)


def adaln(c, w, b):
    return jnp.split(jax.nn.silu(c) @ w + b, N_MOD, axis=-1)


def modulate(h, shift, scale):
    return h * (1 + scale[:, None, :]) + shift[:, None, :]


def swiglu(h, w13, w2):
    a, g = jnp.split(h @ w13, 2, axis=-1)
    return (jax.nn.silu(a) * g) @ w2


def ffn_sublayer(x, shift, scale, gate, g, w13, w2):
    h = modulate(rmsnorm(x, g), shift, scale)
    return x + FFN_RES * gate[:, None, :] * swiglu(h, w13, w2)


def ab_split(h, w_in, fox_fb, ml_ib, ml_fb):
    B, S = h.shape[:2]
    offs, acc = [], 0
    for size in AB_SIZES[:-1]:
        acc += size
        offs.append(acc)
    q_f, k_f, v_f, f_f, q_m, k_m, v_m, i_m, f_m, o_m = jnp.split(h @ w_in, offs, axis=-1)

    def heads(a, n):
        return a.reshape(B, S, n, HEAD_DIM).astype(F32)

    def bhsd(a):
        return heads(a, H_MLSTM).transpose(0, 2, 1, 3)

    fox = (heads(q_f, H_FOX) * HEAD_DIM ** -0.5, heads(k_f, H_FOX), heads(v_f, H_FOX),
           jax.nn.log_sigmoid(f_f.astype(F32) + fox_fb.astype(F32)))
    ml = (bhsd(q_m), bhsd(k_m) * HEAD_DIM ** -0.5, bhsd(v_m),
          (i_m.astype(F32) + ml_ib.astype(F32)).transpose(0, 2, 1),
          jax.nn.log_sigmoid(f_m.astype(F32) + ml_fb.astype(F32)).transpose(0, 2, 1))
    return fox, ml, o_m


def fox_prompt(q, k, v, lf):
    B, S, H, hd = q.shape
    R = lax.cumsum(lf, axis=1, reverse=True) - lf
    RT = R.transpose(0, 2, 1)
    nb = S // Q_BLOCK
    qb = jnp.moveaxis(q.reshape(B, nb, Q_BLOCK, H, hd), 1, 0)
    Rb = jnp.moveaxis(RT.reshape(B, H, nb, Q_BLOCK), 2, 0)
    key_pos = jnp.arange(S)

    def block(args):
        qi, Ri, bi = args
        s = jnp.einsum('bqhd,bshd->bhqs', qi, k)
        bias = RT[:, :, None, :] - Ri[:, :, :, None]
        qpos = bi * Q_BLOCK + jnp.arange(Q_BLOCK)
        mask = key_pos[None, :] <= qpos[:, None]
        p = jax.nn.softmax(jnp.where(mask, s + bias, -jnp.inf), axis=-1)
        return jnp.einsum('bhqs,bshd->bqhd', p, v)

    o = lax.map(block, (qb, Rb, jnp.arange(nb)))
    return jnp.moveaxis(o, 0, 1).reshape(B, S, H, hd)


def fox_sample(q, k, v, lf, k_past, v_past, lf_past):
    P, T = k_past.shape[1], q.shape[1]
    k_all = jnp.concatenate([k_past.astype(F32), k], axis=1)
    v_all = jnp.concatenate([v_past.astype(F32), v], axis=1)
    lf_all = jnp.concatenate([lf_past.astype(F32), lf], axis=1)
    RT = (lax.cumsum(lf_all, axis=1, reverse=True) - lf_all).transpose(0, 2, 1)
    s = jnp.einsum('bthd,bshd->bhts', q, k_all)
    bias = RT[:, :, None, :] - RT[:, :, P:, None]
    mask = jnp.arange(P + T)[None, :] <= (P + jnp.arange(T))[:, None]
    p = jax.nn.softmax(jnp.where(mask, s + bias, -jnp.inf), axis=-1)
    return jnp.einsum('bhts,bshd->bthd', p, v_all)


def gather_pages(pool, page_table):
    g = jnp.take(pool, page_table, axis=0)
    return g.reshape((page_table.shape[0], page_table.shape[1] * pool.shape[1]) + pool.shape[2:])


def mlstm_chunk(state, q, k, v, ig, lf):
    C, n, m = state
    L = q.shape[2]
    b = jnp.cumsum(lf, axis=-1)
    causal = jnp.tril(jnp.ones((L, L), dtype=bool))
    dmat = jnp.where(causal, b[..., :, None] - b[..., None, :] + ig[..., None, :], -jnp.inf)
    inter = b + m[..., None]
    m_tok = jnp.maximum(inter, jnp.max(dmat, axis=-1))
    w_inter = jnp.exp(inter - m_tok)
    qk = jnp.einsum('bhtd,bhsd->bhts', q, k) * jnp.exp(dmat - m_tok[..., None])
    num = w_inter[..., None] * jnp.einsum('bhtd,bhde->bhte', q, C) + jnp.einsum('bhts,bhse->bhte', qk, v)
    den = w_inter * jnp.einsum('bhtd,bhd->bht', q, n) + jnp.sum(qk, axis=-1)
    h = num / jnp.maximum(jnp.abs(den), jnp.exp(-m_tok))[..., None]
    bL = b[..., -1]
    g = bL[..., None] - b + ig
    m_new = jnp.maximum(bL + m, jnp.max(g, axis=-1))
    a_prev = jnp.exp(bL + m - m_new)
    w_tok = jnp.exp(g - m_new[..., None])
    C_new = a_prev[..., None, None] * C + jnp.einsum('bhs,bhsd,bhse->bhde', w_tok, k, v)
    n_new = a_prev[..., None] * n + jnp.einsum('bhs,bhsd->bhd', w_tok, k)
    return (C_new, n_new, m_new), h


def mlstm_prompt(q, k, v, ig, lf):
    B, H, S, dk = q.shape
    dv = v.shape[-1]
    nc = S // MLSTM_CHUNK

    def chunks(a):
        return jnp.moveaxis(a.reshape(a.shape[:2] + (nc, MLSTM_CHUNK) + a.shape[3:]), 2, 0)

    state0 = (jnp.zeros((B, H, dk, dv), F32), jnp.zeros((B, H, dk), F32), jnp.zeros((B, H), F32))
    state, hs = lax.scan(lambda st, xs_: mlstm_chunk(st, *xs_), state0,
                         (chunks(q), chunks(k), chunks(v), chunks(ig), chunks(lf)))
    h = jnp.moveaxis(hs, 0, 2).reshape(B, H, S, dv)
    return h, state


def ab_merge(o_fox, h_ml, o_gate, norm_g, w_out, dtype):
    B, S = o_gate.shape[:2]
    hn = h_ml * lax.rsqrt(jnp.mean(h_ml * h_ml, axis=-1, keepdims=True) + EPS)
    hn = hn * norm_g.astype(F32).reshape(H_MLSTM, HEAD_DIM)[None, :, None, :]
    hn = hn.transpose(0, 2, 1, 3).reshape(B, S, ML_W) * jax.nn.sigmoid(o_gate.astype(F32))
    mix = jnp.concatenate([o_fox.reshape(B, S, FOX_W), hn], axis=-1)
    return mix.astype(dtype) @ w_out


def dil_project(h, w_in):
    B, S = h.shape[:2]
    q, k, v = jnp.split(h @ w_in, 3, axis=-1)

    def r(a):
        return a.reshape(B, S, H_DIL, HEAD_DIM).astype(F32)

    return r(q) * HEAD_DIM ** -0.5, r(k), r(v)


def dilated_branch_prompt(q, k, v, d, span):
    B, S, H, hd = q.shape
    L = -(-S // d)
    Lp = -(-L // DIL_BLOCK) * DIL_BLOCK
    Sp = Lp * d
    nb = Lp // DIL_BLOCK

    def to_sub(x):
        x = jnp.pad(x, ((0, 0), (0, Sp - S), (0, 0), (0, 0)))
        x = x.reshape(B, Lp, d, H, hd).transpose(0, 2, 3, 1, 4)
        return x.reshape(B, d, H, nb, DIL_BLOCK, hd)

    def with_prev(x):
        prev = jnp.pad(x[:, :, :, :-1], ((0, 0), (0, 0), (0, 0), (1, 0), (0, 0), (0, 0)))
        return jnp.concatenate([prev, x], axis=4)

    qs = to_sub(q)
    kk, vv = with_prev(to_sub(k)), with_prev(to_sub(v))
    s = jnp.einsum('bdhnqe,bdhnke->bdhnqk', qs, kk)
    qi = jnp.arange(DIL_BLOCK)[:, None]
    kj = jnp.arange(2 * DIL_BLOCK)[None, :]
    dist = DIL_BLOCK + qi - kj
    band = (dist >= 0) & (dist <= span)
    first = (jnp.arange(nb) == 0)[:, None, None]
    valid = band[None] & ~(first & (kj < DIL_BLOCK)[None])
    s = jnp.where(valid, s, -jnp.inf)
    m = jnp.max(s, axis=-1)
    p = jnp.exp(s - m[..., None])
    l = jnp.sum(p, axis=-1)
    acc = jnp.einsum('bdhnqk,bdhnke->bdhnqe', p, vv)

    def from_sub(x):
        rest = x.shape[5:]
        x = jnp.moveaxis(x.reshape((B, d, H, Lp) + rest), 3, 1)
        return x.reshape((B, Sp, H) + rest)[:, :S]

    return from_sub(m), from_sub(l), from_sub(acc)


def dilated_branch_sample(q, k_all, v_all, d, span):
    T = q.shape[1]
    CB = k_all.shape[1] - T
    idx = CB + jnp.arange(T)[:, None] - d * jnp.arange(span + 1)[None, :]
    valid = idx >= 0
    idxc = jnp.maximum(idx, 0)
    kg, vg = k_all[:, idxc], v_all[:, idxc]
    s = jnp.einsum('bthe,btkhe->bthk', q, kg)
    s = jnp.where(valid[None, :, None, :], s, -jnp.inf)
    m = jnp.max(s, axis=-1)
    p = jnp.exp(s - m[..., None])
    return m, jnp.sum(p, axis=-1), jnp.einsum('bthk,btkhe->bthe', p, vg)


def combine_branches(parts):
    ms = jnp.stack([pt[0] for pt in parts])
    w = jnp.exp(ms - jnp.max(ms, axis=0))
    den = jnp.sum(w * jnp.stack([pt[1] for pt in parts]), axis=0)
    num = jnp.sum(w[..., None] * jnp.stack([pt[2] for pt in parts]), axis=0)
    return num / den[..., None]


def setup_inputs(seed: int = 0) -> dict:
    key = jax.random.key(seed)
    ks = list(jax.random.split(key, 32))
    it = iter(ks)

    def nrm(shape, s):
        return jax.random.normal(next(it), shape, F32) * s

    def gain(shape):
        return 1.0 + nrm(shape, 0.02)

    n_pages = PAST_LEN // PAGE_SIZE
    n_used = DEC_BATCH * n_pages
    n_pool = n_used + max(1, n_used // 4)
    cb = min(WINDOW_MAX, PAST_LEN)
    d = D_MODEL
    return {
        'x_prompt': nrm((BATCH, SEQ, d), 1.0),
        'x_sample': nrm((DEC_BATCH, DEC_SEQ, d), 1.0),
        'c_prompt': nrm((BATCH, d), 1.0),
        'c_sample': nrm((DEC_BATCH, d), 1.0),
        'page_table': jax.random.permutation(next(it), n_pool)[:n_used].reshape(DEC_BATCH, n_pages).astype(jnp.int32),
        'cache_fox_k': nrm((N_AB, n_pool, PAGE_SIZE, H_FOX, HEAD_DIM), 1.0),
        'cache_fox_v': nrm((N_AB, n_pool, PAGE_SIZE, H_FOX, HEAD_DIM), 1.0),
        'cache_fox_logf': jax.nn.log_sigmoid(CACHE_LOGF_BIAS + nrm((N_AB, n_pool, PAGE_SIZE, H_FOX), 0.5)),
        'state_mlstm_c': nrm((N_AB, DEC_BATCH, H_MLSTM, HEAD_DIM, HEAD_DIM), 0.1),
        'state_mlstm_n': nrm((N_AB, DEC_BATCH, H_MLSTM, HEAD_DIM), 0.1),
        'state_mlstm_m': nrm((N_AB, DEC_BATCH, H_MLSTM), 1.0),
        'cache_win_k': nrm((N_C, DEC_BATCH, cb, H_DIL, HEAD_DIM), 1.0),
        'cache_win_v': nrm((N_C, DEC_BATCH, cb, H_DIL, HEAD_DIM), 1.0),
        'w_ada': nrm((DEPTH, d, N_MOD * d), 0.5 * d ** -0.5),
        'b_ada': nrm((DEPTH, N_MOD * d), 0.02),
        'norm_pre': gain((DEPTH, d)),
        'norm_mix': gain((DEPTH, d)),
        'norm_post': gain((DEPTH, d)),
        'norm_final': gain((d,)),
        'ffn_pre_w13': nrm((DEPTH, d, 2 * D_FF), d ** -0.5),
        'ffn_pre_w2': nrm((DEPTH, D_FF, d), D_FF ** -0.5),
        'ffn_post_w13': nrm((DEPTH, d, 2 * D_FF), d ** -0.5),
        'ffn_post_w2': nrm((DEPTH, D_FF, d), D_FF ** -0.5),
        'ab_w_in': nrm((N_AB, d, AB_WIDTH), d ** -0.5),
        'ab_w_out': nrm((N_AB, FOX_W + ML_W, d), (FOX_W + ML_W) ** -0.5),
        'fox_fgate_b': jax.random.uniform(next(it), (N_AB, H_FOX), F32, 3.0, 8.0),
        'mlstm_igate_b': nrm((N_AB, H_MLSTM), 0.1),
        'mlstm_fgate_b': jax.random.uniform(next(it), (N_AB, H_MLSTM), F32, 3.0, 6.0),
        'mlstm_norm_g': gain((N_AB, ML_W)),
        'c_w_in': nrm((N_C, d, 3 * H_DIL * HEAD_DIM), d ** -0.5),
        'c_w_out': nrm((N_C, H_DIL * HEAD_DIM, d), (H_DIL * HEAD_DIM) ** -0.5),
    }


def reference(x_prompt, x_sample, c_prompt, c_sample, page_table,
              cache_fox_k, cache_fox_v, cache_fox_logf,
              state_mlstm_c, state_mlstm_n, state_mlstm_m,
              cache_win_k, cache_win_v,
              w_ada, b_ada, norm_pre, norm_mix, norm_post, norm_final,
              ffn_pre_w13, ffn_pre_w2, ffn_post_w13, ffn_post_w2,
              ab_w_in, ab_w_out, fox_fgate_b, mlstm_igate_b, mlstm_fgate_b, mlstm_norm_g,
              c_w_in, c_w_out):
    xp, xs = x_prompt, x_sample
    fk_p, fv_p, fl_p, fk_s, fv_s, fl_s = [], [], [], [], [], []
    mc_p, mn_p, mm_p, mc_s, mn_s, mm_s = [], [], [], [], [], []
    wk_p, wv_p, wk_s, wv_s = [], [], [], []
    for l in range(DEPTH):
        j = l // 2
        mp = adaln(c_prompt, w_ada[l], b_ada[l])
        md = adaln(c_sample, w_ada[l], b_ada[l])
        xp = ffn_sublayer(xp, mp[0], mp[1], mp[2], norm_pre[l], ffn_pre_w13[l], ffn_pre_w2[l])
        xs = ffn_sublayer(xs, md[0], md[1], md[2], norm_pre[l], ffn_pre_w13[l], ffn_pre_w2[l])
        hp = modulate(rmsnorm(xp, norm_mix[l]), mp[3], mp[4])
        hs = modulate(rmsnorm(xs, norm_mix[l]), md[3], md[4])
        if l % 2 == 0:
            fox, ml, og = ab_split(hp, ab_w_in[j], fox_fgate_b[j], mlstm_igate_b[j], mlstm_fgate_b[j])
            o_fox = fox_prompt(*fox)
            h_ml, st = mlstm_prompt(*ml)
            op = ab_merge(o_fox, h_ml, og, mlstm_norm_g[j], ab_w_out[j], hp.dtype)
            fk_p.append(fox[1]); fv_p.append(fox[2]); fl_p.append(fox[3])
            mc_p.append(st[0]); mn_p.append(st[1]); mm_p.append(st[2])
            fox, ml, og = ab_split(hs, ab_w_in[j], fox_fgate_b[j], mlstm_igate_b[j], mlstm_fgate_b[j])
            k_past = gather_pages(cache_fox_k[j], page_table)
            v_past = gather_pages(cache_fox_v[j], page_table)
            lf_past = gather_pages(cache_fox_logf[j], page_table)
            o_fox = fox_sample(*fox, k_past, v_past, lf_past)
            st0 = (state_mlstm_c[j].astype(F32), state_mlstm_n[j].astype(F32), state_mlstm_m[j].astype(F32))
            st, h_ml = mlstm_chunk(st0, *ml)
            os_ = ab_merge(o_fox, h_ml, og, mlstm_norm_g[j], ab_w_out[j], hs.dtype)
            fk_s.append(fox[1]); fv_s.append(fox[2]); fl_s.append(fox[3])
            mc_s.append(st[0]); mn_s.append(st[1]); mm_s.append(st[2])
        else:
            q, k, v = dil_project(hp, c_w_in[j])
            o = combine_branches([dilated_branch_prompt(q, k, v, dil, w // dil) for (w, dil) in DIL_PATTERNS])
            op = o.reshape(hp.shape[0], hp.shape[1], H_DIL * HEAD_DIM).astype(hp.dtype) @ c_w_out[j]
            keep = min(WINDOW_MAX, k.shape[1])
            wk_p.append(k[:, k.shape[1] - keep:]); wv_p.append(v[:, v.shape[1] - keep:])
            q, k, v = dil_project(hs, c_w_in[j])
            k_all = jnp.concatenate([cache_win_k[j].astype(F32), k], axis=1)
            v_all = jnp.concatenate([cache_win_v[j].astype(F32), v], axis=1)
            o = combine_branches([dilated_branch_sample(q, k_all, v_all, dil, w // dil) for (w, dil) in DIL_PATTERNS])
            os_ = o.reshape(hs.shape[0], hs.shape[1], H_DIL * HEAD_DIM).astype(hs.dtype) @ c_w_out[j]
            keep = min(WINDOW_MAX, k_all.shape[1])
            wk_s.append(k_all[:, k_all.shape[1] - keep:]); wv_s.append(v_all[:, v_all.shape[1] - keep:])
        xp = xp + mp[5][:, None, :] * op
        xs = xs + md[5][:, None, :] * os_
        xp = ffn_sublayer(xp, mp[6], mp[7], mp[8], norm_post[l], ffn_post_w13[l], ffn_post_w2[l])
        xs = ffn_sublayer(xs, md[6], md[7], md[8], norm_post[l], ffn_post_w13[l], ffn_post_w2[l])
    y_prompt = rmsnorm(xp, norm_final)
    y_sample = rmsnorm(xs, norm_final)
    return (y_prompt, y_sample,
            jnp.stack(fk_p), jnp.stack(fv_p), jnp.stack(fl_p),
            jnp.stack(fk_s), jnp.stack(fv_s), jnp.stack(fl_s),
            jnp.stack(mc_p), jnp.stack(mn_p), jnp.stack(mm_p),
            jnp.stack(mc_s), jnp.stack(mn_s), jnp.stack(mm_s),
            jnp.stack(wk_p), jnp.stack(wv_p),
            jnp.stack(wk_s), jnp.stack(wv_s))
```

```python
import functools

import jax
import jax.numpy as jnp
from jax import lax
from jax.experimental import pallas as pl
from jax.experimental.pallas import tpu as pltpu

F32 = jnp.float32
BF16 = jnp.bfloat16

HEAD_DIM = 128
H_FOX = 8
H_MLSTM = 8
H_DIL = 16
N_MOD = 9
EPS = 1e-6
FFN_RES = 0.5
QK_SCALE = HEAD_DIM ** -0.5
DIL_PATTERNS = ((128, 1), (512, 4), (2048, 16))
CHUNK = 128
NEG = -1e30

LANES = 128
MOD_ROWS = 16
VMEM_BIG = 60 * 1024 * 1024
VMEM_MID = 48 * 1024 * 1024

TM = 1024
TF = 256
TN = 512
TQ = 256


def _params(sem, vmem=VMEM_MID):
    return pltpu.CompilerParams(dimension_semantics=sem, vmem_limit_bytes=vmem)


def _sigmoid(x):
    return 1.0 / (1.0 + jnp.exp(-x))


def _log_sigmoid(x):
    return jnp.minimum(x, 0.0) - jnp.log1p(jnp.exp(-jnp.abs(x)))


def _dot(a, b):
    return jnp.dot(a, b, preferred_element_type=F32)


def _dot_nt(a, b):
    return lax.dot_general(a, b, (((1,), (1,)), ((), ())), preferred_element_type=F32)


def _norm_mod(x, g, shift, scale):
    ms = jnp.mean(x * x, axis=-1, keepdims=True)
    y = x * lax.rsqrt(ms + EPS) * g
    return y * (1.0 + scale) + shift


def _cumsum_lanes(x):
    lane = lax.broadcasted_iota(jnp.int32, x.shape, 1)
    sh = 1
    while sh < x.shape[1]:
        x = x + jnp.where(lane >= sh, pltpu.roll(x, sh, 1), 0.0)
        sh *= 2
    return x


def _mod_spec(arr, tm, seq_rows, ngrid):
    d = arr.shape[-1]
    if arr.ndim == 3:
        if ngrid == 2:
            return pl.BlockSpec((None, 1, d), lambda i, j: ((i * tm) // seq_rows, 0, 0))
        return pl.BlockSpec((None, 1, d), lambda i: ((i * tm) // seq_rows, 0, 0))
    if ngrid == 2:
        return pl.BlockSpec((tm, d), lambda i, j: (i, 0))
    return pl.BlockSpec((tm, d), lambda i: (i, 0))


def _adaln_kernel(c_ref, w_ref, b_ref, o_ref):
    c = c_ref[...]
    s = (c * _sigmoid(c)).astype(BF16)
    o_ref[...] = _dot(s, w_ref[...].astype(BF16)) + b_ref[...]


def _adaln_all(c_all, w_ada, b_ada):
    depth, d, n = w_ada.shape
    tn = 1024
    return pl.pallas_call(
        _adaln_kernel,
        grid=(depth, n // tn),
        in_specs=[pl.BlockSpec((MOD_ROWS, d), lambda l, j: (0, 0)),
                  pl.BlockSpec((None, d, tn), lambda l, j: (l, 0, j)),
                  pl.BlockSpec((None, 1, tn), lambda l, j: (l, 0, j))],
        out_specs=pl.BlockSpec((None, MOD_ROWS, tn), lambda l, j: (l, 0, j)),
        out_shape=jax.ShapeDtypeStruct((depth, MOD_ROWS, n), F32),
        compiler_params=_params(("parallel", "parallel")),
        name="adaln",
    )(c_all, w_ada, b_ada.reshape(depth, 1, n))


def _ffn_kernel(x_ref, sh_ref, sc_ref, gt_ref, g_ref, w1_ref, w3_ref, w2_ref, o_ref, h_ref,
                *, nf, tf, rem):
    f = pl.program_id(1)

    @pl.when(f == 0)
    def _():
        h_ref[...] = _norm_mod(x_ref[...], g_ref[...], sh_ref[...], sc_ref[...]).astype(BF16)

    def part(w1, w3, w2):
        h = h_ref[...]
        a = _dot(h, w1.astype(BF16))
        g = _dot(h, w3.astype(BF16))
        u = (a * _sigmoid(a) * g).astype(BF16)
        return _dot(u, w2.astype(BF16))

    @pl.when(f == 0)
    def _():
        o_ref[...] = part(w1_ref[...], w3_ref[...], w2_ref[...])

    @pl.when(jnp.logical_and(f > 0, f < nf - 1))
    def _():
        o_ref[...] += part(w1_ref[...], w3_ref[...], w2_ref[...])

    @pl.when(f == nf - 1)
    def _():
        p = part(w1_ref[:, :rem], w3_ref[:, tf - rem:], w2_ref[:rem, :])
        o_ref[...] = x_ref[...] + (FFN_RES * gt_ref[...]) * (o_ref[...] + p)


def _ffn(x, shift, scale, gate, g, w13, w2, *, tm, seq_rows):
    m, d = x.shape
    d_ff = w2.shape[0]
    tf = TF
    nf = pl.cdiv(d_ff, tf)
    rem = d_ff - (nf - 1) * tf
    assert nf >= 3 and m % tm == 0
    mod = lambda a: _mod_spec(a, tm, seq_rows, 2)
    return pl.pallas_call(
        functools.partial(_ffn_kernel, nf=nf, tf=tf, rem=rem),
        grid=(m // tm, nf),
        in_specs=[pl.BlockSpec((tm, d), lambda i, f: (i, 0)),
                  mod(shift), mod(scale), mod(gate),
                  pl.BlockSpec((1, d), lambda i, f: (0, 0)),
                  pl.BlockSpec((d, tf), lambda i, f: (0, f)),
                  pl.BlockSpec((pl.Element(d), pl.Element(tf)),
                               lambda i, f: (0, LANES * jnp.minimum(d_ff // LANES + f * (tf // LANES),
                                                                    (2 * d_ff - tf) // LANES))),
                  pl.BlockSpec((tf, d), lambda i, f: (f, 0))],
        out_specs=pl.BlockSpec((tm, d), lambda i, f: (i, 0)),
        out_shape=jax.ShapeDtypeStruct((m, d), F32),
        scratch_shapes=[pltpu.VMEM((tm, d), BF16)],
        compiler_params=_params(("parallel", "arbitrary"), VMEM_BIG),
        name="ffn",
    )(x, shift, scale, gate, g.reshape(1, d), w13, w13, w2)


def _proj_in_kernel(x_ref, sh_ref, sc_ref, g_ref, w_ref, *rest, gates):
    if gates:
        wg_ref, o_ref, og_ref, h_ref = rest
    else:
        o_ref, h_ref = rest
    n = pl.program_id(1)

    @pl.when(n == 0)
    def _():
        h = _norm_mod(x_ref[...], g_ref[...], sh_ref[...], sc_ref[...]).astype(BF16)
        h_ref[...] = h
        if gates:
            og_ref[...] = _dot(h, wg_ref[...].astype(BF16))

    o_ref[...] = _dot(h_ref[...], w_ref[...].astype(BF16))


def _proj_in(x, shift, scale, g, w, wg, *, tm, seq_rows):
    m, d = x.shape
    n = w.shape[1]
    tn = TN
    assert n % tn == 0 and m % tm == 0
    gates = wg is not None
    mod = lambda a: _mod_spec(a, tm, seq_rows, 2)
    in_specs = [pl.BlockSpec((tm, d), lambda i, j: (i, 0)), mod(shift), mod(scale),
                pl.BlockSpec((1, d), lambda i, j: (0, 0)),
                pl.BlockSpec((d, tn), lambda i, j: (0, j))]
    out_specs = [pl.BlockSpec((tm, tn), lambda i, j: (i, j))]
    out_shape = [jax.ShapeDtypeStruct((m, n), F32)]
    args = [x, shift, scale, g.reshape(1, d), w]
    if gates:
        in_specs.append(pl.BlockSpec((d, LANES), lambda i, j: (0, 0)))
        out_specs.append(pl.BlockSpec((tm, LANES), lambda i, j: (i, 0)))
        out_shape.append(jax.ShapeDtypeStruct((m, LANES), F32))
        args.append(wg)
    res = pl.pallas_call(
        functools.partial(_proj_in_kernel, gates=gates),
        grid=(m // tm, n // tn),
        in_specs=in_specs, out_specs=out_specs, out_shape=out_shape,
        scratch_shapes=[pltpu.VMEM((tm, d), BF16)],
        compiler_params=_params(("parallel", "arbitrary")),
        name="proj_in",
    )(*args)
    return (res[0], res[1]) if gates else (res[0], None)


def _proj_out_kernel(a_ref, b_ref, wa_ref, wb_ref, x_ref, gt_ref, o_ref, ab_ref, bb_ref):
    n = pl.program_id(1)

    @pl.when(n == 0)
    def _():
        ab_ref[...] = a_ref[...].astype(BF16)
        bb_ref[...] = b_ref[...].astype(BF16)

    y = _dot(ab_ref[...], wa_ref[...].astype(BF16)) + _dot(bb_ref[...], wb_ref[...].astype(BF16))
    o_ref[...] = x_ref[...] + gt_ref[...] * y


def _proj_out(mix_a, a_blk, mix_b, b_blk, w, x, gate, *, tm, seq_rows):
    m, d = x.shape
    kh = w.shape[0] // 2
    tn = TN
    if gate.ndim == 3:
        gspec = pl.BlockSpec((None, 1, tn), lambda i, j: ((i * tm) // seq_rows, 0, j))
    else:
        gspec = pl.BlockSpec((tm, tn), lambda i, j: (i, j))
    return pl.pallas_call(
        _proj_out_kernel,
        grid=(m // tm, d // tn),
        in_specs=[pl.BlockSpec((tm, kh), lambda i, j: (i, a_blk)),
                  pl.BlockSpec((tm, kh), lambda i, j: (i, b_blk)),
                  pl.BlockSpec((kh, tn), lambda i, j: (0, j)),
                  pl.BlockSpec((kh, tn), lambda i, j: (1, j)),
                  pl.BlockSpec((tm, tn), lambda i, j: (i, j)),
                  gspec],
        out_specs=pl.BlockSpec((tm, tn), lambda i, j: (i, j)),
        out_shape=jax.ShapeDtypeStruct((m, d), F32),
        scratch_shapes=[pltpu.VMEM((tm, kh), BF16), pltpu.VMEM((tm, kh), BF16)],
        compiler_params=_params(("parallel", "arbitrary")),
        name="proj_out",
    )(mix_a, mix_b, w, w, x, gate)


def _gates_kernel(p_ref, b_ref, o_ref, *, n_valid):
    s = p_ref.shape[-1]
    lane = lax.broadcasted_iota(jnp.int32, (8, LANES), 1)
    carry = jnp.zeros((8, 1), F32)
    for blk in range(s // LANES):
        sl = slice(blk * LANES, (blk + 1) * LANES)
        pre = p_ref[:, sl] + b_ref[...]
        valid = (lane + blk * LANES) < n_valid
        lf_f = jnp.where(valid, _log_sigmoid(pre[0:8]), 0.0)
        ig = jnp.where(valid, pre[8:16], NEG)
        lf_m = jnp.where(valid, _log_sigmoid(pre[16:24]), 0.0)
        c_f = _cumsum_lanes(lf_f) + carry
        carry = c_f[:, LANES - 1:LANES]
        o_ref[0:8, sl] = c_f
        o_ref[8:16, sl] = ig
        o_ref[16:24, sl] = _cumsum_lanes(lf_m)
        o_ref[24:32, sl] = lf_f


def _gates(pre_t, bias, n_valid):
    n_seq, r, s = pre_t.shape
    return pl.pallas_call(
        functools.partial(_gates_kernel, n_valid=n_valid),
        grid=(n_seq,),
        in_specs=[pl.BlockSpec((None, r, s), lambda b: (b, 0, 0)),
                  pl.BlockSpec((r, 1), lambda b: (0, 0))],
        out_specs=pl.BlockSpec((None, r, s), lambda b: (b, 0, 0)),
        out_shape=jax.ShapeDtypeStruct((n_seq, r, s), F32),
        compiler_params=_params(("parallel",)),
        name="gates",
    )(pre_t, bias)


def _fox_kernel(q_ref, k_ref, v_ref, c_ref, o_ref, *, tq):
    qi = pl.program_id(2)
    q = (q_ref[...] * QK_SCALE).astype(BF16)
    row = lax.broadcasted_iota(jnp.int32, (tq, tq), 0)
    col = lax.broadcasted_iota(jnp.int32, (tq, tq), 1)

    def block(kb, carry, masked):
        m, l, acc = carry
        ks = pl.multiple_of(kb * tq, tq)
        k = k_ref[pl.ds(ks, tq), :].astype(BF16)
        v = v_ref[pl.ds(ks, tq), :].astype(BF16)
        s = _dot_nt(q, k) - c_ref[:, pl.ds(ks, tq)]
        if masked:
            s = jnp.where(col <= row, s, -jnp.inf)
        m_new = jnp.maximum(m, jnp.max(s, axis=-1, keepdims=True))
        alpha = jnp.exp(m - m_new)
        p = jnp.exp(s - m_new)
        l = alpha * l + jnp.sum(p, axis=-1, keepdims=True)
        acc = alpha * acc + _dot(p.astype(BF16), v)
        return m_new, l, acc

    init = (jnp.full((tq, 1), -jnp.inf, F32), jnp.zeros((tq, 1), F32), jnp.zeros((tq, HEAD_DIM), F32))
    carry = lax.fori_loop(0, qi, lambda kb, c: block(kb, c, False), init)
    m, l, acc = block(qi, carry, True)
    o_ref[...] = acc / l


def _fox_prompt(proj, gt4, n_seq, s):
    tq = TQ
    nq = s // tq
    return pl.pallas_call(
        functools.partial(_fox_kernel, tq=tq),
        grid=(n_seq, H_FOX, nq),
        in_specs=[pl.BlockSpec((tq, HEAD_DIM), lambda b, h, q: (b * nq + q, h)),
                  pl.BlockSpec((s, HEAD_DIM), lambda b, h, q: (b, H_FOX + h)),
                  pl.BlockSpec((s, HEAD_DIM), lambda b, h, q: (b, 2 * H_FOX + h)),
                  pl.BlockSpec((None, None, 1, s), lambda b, h, q: (b, h, 0, 0))],
        out_specs=pl.BlockSpec((tq, HEAD_DIM), lambda b, h, q: (b * nq + q, h)),
        out_shape=jax.ShapeDtypeStruct((n_seq * s, H_FOX * HEAD_DIM), F32),
        compiler_params=_params(("parallel", "parallel", "arbitrary")),
        name="fox_prompt",
    )(proj, proj, proj, gt4)


def _fox_decode_kernel(pt_ref, q_ref, k_ref, v_ref, lf_ref, kn_ref, vn_ref, lfn_ref, o_ref,
                       m_ref, l_ref, acc_ref, c_ref, *, n_pages, t_new):
    del pt_ref
    p = pl.program_id(1)
    rows = q_ref.shape[0]

    @pl.when(p == 0)
    def _():
        m_ref[...] = jnp.full(m_ref.shape, -jnp.inf, F32)
        l_ref[...] = jnp.zeros(l_ref.shape, F32)
        acc_ref[...] = jnp.zeros(acc_ref.shape, F32)
        c_ref[...] = jnp.zeros(c_ref.shape, F32)

    def process(k, v, lf, new):
        q = (q_ref[...] * QK_SCALE).astype(BF16)
        c = _cumsum_lanes(lf) + c_ref[...]
        c_ref[...] = c[:, LANES - 1:LANES]
        bias = jnp.concatenate([c] * (rows // H_FOX), axis=0)
        s = _dot_nt(q, k.astype(BF16)) - bias
        if new:
            tok = lax.shift_right_logical(lax.broadcasted_iota(jnp.int32, s.shape, 0),
                                          H_FOX.bit_length() - 1)
            key = lax.broadcasted_iota(jnp.int32, s.shape, 1)
            s = jnp.where(key <= tok, s, -jnp.inf)
        m_old = m_ref[...]
        m_new = jnp.maximum(m_old, jnp.max(s, axis=-1, keepdims=True))
        alpha = jnp.exp(m_old - m_new)
        pr = jnp.exp(s - m_new)
        l_ref[...] = alpha * l_ref[...] + jnp.sum(pr, axis=-1, keepdims=True)
        acc_ref[...] = alpha * acc_ref[...] + _dot(pr.astype(BF16), v.astype(BF16))
        m_ref[...] = m_new

    @pl.when(p < n_pages)
    def _():
        process(k_ref[...], v_ref[...], lf_ref[...], False)

    @pl.when(p == n_pages)
    def _():
        assert t_new <= LANES
        process(kn_ref[...], vn_ref[...], lfn_ref[...], True)
        o_ref[...] = acc_ref[...] / l_ref[...]


def _fox_decode(page_table, q_bd, k_pool, v_pool, lf_pool_t, k_new, v_new, lf_new_t, t_new):
    db, rows, w = q_bd.shape
    n_pages = page_table.shape[1]
    page = k_pool.shape[1]
    pool_map = lambda b, p, pt: (pt[b, jnp.minimum(p, n_pages - 1)], 0, 0)
    seq_map = lambda b, p, pt: (b, 0, 0)
    grid_spec = pltpu.PrefetchScalarGridSpec(
        num_scalar_prefetch=1,
        grid=(db, n_pages + 1),
        in_specs=[pl.BlockSpec((None, rows, w), seq_map),
                  pl.BlockSpec((None, page, w), pool_map),
                  pl.BlockSpec((None, page, w), pool_map),
                  pl.BlockSpec((None, H_FOX, page), pool_map),
                  pl.BlockSpec((None, page, w), seq_map),
                  pl.BlockSpec((None, page, w), seq_map),
                  pl.BlockSpec((None, H_FOX, page), seq_map)],
        out_specs=pl.BlockSpec((None, rows, w), seq_map),
        scratch_shapes=[pltpu.VMEM((rows, 1), F32), pltpu.VMEM((rows, 1), F32),
                        pltpu.VMEM((rows, w), F32), pltpu.VMEM((H_FOX, 1), F32)])
    return pl.pallas_call(
        functools.partial(_fox_decode_kernel, n_pages=n_pages, t_new=t_new),
        grid_spec=grid_spec,
        out_shape=jax.ShapeDtypeStruct((db, rows, w), F32),
        compiler_params=_params(("parallel", "arbitrary")),
        name="fox_decode",
    )(page_table, q_bd, k_pool, v_pool, lf_pool_t, k_new, v_new, lf_new_t)


def _mlstm_kernel(q_ref, k_ref, v_ref, og_ref, ig_ref, b_ref, c0_ref, n0_ref, m0_ref, g_ref,
                  hn_ref, c_ref, n_ref, m_ref):
    s = q_ref.shape[0]
    ln = CHUNK
    row = lax.broadcasted_iota(jnp.int32, (ln, ln), 0)
    col = lax.broadcasted_iota(jnp.int32, (ln, ln), 1)
    causal = col <= row

    def chunk(ci, carry):
        c_st, n_st, m_st = carry
        st = pl.multiple_of(ci * ln, ln)
        q = q_ref[pl.ds(st, ln), :]
        k = k_ref[pl.ds(st, ln), :] * QK_SCALE
        v = v_ref[pl.ds(st, ln), :]
        b_row = b_ref[:, pl.ds(st, ln)]
        ig_row = ig_ref[:, pl.ds(st, ln)]
        b_mat = jnp.broadcast_to(b_row, (ln, ln))
        b_col = b_mat.T
        dmat = jnp.where(causal, b_col - b_mat + ig_row, NEG)
        inter = b_col[:, 0:1] + m_st
        m_tok = jnp.maximum(inter, jnp.max(dmat, axis=-1, keepdims=True))
        w_inter = jnp.exp(inter - m_tok)
        qb = q.astype(BF16)
        k_t = k.T
        vb = v.astype(BF16)
        qk = _dot(qb, k_t.astype(BF16)) * jnp.exp(dmat - m_tok)
        num = w_inter * _dot(qb, c_st.astype(BF16)) + _dot(qk.astype(BF16), vb)
        qn = _dot(qb, jnp.broadcast_to(n_st, (HEAD_DIM, LANES)).astype(BF16))[:, 0:1]
        den = w_inter * qn + jnp.sum(qk, axis=-1, keepdims=True)
        h = num / jnp.maximum(jnp.abs(den), jnp.exp(-m_tok))
        hn = h * lax.rsqrt(jnp.mean(h * h, axis=-1, keepdims=True) + EPS) * g_ref[...]
        hn_ref[pl.ds(st, ln), :] = hn * _sigmoid(og_ref[pl.ds(st, ln), :])
        b_last = b_row[:, ln - 1:ln]
        g_row = b_last - b_row + ig_row
        m_new = jnp.maximum(b_last + m_st, jnp.max(g_row, axis=-1, keepdims=True))
        a_prev = jnp.exp(b_last + m_st - m_new)
        kw = k_t * jnp.exp(g_row - m_new)
        c_new = a_prev * c_st + _dot(kw.astype(BF16), vb)
        n_new = a_prev * n_st + jnp.sum(kw, axis=-1, keepdims=True)
        return c_new, n_new, m_new

    c_st, n_st, m_st = lax.fori_loop(0, s // ln, chunk, (c0_ref[...], n0_ref[...], m0_ref[...]))
    c_ref[...] = c_st
    n_ref[...] = n_st
    m_ref[...] = m_st


def _mlstm(proj, gt4, c0, n0, m0, norm_g, n_seq, s):
    h = H_MLSTM
    col = lambda base: pl.BlockSpec((s, HEAD_DIM), lambda b, j: (b, base + j))
    row = lambda base: pl.BlockSpec((None, None, 1, s), lambda b, j: (b, base + j, 0, 0))
    st = lambda shape: pl.BlockSpec((None, None) + shape, lambda b, j: (b, j, 0, 0))
    return pl.pallas_call(
        _mlstm_kernel,
        grid=(n_seq, h),
        in_specs=[col(24), col(32), col(40), col(48), row(8), row(16),
                  st((HEAD_DIM, HEAD_DIM)), st((HEAD_DIM, 1)), st((1, 1)),
                  pl.BlockSpec((None, 1, HEAD_DIM), lambda b, j: (j, 0, 0))],
        out_specs=[pl.BlockSpec((s, HEAD_DIM), lambda b, j: (b, j)),
                   st((HEAD_DIM, HEAD_DIM)), st((HEAD_DIM, 1)), st((1, 1))],
        out_shape=[jax.ShapeDtypeStruct((n_seq * s, h * HEAD_DIM), F32),
                   jax.ShapeDtypeStruct((n_seq, h, HEAD_DIM, HEAD_DIM), F32),
                   jax.ShapeDtypeStruct((n_seq, h, HEAD_DIM, 1), F32),
                   jax.ShapeDtypeStruct((n_seq, h, 1, 1), F32)],
        compiler_params=_params(("parallel", "parallel")),
        name="mlstm",
    )(proj, proj, proj, proj, gt4, gt4, c0, n0, m0, norm_g.reshape(h, 1, HEAD_DIM))


def _dil_kernel(q_ref, k_ref, v_ref, o_ref, m_ref, l_ref, acc_ref):
    s = q_ref.shape[0]
    blk = CHUNK
    m_ref[...] = jnp.full(m_ref.shape, -jnp.inf, F32)
    l_ref[...] = jnp.zeros(l_ref.shape, F32)
    acc_ref[...] = jnp.zeros(acc_ref.shape, F32)
    row = lax.broadcasted_iota(jnp.int32, (blk, blk), 0)
    col = lax.broadcasted_iota(jnp.int32, (blk, blk), 1)

    def rows(start, d):
        return pl.ds(start, blk) if d == 1 else pl.ds(start, blk, stride=d)

    def update(qs, ks, d, prev):
        rq, rk = rows(qs, d), rows(ks, d)
        q = (q_ref[rq, :] * QK_SCALE).astype(BF16)
        k = k_ref[rk, :].astype(BF16)
        v = v_ref[rk, :].astype(BF16)
        sc = _dot_nt(q, k)
        sc = jnp.where((col >= row) if prev else (col <= row), sc, -jnp.inf)
        m_old = m_ref[rq, :]
        m_new = jnp.maximum(m_old, jnp.max(sc, axis=-1, keepdims=True))
        alpha = jnp.exp(m_old - m_new)
        p = jnp.exp(sc - m_new)
        l_ref[rq, :] = alpha * l_ref[rq, :] + jnp.sum(p, axis=-1, keepdims=True)
        acc_ref[rq, :] = alpha * acc_ref[rq, :] + _dot(p.astype(BF16), v)
        m_ref[rq, :] = m_new

    for window, d in DIL_PATTERNS:
        assert window // d == blk and s % (d * blk) == 0
        for r in range(d):
            for n in range(s // (d * blk)):
                start = r + d * blk * n
                update(start, start, d, False)
                if n > 0:
                    update(start, start - d * blk, d, True)
    o_ref[...] = acc_ref[...] / l_ref[...]


def _dil_prompt(proj, n_seq, s):
    h = H_DIL
    col = lambda base: pl.BlockSpec((s, HEAD_DIM), lambda b, j: (b, base + j))
    return pl.pallas_call(
        _dil_kernel,
        grid=(n_seq, h),
        in_specs=[col(0), col(h), col(2 * h)],
        out_specs=col(0),
        out_shape=jax.ShapeDtypeStruct((n_seq * s, h * HEAD_DIM), F32),
        scratch_shapes=[pltpu.VMEM((s, LANES), F32), pltpu.VMEM((s, LANES), F32),
                        pltpu.VMEM((s, HEAD_DIM), F32)],
        compiler_params=_params(("parallel", "parallel")),
        name="dil_prompt",
    )(proj, proj, proj)


def _dil_decode_kernel(q_ref, k_ref, v_ref, kn_ref, vn_ref, o_ref, m_ref, l_ref, acc_ref,
                       *, n_tiles, tk, cb):
    p = pl.program_id(1)

    @pl.when(p == 0)
    def _():
        m_ref[...] = jnp.full(m_ref.shape, -jnp.inf, F32)
        l_ref[...] = jnp.zeros(l_ref.shape, F32)
        acc_ref[...] = jnp.zeros(acc_ref.shape, F32)

    def process(k, v, key0):
        q = (q_ref[...] * QK_SCALE).astype(BF16)
        s = _dot_nt(q, k.astype(BF16))
        tok = lax.shift_right_logical(lax.broadcasted_iota(jnp.int32, s.shape, 0),
                                      H_DIL.bit_length() - 1)
        key = lax.broadcasted_iota(jnp.int32, s.shape, 1) + key0
        delta = cb + tok - key
        cnt = jnp.zeros(s.shape, F32)
        for window, d in DIL_PATTERNS:
            assert d & (d - 1) == 0
            hit = jnp.where((delta & (d - 1)) == 0, jnp.where(delta <= window, 1.0, 0.0), 0.0)
            cnt = cnt + jnp.where(delta >= 0, hit, 0.0)
        s = jnp.where(cnt > 0.0, s, -jnp.inf)
        m_old = m_ref[...]
        m_new = jnp.maximum(m_old, jnp.max(s, axis=-1, keepdims=True))
        m_safe = jnp.where(m_new == -jnp.inf, 0.0, m_new)
        alpha = jnp.exp(m_old - m_safe)
        pr = cnt * jnp.exp(s - m_safe)
        l_ref[...] = alpha * l_ref[...] + jnp.sum(pr, axis=-1, keepdims=True)
        acc_ref[...] = alpha * acc_ref[...] + _dot(pr.astype(BF16), v.astype(BF16))
        m_ref[...] = m_new

    @pl.when(p < n_tiles)
    def _():
        process(k_ref[...], v_ref[...], p * tk)

    @pl.when(p == n_tiles)
    def _():
        process(kn_ref[...], vn_ref[...], cb)
        o_ref[...] = acc_ref[...] / l_ref[...]


def _dil_decode(q_bd, k_cache, v_cache, k_new, v_new):
    db, rows, w = q_bd.shape
    cb = k_cache.shape[1]
    tk = 512
    n_tiles = cb // tk
    pad = k_new.shape[1]
    seq_map = lambda b, p: (b, 0, 0)
    tile_map = lambda b, p: (b, jnp.minimum(p, n_tiles - 1), 0)
    return pl.pallas_call(
        functools.partial(_dil_decode_kernel, n_tiles=n_tiles, tk=tk, cb=cb),
        grid=(db, n_tiles + 1),
        in_specs=[pl.BlockSpec((None, rows, w), seq_map),
                  pl.BlockSpec((None, tk, w), tile_map),
                  pl.BlockSpec((None, tk, w), tile_map),
                  pl.BlockSpec((None, pad, w), seq_map),
                  pl.BlockSpec((None, pad, w), seq_map)],
        out_specs=pl.BlockSpec((None, rows, w), seq_map),
        out_shape=jax.ShapeDtypeStruct((db, rows, w), F32),
        scratch_shapes=[pltpu.VMEM((rows, 1), F32), pltpu.VMEM((rows, 1), F32),
                        pltpu.VMEM((rows, w), F32)],
        compiler_params=_params(("parallel", "arbitrary")),
        name="dil_decode",
    )(q_bd, k_cache, v_cache, k_new, v_new)


def _final_norm_kernel(x_ref, g_ref, o_ref):
    x = x_ref[...]
    o_ref[...] = x * lax.rsqrt(jnp.mean(x * x, axis=-1, keepdims=True) + EPS) * g_ref[...]


def _final_norm(x, g, tm):
    m, d = x.shape
    return pl.pallas_call(
        _final_norm_kernel,
        grid=(m // tm,),
        in_specs=[pl.BlockSpec((tm, d), lambda i: (i, 0)), pl.BlockSpec((1, d), lambda i: (0, 0))],
        out_specs=pl.BlockSpec((tm, d), lambda i: (i, 0)),
        out_shape=jax.ShapeDtypeStruct((m, d), F32),
        compiler_params=_params(("parallel",)),
        name="final_norm",
    )(x, g.reshape(1, d))


def _block_diag_queries(q, n_heads):
    db, t, w = q.shape
    head_of_col = jnp.arange(w, dtype=jnp.int32) // HEAD_DIM
    keep = head_of_col[None, :] == jnp.arange(n_heads, dtype=jnp.int32)[:, None]
    out = jnp.where(keep[None, None], q[:, :, None, :], 0.0)
    return out.reshape(db, t * n_heads, w)


def _diag_heads(o, t, n_heads):
    db = o.shape[0]
    o5 = o.reshape(db, t, n_heads, n_heads, HEAD_DIM)
    idx = jnp.arange(n_heads)
    return o5[:, :, idx, idx, :].reshape(db * t, n_heads * HEAD_DIM)


def _pad_tokens(a, db, t, pad):
    w = a.shape[-1]
    return jnp.pad(a.reshape(db, t, w), ((0, 0), (0, pad - t), (0, 0)))


def kernel(x_prompt, x_sample, c_prompt, c_sample, page_table, cache_fox_k, cache_fox_v, cache_fox_logf, state_mlstm_c, state_mlstm_n, state_mlstm_m, cache_win_k, cache_win_v, w_ada, b_ada, norm_pre, norm_mix, norm_post, norm_final, ffn_pre_w13, ffn_pre_w2, ffn_post_w13, ffn_post_w2, ab_w_in, ab_w_out, fox_fgate_b, mlstm_igate_b, mlstm_fgate_b, mlstm_norm_g, c_w_in, c_w_out):
    nb, s, d = x_prompt.shape
    db, t, _ = x_sample.shape
    depth = w_ada.shape[0]
    ms = db * t
    fw = H_FOX * HEAD_DIM
    mw = H_MLSTM * HEAD_DIM
    assert nb + db <= MOD_ROWS and s % TM == 0

    xp = x_prompt.reshape(nb * s, d)
    xs = x_sample.reshape(ms, d)
    c_all = jnp.concatenate([c_prompt, c_sample, jnp.zeros((MOD_ROWS - nb - db, d), F32)], axis=0)
    mods = _adaln_all(c_all, w_ada, b_ada).reshape(depth, MOD_ROWS, N_MOD, d)

    n_pool, page = cache_fox_k.shape[1], cache_fox_k.shape[2]
    zeros_c = jnp.zeros((nb, H_MLSTM, HEAD_DIM, HEAD_DIM), F32)
    zeros_n = jnp.zeros((nb, H_MLSTM, HEAD_DIM, 1), F32)
    zeros_m = jnp.zeros((nb, H_MLSTM, 1, 1), F32)

    outs = {name: [] for name in ("fk_p", "fv_p", "fl_p", "fk_s", "fv_s", "fl_s", "mc_p", "mn_p", "mm_p",
                                  "mc_s", "mn_s", "mm_s", "wk_p", "wv_p", "wk_s", "wv_s")}

    for l in range(depth):
        j = l // 2
        mp = [mods[l, :nb, i][:, None, :] for i in range(N_MOD)]
        md = [jnp.repeat(mods[l, nb:nb + db, i], t, axis=0) for i in range(N_MOD)]
        prompt = dict(tm=TM, seq_rows=s)
        sample = dict(tm=ms, seq_rows=None)

        xp = _ffn(xp, mp[0], mp[1], mp[2], norm_pre[l], ffn_pre_w13[l], ffn_pre_w2[l], **prompt)
        xs = _ffn(xs, md[0], md[1], md[2], norm_pre[l], ffn_pre_w13[l], ffn_pre_w2[l], **sample)

        if l % 2 == 0:
            w = ab_w_in[j]
            o1, o2, o3 = 3 * fw, 3 * fw + H_FOX, 3 * fw + H_FOX + 3 * mw
            o4 = o3 + 2 * H_MLSTM
            w_main = jnp.concatenate([w[:, :o1], w[:, o2:o3], w[:, o4:]], axis=1)
            w_gate = jnp.concatenate([w[:, o1:o2], w[:, o3:o4],
                                      jnp.zeros((d, LANES - H_FOX - 2 * H_MLSTM), F32)], axis=1)
            bias = jnp.concatenate([fox_fgate_b[j], mlstm_igate_b[j], mlstm_fgate_b[j],
                                    jnp.zeros((8,), F32)]).reshape(32, 1)

            pp, gp = _proj_in(xp, mp[3], mp[4], norm_mix[l], w_main, w_gate, **prompt)
            gt = _gates(gp.reshape(nb, s, LANES)[:, :, :32].transpose(0, 2, 1), bias, s)
            gt4 = gt.reshape(nb, 32, 1, s)
            o_fox = _fox_prompt(pp, gt4, nb, s)
            hn, c_p, n_p, m_p = _mlstm(pp, gt4, zeros_c, zeros_n, zeros_m, mlstm_norm_g[j], nb, s)
            xp = _proj_out(o_fox, 0, hn, 0, ab_w_out[j], xp, mp[5], **prompt)
            outs["fk_p"].append(pp[:, fw:2 * fw].reshape(nb, s, H_FOX, HEAD_DIM))
            outs["fv_p"].append(pp[:, 2 * fw:3 * fw].reshape(nb, s, H_FOX, HEAD_DIM))
            outs["fl_p"].append(gt[:, 24:32, :].transpose(0, 2, 1))
            outs["mc_p"].append(c_p)
            outs["mn_p"].append(n_p.reshape(nb, H_MLSTM, HEAD_DIM))
            outs["mm_p"].append(m_p.reshape(nb, H_MLSTM))

            ps, gs = _proj_in(xs, md[3], md[4], norm_mix[l], w_main, w_gate, **sample)
            gts = _gates(_pad_tokens(gs, db, t, CHUNK)[:, :, :32].transpose(0, 2, 1), bias, t)
            gts4 = gts.reshape(db, 32, 1, CHUNK)
            ps_pad = _pad_tokens(ps, db, t, CHUNK)
            q_bd = _block_diag_queries(ps[:, :fw].reshape(db, t, fw), H_FOX)
            o_dec = _fox_decode(page_table, q_bd,
                                cache_fox_k[j].reshape(n_pool, page, fw),
                                cache_fox_v[j].reshape(n_pool, page, fw),
                                cache_fox_logf[j].transpose(0, 2, 1),
                                ps_pad[:, :, fw:2 * fw], ps_pad[:, :, 2 * fw:3 * fw],
                                gts[:, 24:32, :], t)
            o_fox_s = _diag_heads(o_dec, t, H_FOX)
            hn_s, c_s, n_s, m_s = _mlstm(ps_pad.reshape(db * CHUNK, -1), gts4,
                                         state_mlstm_c[j],
                                         state_mlstm_n[j].reshape(db, H_MLSTM, HEAD_DIM, 1),
                                         state_mlstm_m[j].reshape(db, H_MLSTM, 1, 1),
                                         mlstm_norm_g[j], db, CHUNK)
            hn_s = hn_s.reshape(db, CHUNK, mw)[:, :t].reshape(ms, mw)
            xs = _proj_out(o_fox_s, 0, hn_s, 0, ab_w_out[j], xs, md[5], **sample)
            outs["fk_s"].append(ps[:, fw:2 * fw].reshape(db, t, H_FOX, HEAD_DIM))
            outs["fv_s"].append(ps[:, 2 * fw:3 * fw].reshape(db, t, H_FOX, HEAD_DIM))
            outs["fl_s"].append(gts[:, 24:32, :t].transpose(0, 2, 1))
            outs["mc_s"].append(c_s)
            outs["mn_s"].append(n_s.reshape(db, H_MLSTM, HEAD_DIM))
            outs["mm_s"].append(m_s.reshape(db, H_MLSTM))
        else:
            dw = H_DIL * HEAD_DIM
            pp, _ = _proj_in(xp, mp[3], mp[4], norm_mix[l], c_w_in[j], None, **prompt)
            o_dil = _dil_prompt(pp, nb, s)
            xp = _proj_out(o_dil, 0, o_dil, 1, c_w_out[j], xp, mp[5], **prompt)
            keep = min(DIL_PATTERNS[-1][0], s)
            outs["wk_p"].append(pp[:, dw:2 * dw].reshape(nb, s, H_DIL, HEAD_DIM)[:, s - keep:])
            outs["wv_p"].append(pp[:, 2 * dw:].reshape(nb, s, H_DIL, HEAD_DIM)[:, s - keep:])

            ps, _ = _proj_in(xs, md[3], md[4], norm_mix[l], c_w_in[j], None, **sample)
            cb = cache_win_k.shape[2]
            k_new = ps[:, dw:2 * dw]
            v_new = ps[:, 2 * dw:]
            q_bd = _block_diag_queries(ps[:, :dw].reshape(db, t, dw), H_DIL)
            o_dec = _dil_decode(q_bd, cache_win_k[j].reshape(db, cb, dw), cache_win_v[j].reshape(db, cb, dw),
                                _pad_tokens(k_new, db, t, CHUNK), _pad_tokens(v_new, db, t, CHUNK))
            o_dil_s = _diag_heads(o_dec, t, H_DIL)
            xs = _proj_out(o_dil_s, 0, o_dil_s, 1, c_w_out[j], xs, md[5], **sample)
            k_all = jnp.concatenate([cache_win_k[j], k_new.reshape(db, t, H_DIL, HEAD_DIM)], axis=1)
            v_all = jnp.concatenate([cache_win_v[j], v_new.reshape(db, t, H_DIL, HEAD_DIM)], axis=1)
            keep = min(DIL_PATTERNS[-1][0], cb + t)
            outs["wk_s"].append(k_all[:, cb + t - keep:])
            outs["wv_s"].append(v_all[:, cb + t - keep:])

        xp = _ffn(xp, mp[6], mp[7], mp[8], norm_post[l], ffn_post_w13[l], ffn_post_w2[l], **prompt)
        xs = _ffn(xs, md[6], md[7], md[8], norm_post[l], ffn_post_w13[l], ffn_post_w2[l], **sample)

    y_prompt = _final_norm(xp, norm_final, TM).reshape(nb, s, d)
    y_sample = _final_norm(xs, norm_final, ms).reshape(db, t, d)
    st = lambda name: jnp.stack(outs[name])
    return (y_prompt, y_sample,
            st("fk_p"), st("fv_p"), st("fl_p"), st("fk_s"), st("fv_s"), st("fl_s"),
            st("mc_p"), st("mn_p"), st("mm_p"), st("mc_s"), st("mn_s"), st("mm_s"),
            st("wk_p"), st("wv_p"), st("wk_s"), st("wv_s"))
```

```python
import functools

import jax
import jax.numpy as jnp
from jax import lax
from jax.experimental import pallas as pl
from jax.experimental.pallas import tpu as pltpu

F32 = jnp.float32
BF16 = jnp.bfloat16

HEAD_DIM = 128
H_FOX = 8
H_MLSTM = 8
H_DIL = 16
N_MOD = 9
EPS = 1e-6
FFN_RES = 0.5
QK_SCALE = HEAD_DIM ** -0.5
DIL_PATTERNS = ((128, 1), (512, 4), (2048, 16))
CHUNK = 128
NEG = -1e30

LANES = 128
SUBLANES = 8
VMEM_BIG = 60 * 1024 * 1024
VMEM_MID = 48 * 1024 * 1024

TM = 1024
TF = 256
TN = 512
TQ = 256
MLSTM_HEADS = 2
PAGES_PER_STEP = 8
DIL_TK = 512
NEW_KEYS = 16


def _params(sem, vmem=VMEM_MID):
    return pltpu.CompilerParams(dimension_semantics=sem, vmem_limit_bytes=vmem)


def _sigmoid(x):
    return 1.0 / (1.0 + jnp.exp(-x))


def _log_sigmoid(x):
    return jnp.minimum(x, 0.0) - jnp.log1p(jnp.exp(-jnp.abs(x)))


def _dot(a, b):
    return jnp.dot(a, b, preferred_element_type=F32)


def _dot_nt(a, b):
    return lax.dot_general(a, b, (((1,), (1,)), ((), ())), preferred_element_type=F32)


def _rms(x):
    return x * lax.rsqrt(jnp.mean(x * x, axis=-1, keepdims=True) + EPS)


def _norm_mod(x, g, shift, scale):
    return _rms(x) * g * (1.0 + scale) + shift


def _cumsum_lanes(x):
    lane = lax.broadcasted_iota(jnp.int32, x.shape, 1)
    sh = 1
    while sh < x.shape[1]:
        x = x + jnp.where(lane >= sh, pltpu.roll(x, sh, 1), 0.0)
        sh *= 2
    return x


class _Mods:
    def __init__(self, arr, layer, prompt, tm, n_seq=None, seq_rows=None):
        self.arr, self.layer, self.prompt, self.tm, self.n_seq, self.seq_rows = arr, layer, prompt, tm, n_seq, seq_rows

    def spec(self, k, tn=None):
        d = self.arr.shape[-1] if self.prompt else self.arr.shape[-1] // N_MOD
        l, tm = self.layer, self.tm
        if self.prompt:
            n_seq, seq_rows = self.n_seq, self.seq_rows
            row = lambda i: (l * n_seq + (i * tm) // seq_rows) * N_MOD + k
            if tn is None:
                return pl.BlockSpec((None, 1, d), lambda i, j: (row(i), 0, 0))
            return pl.BlockSpec((None, 1, tn), lambda i, j: (row(i), 0, j))
        if tn is None:
            return pl.BlockSpec((None, tm, d), lambda i, j: (l, 0, k))
        return pl.BlockSpec((None, tm, tn), lambda i, j: (l, 0, k * (d // tn) + j))


def _adaln_kernel(c_ref, w_ref, b_ref, o_ref):
    c = c_ref[...]
    s = (c * _sigmoid(c)).astype(BF16)
    o_ref[...] = _dot(s, w_ref[...].astype(BF16)) + b_ref[...]


def _adaln_all(c_rows, w_ada, b_ada):
    depth, d, n = w_ada.shape
    rows = c_rows.shape[0]
    tn = 1024
    return pl.pallas_call(
        _adaln_kernel,
        grid=(depth, n // tn),
        in_specs=[pl.BlockSpec((rows, d), lambda l, j: (0, 0)),
                  pl.BlockSpec((None, d, tn), lambda l, j: (l, 0, j)),
                  pl.BlockSpec((None, 1, tn), lambda l, j: (l, 0, j))],
        out_specs=pl.BlockSpec((None, rows, tn), lambda l, j: (l, 0, j)),
        out_shape=jax.ShapeDtypeStruct((depth, rows, n), F32),
        compiler_params=_params(("parallel", "parallel")),
        name="adaln",
    )(c_rows, w_ada, b_ada.reshape(depth, 1, n))


def _ffn_kernel(x_ref, sh_ref, sc_ref, gt_ref, g_ref, w1_ref, w3_ref, w2_ref, *rest, nf, tf, rem, final):
    if final:
        gf_ref, o_ref, h_ref = rest
    else:
        o_ref, h_ref = rest
    f = pl.program_id(1)

    @pl.when(f == 0)
    def _():
        h_ref[...] = _norm_mod(x_ref[...], g_ref[...], sh_ref[...], sc_ref[...]).astype(BF16)

    def part(w1, w3, w2):
        h = h_ref[...]
        a = _dot(h, w1.astype(BF16))
        g = _dot(h, w3.astype(BF16))
        u = (a * _sigmoid(a) * g).astype(BF16)
        return _dot(u, w2.astype(BF16))

    @pl.when(f == 0)
    def _():
        o_ref[...] = part(w1_ref[...], w3_ref[...], w2_ref[...])

    @pl.when(jnp.logical_and(f > 0, f < nf - 1))
    def _():
        o_ref[...] += part(w1_ref[...], w3_ref[...], w2_ref[...])

    @pl.when(f == nf - 1)
    def _():
        p = part(w1_ref[:, :rem], w3_ref[:, tf - rem:], w2_ref[:rem, :])
        y = x_ref[...] + (FFN_RES * gt_ref[...]) * (o_ref[...] + p)
        if final:
            y = _rms(y) * gf_ref[...]
        o_ref[...] = y


def _ffn(x, mods, ks, g_all, w13_all, w2_all, layer, final_g=None):
    m, d = x.shape
    tm = mods.tm
    d_ff = w2_all.shape[1]
    tf = TF
    nf = pl.cdiv(d_ff, tf)
    rem = d_ff - (nf - 1) * tf
    assert nf >= 3 and m % tm == 0 and d_ff % LANES == 0
    final = final_g is not None
    in_specs = [pl.BlockSpec((tm, d), lambda i, f: (i, 0)),
                mods.spec(ks[0]), mods.spec(ks[1]), mods.spec(ks[2]),
                pl.BlockSpec((None, 1, d), lambda i, f: (layer, 0, 0)),
                pl.BlockSpec((None, d, tf), lambda i, f: (layer, 0, f)),
                pl.BlockSpec((None, pl.Element(d), pl.Element(tf)),
                             lambda i, f: (layer, 0, LANES * jnp.minimum(d_ff // LANES + f * (tf // LANES),
                                                                        (2 * d_ff - tf) // LANES))),
                pl.BlockSpec((None, tf, d), lambda i, f: (layer, f, 0))]
    args = [x, mods.arr, mods.arr, mods.arr, g_all, w13_all, w13_all, w2_all]
    if final:
        in_specs.append(pl.BlockSpec((1, d), lambda i, f: (0, 0)))
        args.append(final_g.reshape(1, d))
    return pl.pallas_call(
        functools.partial(_ffn_kernel, nf=nf, tf=tf, rem=rem, final=final),
        grid=(m // tm, nf),
        in_specs=in_specs,
        out_specs=pl.BlockSpec((tm, d), lambda i, f: (i, 0)),
        out_shape=jax.ShapeDtypeStruct((m, d), F32),
        scratch_shapes=[pltpu.VMEM((tm, d), BF16)],
        compiler_params=_params(("parallel", "arbitrary"), VMEM_BIG),
        name="ffn",
    )(*args)


def _proj_in_kernel(x_ref, sh_ref, sc_ref, g_ref, w_ref, *rest, gates):
    if gates:
        wg_ref, o_ref, og_ref, h_ref = rest
    else:
        o_ref, h_ref = rest
    n = pl.program_id(1)

    @pl.when(n == 0)
    def _():
        h = _norm_mod(x_ref[...], g_ref[...], sh_ref[...], sc_ref[...]).astype(BF16)
        h_ref[...] = h
        if gates:
            og_ref[...] = _dot(h, wg_ref[...].astype(BF16))

    o_ref[...] = _dot(h_ref[...], w_ref[...].astype(BF16))


def _proj_in(x, mods, ks, g_all, layer, w_all, wg_all, j):
    m, d = x.shape
    tm = mods.tm
    n = w_all.shape[2]
    tn = TN
    assert n % tn == 0 and m % tm == 0
    gates = wg_all is not None
    in_specs = [pl.BlockSpec((tm, d), lambda i, c: (i, 0)), mods.spec(ks[0]), mods.spec(ks[1]),
                pl.BlockSpec((None, 1, d), lambda i, c: (layer, 0, 0)),
                pl.BlockSpec((None, d, tn), lambda i, c: (j, 0, c))]
    out_specs = [pl.BlockSpec((tm, tn), lambda i, c: (i, c))]
    out_shape = [jax.ShapeDtypeStruct((m, n), F32)]
    args = [x, mods.arr, mods.arr, g_all, w_all]
    if gates:
        in_specs.append(pl.BlockSpec((None, d, LANES), lambda i, c: (j, 0, 0)))
        out_specs.append(pl.BlockSpec((tm, LANES), lambda i, c: (i, 0)))
        out_shape.append(jax.ShapeDtypeStruct((m, LANES), F32))
        args.append(wg_all)
    res = pl.pallas_call(
        functools.partial(_proj_in_kernel, gates=gates),
        grid=(m // tm, n // tn),
        in_specs=in_specs, out_specs=out_specs, out_shape=out_shape,
        scratch_shapes=[pltpu.VMEM((tm, d), BF16)],
        compiler_params=_params(("parallel", "arbitrary")),
        name="proj_in",
    )(*args)
    return (res[0], res[1]) if gates else (res[0], None)


def _proj_out_kernel(a_ref, b_ref, wa_ref, wb_ref, x_ref, gt_ref, o_ref, ab_ref, bb_ref):
    n = pl.program_id(1)

    @pl.when(n == 0)
    def _():
        ab_ref[...] = a_ref[...].astype(BF16)
        bb_ref[...] = b_ref[...].astype(BF16)

    y = _dot(ab_ref[...], wa_ref[...].astype(BF16)) + _dot(bb_ref[...], wb_ref[...].astype(BF16))
    o_ref[...] = x_ref[...] + gt_ref[...] * y


def _proj_out(mix_a, a_blk, mix_b, b_blk, w_all, j, x, mods, k):
    m, d = x.shape
    tm = mods.tm
    kh = w_all.shape[1] // 2
    tn = TN
    return pl.pallas_call(
        _proj_out_kernel,
        grid=(m // tm, d // tn),
        in_specs=[pl.BlockSpec((tm, kh), lambda i, c: (i, a_blk)),
                  pl.BlockSpec((tm, kh), lambda i, c: (i, b_blk)),
                  pl.BlockSpec((None, kh, tn), lambda i, c: (j, 0, c)),
                  pl.BlockSpec((None, kh, tn), lambda i, c: (j, 1, c)),
                  pl.BlockSpec((tm, tn), lambda i, c: (i, c)),
                  mods.spec(k, tn)],
        out_specs=pl.BlockSpec((tm, tn), lambda i, c: (i, c)),
        out_shape=jax.ShapeDtypeStruct((m, d), F32),
        scratch_shapes=[pltpu.VMEM((tm, kh), BF16), pltpu.VMEM((tm, kh), BF16)],
        compiler_params=_params(("parallel", "arbitrary")),
        name="proj_out",
    )(mix_a, mix_b, w_all, w_all, x, mods.arr)


def _gates_kernel(p_ref, b_ref, o_ref, *, n_valid):
    s = p_ref.shape[0]
    lane = lax.broadcasted_iota(jnp.int32, (8, LANES), 1)
    carry = jnp.zeros((8, 1), F32)
    for blk in range(s // LANES):
        sl = slice(blk * LANES, (blk + 1) * LANES)
        pre = p_ref[sl, :].T[0:32] + b_ref[...]
        valid = (lane + blk * LANES) < n_valid
        lf_f = jnp.where(valid, _log_sigmoid(pre[0:8]), 0.0)
        ig = jnp.where(valid, pre[8:16], NEG)
        lf_m = jnp.where(valid, _log_sigmoid(pre[16:24]), 0.0)
        c_f = _cumsum_lanes(lf_f) + carry
        carry = c_f[:, LANES - 1:LANES]
        o_ref[0:8, sl] = c_f
        o_ref[8:16, sl] = ig
        o_ref[16:24, sl] = _cumsum_lanes(lf_m)
        o_ref[24:32, sl] = lf_f


def _gates(pre, bias, n_seq, s, n_valid):
    return pl.pallas_call(
        functools.partial(_gates_kernel, n_valid=n_valid),
        grid=(n_seq,),
        in_specs=[pl.BlockSpec((s, LANES), lambda b: (b, 0)),
                  pl.BlockSpec((32, 1), lambda b: (0, 0))],
        out_specs=pl.BlockSpec((None, 32, s), lambda b: (b, 0, 0)),
        out_shape=jax.ShapeDtypeStruct((n_seq, 32, s), F32),
        compiler_params=_params(("parallel",)),
        name="gates",
    )(pre, bias)


def _fox_kernel(q_ref, k_ref, v_ref, c_ref, o_ref, kb_ref, vb_ref, *, tq):
    s = q_ref.shape[0]
    kb_ref[...] = k_ref[...].astype(BF16)
    vb_ref[...] = v_ref[...].astype(BF16)
    row = lax.broadcasted_iota(jnp.int32, (tq, tq), 0)
    col = lax.broadcasted_iota(jnp.int32, (tq, tq), 1)
    for qi in range(s // tq):
        q = (q_ref[qi * tq:(qi + 1) * tq, :] * QK_SCALE).astype(BF16)
        m = l = acc = None
        for kb in range(qi + 1):
            ks = slice(kb * tq, (kb + 1) * tq)
            sc = _dot_nt(q, kb_ref[ks, :]) - c_ref[:, ks]
            if kb == qi:
                sc = jnp.where(col <= row, sc, -jnp.inf)
            bm = jnp.max(sc, axis=-1, keepdims=True)
            if m is None:
                m = bm
                p = jnp.exp(sc - m)
                l = jnp.sum(p, axis=-1, keepdims=True)
                acc = _dot(p.astype(BF16), vb_ref[ks, :])
            else:
                m_new = jnp.maximum(m, bm)
                alpha = jnp.exp(m - m_new)
                p = jnp.exp(sc - m_new)
                l = alpha * l + jnp.sum(p, axis=-1, keepdims=True)
                acc = alpha * acc + _dot(p.astype(BF16), vb_ref[ks, :])
                m = m_new
        o_ref[qi * tq:(qi + 1) * tq, :] = acc / l


def _fox_prompt(proj, gt4, n_seq, s):
    col = lambda base: pl.BlockSpec((s, HEAD_DIM), lambda b, h: (b, base + h))
    return pl.pallas_call(
        functools.partial(_fox_kernel, tq=TQ),
        grid=(n_seq, H_FOX),
        in_specs=[col(0), col(H_FOX), col(2 * H_FOX),
                  pl.BlockSpec((None, None, 1, s), lambda b, h: (b, h, 0, 0))],
        out_specs=col(0),
        out_shape=jax.ShapeDtypeStruct((n_seq * s, H_FOX * HEAD_DIM), F32),
        scratch_shapes=[pltpu.VMEM((s, HEAD_DIM), BF16), pltpu.VMEM((s, HEAD_DIM), BF16)],
        compiler_params=_params(("parallel", "parallel")),
        name="fox_prompt",
    )(proj, proj, proj, gt4)


def _page_cumsum_kernel(x_ref, o_ref):
    o_ref[...] = _cumsum_lanes(x_ref[...])


def _page_cumsum(lf_t):
    rows, page = lf_t.shape
    tr = 1024
    assert rows % tr == 0 and page == LANES
    return pl.pallas_call(
        _page_cumsum_kernel,
        grid=(rows // tr,),
        in_specs=[pl.BlockSpec((tr, page), lambda i: (i, 0))],
        out_specs=pl.BlockSpec((tr, page), lambda i: (i, 0)),
        out_shape=jax.ShapeDtypeStruct((rows, page), F32),
        compiler_params=_params(("parallel",)),
        name="page_cumsum",
    )(lf_t)


def _fox_decode_kernel(pt_ref, q_ref, *refs, g, n_steps):
    del pt_ref
    k_refs, v_refs, cs_refs = refs[:g], refs[g:2 * g], refs[2 * g:3 * g]
    kn_ref, vn_ref, cn_ref, o_ref, m_ref, l_ref, acc_ref, carry_ref = refs[3 * g:]
    p = pl.program_id(1)
    rows = q_ref.shape[0]
    hbits = H_FOX.bit_length() - 1

    @pl.when(p == 0)
    def _():
        m_ref[...] = jnp.full(m_ref.shape, -jnp.inf, F32)
        l_ref[...] = jnp.zeros(l_ref.shape, F32)
        acc_ref[...] = jnp.zeros(acc_ref.shape, F32)
        carry_ref[...] = jnp.zeros(carry_ref.shape, F32)

    def iotas(ncols):
        return (lax.broadcasted_iota(jnp.int32, (rows, ncols), 0),
                lax.broadcasted_iota(jnp.int32, (rows, ncols), 1))

    def attend(blocks, mask):
        q = (q_ref[...] * QK_SCALE).astype(BF16)
        scores = [jnp.where(mask, _dot_nt(q, k.astype(BF16)) - bias, -jnp.inf) for k, _, bias in blocks]
        m_old = m_ref[...]
        m_new = m_old
        for sc in scores:
            m_new = jnp.maximum(m_new, jnp.max(sc, axis=-1, keepdims=True))
        alpha = jnp.exp(m_old - m_new)
        l = alpha * l_ref[...]
        acc = alpha * acc_ref[...]
        for sc, (_, v, _) in zip(scores, blocks):
            pr = jnp.exp(sc - m_new)
            l = l + jnp.sum(pr, axis=-1, keepdims=True)
            acc = acc + _dot(pr.astype(BF16), v.astype(BF16))
        m_ref[...] = m_new
        l_ref[...] = l
        acc_ref[...] = acc

    @pl.when(p < n_steps)
    def _():
        r, c = iotas(k_refs[0].shape[0])
        mask = (r & (H_FOX - 1)) == (c & (H_FOX - 1))
        carry = carry_ref[...]
        blocks = []
        for i in range(g):
            cs = cs_refs[i][...]
            blocks.append((k_refs[i][...], v_refs[i][...], carry + cs[0:1]))
            carry = carry + cs[1:2]
        carry_ref[...] = carry
        attend(blocks, mask)

    @pl.when(p == n_steps)
    def _():
        ncol = kn_ref.shape[0]
        r, c = iotas(ncol)
        same_head = (r & (H_FOX - 1)) == (c & (H_FOX - 1))
        causal = lax.shift_right_logical(c, hbits) <= lax.shift_right_logical(r, hbits)
        bias = carry_ref[:, :ncol] + cn_ref[...]
        attend([(kn_ref[...], vn_ref[...], bias)], jnp.logical_and(same_head, causal))
        o_ref[...] = acc_ref[...] / l_ref[...]


def _fox_decode(page_table, q, k_pool, v_pool, cs_pool, j, k_new, v_new, c_new):
    db, rows, hd = q.shape
    n_pages = page_table.shape[1]
    prow = k_pool.shape[2]
    g = PAGES_PER_STEP
    assert n_pages % g == 0
    n_steps = n_pages // g
    nrow = k_new.shape[1]

    def pool_map(i):
        return lambda b, p, pt: (j, pt[b, jnp.minimum(p * g + i, n_pages - 1)], 0, 0)

    seq_map = lambda b, p, pt: (b, 0, 0)
    in_specs = [pl.BlockSpec((None, rows, hd), seq_map)]
    in_specs += [pl.BlockSpec((None, None, prow, hd), pool_map(i)) for i in range(g)]
    in_specs += [pl.BlockSpec((None, None, prow, hd), pool_map(i)) for i in range(g)]
    in_specs += [pl.BlockSpec((None, None, 2, prow), pool_map(i)) for i in range(g)]
    in_specs += [pl.BlockSpec((None, nrow, hd), seq_map), pl.BlockSpec((None, nrow, hd), seq_map),
                 pl.BlockSpec((None, 1, nrow), seq_map)]
    grid_spec = pltpu.PrefetchScalarGridSpec(
        num_scalar_prefetch=1,
        grid=(db, n_steps + 1),
        in_specs=in_specs,
        out_specs=pl.BlockSpec((None, rows, hd), seq_map),
        scratch_shapes=[pltpu.VMEM((rows, 1), F32), pltpu.VMEM((rows, 1), F32),
                        pltpu.VMEM((rows, hd), F32), pltpu.VMEM((1, prow), F32)])
    return pl.pallas_call(
        functools.partial(_fox_decode_kernel, g=g, n_steps=n_steps),
        grid_spec=grid_spec,
        out_shape=jax.ShapeDtypeStruct((db, rows, hd), F32),
        compiler_params=_params(("parallel", "arbitrary")),
        name="fox_decode",
    )(page_table, q, *([k_pool] * g), *([v_pool] * g), *([cs_pool] * g), k_new, v_new, c_new)


def _mlstm_kernel(q_ref, k_ref, v_ref, og_ref, *refs, nh):
    ig_refs, b_refs = refs[:nh], refs[nh:2 * nh]
    c0_ref, n0_ref, m0_ref, g_ref, hn_ref, c_ref, n_ref, m_ref = refs[2 * nh:]
    s = q_ref.shape[0]
    ln = CHUNK
    row = lax.broadcasted_iota(jnp.int32, (ln, ln), 0)
    col = lax.broadcasted_iota(jnp.int32, (ln, ln), 1)
    causal = col <= row

    for hh in range(nh):
        hs = slice(hh * HEAD_DIM, (hh + 1) * HEAD_DIM)
        c_st, n_st, m_st = c0_ref[hh], n0_ref[hh], m0_ref[hh]
        for ci in range(s // ln):
            ts = slice(ci * ln, (ci + 1) * ln)
            q = q_ref[ts, hs]
            k = k_ref[ts, hs] * QK_SCALE
            v = v_ref[ts, hs]
            b_row = b_refs[hh][:, ts]
            ig_row = ig_refs[hh][:, ts]
            b_mat = jnp.broadcast_to(b_row, (ln, ln))
            b_col = b_mat.T
            dmat = jnp.where(causal, b_col - b_mat + ig_row, NEG)
            inter = b_col[:, 0:1] + m_st
            m_tok = jnp.maximum(inter, jnp.max(dmat, axis=-1, keepdims=True))
            w_inter = jnp.exp(inter - m_tok)
            qb = q.astype(BF16)
            k_t = k.T
            vb = v.astype(BF16)
            qk = _dot(qb, k_t.astype(BF16)) * jnp.exp(dmat - m_tok)
            num = w_inter * _dot(qb, c_st.astype(BF16)) + _dot(qk.astype(BF16), vb)
            qn = _dot(qb, jnp.broadcast_to(n_st, (HEAD_DIM, LANES)).astype(BF16))[:, 0:1]
            den = w_inter * qn + jnp.sum(qk, axis=-1, keepdims=True)
            h = num / jnp.maximum(jnp.abs(den), jnp.exp(-m_tok))
            hn_ref[ts, hs] = _rms(h) * g_ref[hh] * _sigmoid(og_ref[ts, hs])
            b_last = b_row[:, ln - 1:ln]
            g_row = b_last - b_row + ig_row
            m_new = jnp.maximum(b_last + m_st, jnp.max(g_row, axis=-1, keepdims=True))
            a_prev = jnp.exp(b_last + m_st - m_new)
            kw = k_t * jnp.exp(g_row - m_new)
            c_st = a_prev * c_st + _dot(kw.astype(BF16), vb)
            n_st = a_prev * n_st + jnp.sum(kw, axis=-1, keepdims=True)
            m_st = m_new
        c_ref[hh] = c_st
        n_ref[hh] = n_st
        m_ref[hh] = m_st


def _mlstm(proj, gt4, c0, n0, m0, g_all, j, n_seq, s):
    h, nh = H_MLSTM, MLSTM_HEADS
    w = nh * HEAD_DIM
    col = lambda base: pl.BlockSpec((s, w), lambda b, c: (b, base // nh + c))
    row = lambda base, hh: pl.BlockSpec((None, None, 1, s), lambda b, c: (b, base + c * nh + hh, 0, 0))
    st = lambda shape: pl.BlockSpec((None, nh) + shape, lambda b, c: (b, c, 0, 0))
    return pl.pallas_call(
        functools.partial(_mlstm_kernel, nh=nh),
        grid=(n_seq, h // nh),
        in_specs=[col(24), col(32), col(40), col(48)]
        + [row(8, hh) for hh in range(nh)] + [row(16, hh) for hh in range(nh)]
        + [st((HEAD_DIM, HEAD_DIM)), st((HEAD_DIM, 1)), st((1, 1)),
           pl.BlockSpec((None, nh, 1, HEAD_DIM), lambda b, c: (j, c, 0, 0))],
        out_specs=[pl.BlockSpec((s, w), lambda b, c: (b, c)),
                   st((HEAD_DIM, HEAD_DIM)), st((HEAD_DIM, 1)), st((1, 1))],
        out_shape=[jax.ShapeDtypeStruct((n_seq * s, h * HEAD_DIM), F32),
                   jax.ShapeDtypeStruct((n_seq, h, HEAD_DIM, HEAD_DIM), F32),
                   jax.ShapeDtypeStruct((n_seq, h, HEAD_DIM, 1), F32),
                   jax.ShapeDtypeStruct((n_seq, h, 1, 1), F32)],
        compiler_params=_params(("parallel", "parallel")),
        name="mlstm",
    )(proj, proj, proj, proj, *([gt4] * (2 * nh)), c0, n0, m0, g_all)


def _dil_kernel(q_ref, k_ref, v_ref, o_ref, m_ref, l_ref, acc_ref):
    s = q_ref.shape[0]
    blk = CHUNK
    row = lax.broadcasted_iota(jnp.int32, (blk, blk), 0)
    col = lax.broadcasted_iota(jnp.int32, (blk, blk), 1)
    cur_mask = col <= row
    dist = blk + lax.broadcasted_iota(jnp.int32, (blk, 2 * blk), 0) - lax.broadcasted_iota(jnp.int32, (blk, 2 * blk), 1)
    band_mask = jnp.logical_and(dist >= 0, dist <= blk)

    def rows(start, size, d):
        return pl.ds(start, size) if d == 1 else pl.ds(start, size, stride=d)

    for pi, (window, d) in enumerate(DIL_PATTERNS):
        assert window // d == blk and s % (d * blk) == 0
        for r in range(d):
            for n in range(s // (d * blk)):
                start = r + d * blk * n
                q = (q_ref[rows(start, blk, d), :] * QK_SCALE).astype(BF16)
                if n == 0:
                    kv_rows, mask = rows(start, blk, d), cur_mask
                else:
                    kv_rows, mask = rows(start - d * blk, 2 * blk, d), band_mask
                sc = _dot_nt(q, k_ref[kv_rows, :].astype(BF16))
                sc = jnp.where(mask, sc, -jnp.inf)
                m = jnp.max(sc, axis=-1, keepdims=True)
                p = jnp.exp(sc - m)
                out_rows = rows(pi * s + start, blk, d)
                m_ref[out_rows, :] = jnp.broadcast_to(m, (blk, LANES))
                l_ref[out_rows, :] = jnp.broadcast_to(jnp.sum(p, axis=-1, keepdims=True), (blk, LANES))
                acc_ref[out_rows, :] = _dot(p.astype(BF16), v_ref[kv_rows, :].astype(BF16))

    n_pat = len(DIL_PATTERNS)
    ms = [m_ref[pi * s:(pi + 1) * s, :] for pi in range(n_pat)]
    m_all = functools.reduce(jnp.maximum, ms)
    den = jnp.zeros((s, LANES), F32)
    num = jnp.zeros((s, HEAD_DIM), F32)
    for pi in range(n_pat):
        w = jnp.exp(ms[pi] - m_all)
        den = den + w * l_ref[pi * s:(pi + 1) * s, :]
        num = num + w * acc_ref[pi * s:(pi + 1) * s, :]
    o_ref[...] = num / den


def _dil_prompt(proj, n_seq, s):
    h = H_DIL
    n_pat = len(DIL_PATTERNS)
    col = lambda base: pl.BlockSpec((s, HEAD_DIM), lambda b, c: (b, base + c))
    return pl.pallas_call(
        _dil_kernel,
        grid=(n_seq, h),
        in_specs=[col(0), col(h), col(2 * h)],
        out_specs=col(0),
        out_shape=jax.ShapeDtypeStruct((n_seq * s, h * HEAD_DIM), F32),
        scratch_shapes=[pltpu.VMEM((n_pat * s, LANES), F32), pltpu.VMEM((n_pat * s, LANES), F32),
                        pltpu.VMEM((n_pat * s, HEAD_DIM), F32)],
        compiler_params=_params(("parallel", "parallel")),
        name="dil_prompt",
    )(proj, proj, proj)


def _dil_decode_kernel(q_ref, k_ref, v_ref, cnt_ref, kn_ref, vn_ref, cntn_ref, o_ref, m_ref, l_ref, acc_ref,
                       *, n_tiles):
    p = pl.program_id(1)

    @pl.when(p == 0)
    def _():
        m_ref[...] = jnp.full(m_ref.shape, -jnp.inf, F32)
        l_ref[...] = jnp.zeros(l_ref.shape, F32)
        acc_ref[...] = jnp.zeros(acc_ref.shape, F32)

    def process(k, v, cnt):
        q = (q_ref[...] * QK_SCALE).astype(BF16)
        s = jnp.where(cnt > 0.0, _dot_nt(q, k.astype(BF16)), -jnp.inf)
        m_old = m_ref[...]
        m_new = jnp.maximum(m_old, jnp.max(s, axis=-1, keepdims=True))
        m_safe = jnp.where(m_new == -jnp.inf, 0.0, m_new)
        alpha = jnp.exp(m_old - m_safe)
        pr = cnt * jnp.exp(s - m_safe)
        l_ref[...] = alpha * l_ref[...] + jnp.sum(pr, axis=-1, keepdims=True)
        acc_ref[...] = alpha * acc_ref[...] + _dot(pr.astype(BF16), v.astype(BF16))
        m_ref[...] = m_new

    @pl.when(p < n_tiles)
    def _():
        process(k_ref[...], v_ref[...], cnt_ref[...])

    @pl.when(p == n_tiles)
    def _():
        process(kn_ref[...], vn_ref[...], cntn_ref[...])
        o_ref[...] = acc_ref[...] / l_ref[...]


def _dil_counts(t, cb, n_keys, key0):
    r = jnp.arange(t * H_DIL, dtype=jnp.int32)[:, None]
    c = jnp.arange(n_keys * H_DIL, dtype=jnp.int32)[None, :]
    delta = cb + r // H_DIL - (key0 + c // H_DIL)
    cnt = jnp.zeros(delta.shape, F32)
    for window, d in DIL_PATTERNS:
        cnt = cnt + ((delta >= 0) & (delta % d == 0) & (delta <= window)).astype(F32)
    return jnp.where(r % H_DIL == c % H_DIL, cnt, 0.0)


def _dil_decode(q, k_cache, v_cache, j, k_new, v_new, t):
    db, rows, hd = q.shape
    cb = k_cache.shape[2] // H_DIL
    tk = DIL_TK
    n_tiles = cb // tk
    nrow = k_new.shape[1]
    cnt = _dil_counts(t, cb, cb, 0)
    cnt_new = _dil_counts(t, cb, nrow // H_DIL, cb)
    seq_map = lambda b, p: (b, 0, 0)
    tile_map = lambda b, p: (j, b, jnp.minimum(p, n_tiles - 1), 0)
    return pl.pallas_call(
        functools.partial(_dil_decode_kernel, n_tiles=n_tiles),
        grid=(db, n_tiles + 1),
        in_specs=[pl.BlockSpec((None, rows, hd), seq_map),
                  pl.BlockSpec((None, None, tk * H_DIL, hd), tile_map),
                  pl.BlockSpec((None, None, tk * H_DIL, hd), tile_map),
                  pl.BlockSpec((rows, tk * H_DIL), lambda b, p: (0, jnp.minimum(p, n_tiles - 1))),
                  pl.BlockSpec((None, nrow, hd), seq_map),
                  pl.BlockSpec((None, nrow, hd), seq_map),
                  pl.BlockSpec((rows, nrow), lambda b, p: (0, 0))],
        out_specs=pl.BlockSpec((None, rows, hd), seq_map),
        out_shape=jax.ShapeDtypeStruct((db, rows, hd), F32),
        scratch_shapes=[pltpu.VMEM((rows, 1), F32), pltpu.VMEM((rows, 1), F32),
                        pltpu.VMEM((rows, hd), F32)],
        compiler_params=_params(("parallel", "arbitrary")),
        name="dil_decode",
    )(q, k_cache, v_cache, cnt, k_new, v_new, cnt_new)


def _pad_rows(a, n_seq, rows, pad):
    w = a.shape[-1]
    return jnp.pad(a.reshape(n_seq, rows, w), ((0, 0), (0, pad - rows), (0, 0)))


def _head_rows(a, n_seq, t, n_heads):
    return a.reshape(n_seq, t * n_heads, HEAD_DIM)


def kernel(x_prompt, x_sample, c_prompt, c_sample, page_table, cache_fox_k, cache_fox_v, cache_fox_logf, state_mlstm_c, state_mlstm_n, state_mlstm_m, cache_win_k, cache_win_v, w_ada, b_ada, norm_pre, norm_mix, norm_post, norm_final, ffn_pre_w13, ffn_pre_w2, ffn_post_w13, ffn_post_w2, ab_w_in, ab_w_out, fox_fgate_b, mlstm_igate_b, mlstm_fgate_b, mlstm_norm_g, c_w_in, c_w_out):
    nb, s, d = x_prompt.shape
    db, t, _ = x_sample.shape
    depth = w_ada.shape[0]
    n_ab, n_c = ab_w_in.shape[0], c_w_in.shape[0]
    ms = db * t
    fw = H_FOX * HEAD_DIM
    mw = H_MLSTM * HEAD_DIM
    dw = H_DIL * HEAD_DIM
    assert s % TM == 0 and ms % SUBLANES == 0 and t <= NEW_KEYS

    xp = x_prompt.reshape(nb * s, d)
    xs = x_sample.reshape(ms, d)

    pad_rows = -(ms + nb) % SUBLANES
    c_rows = jnp.concatenate([jnp.repeat(c_sample, t, axis=0), c_prompt, jnp.zeros((pad_rows, d), F32)], axis=0)
    mods_all = _adaln_all(c_rows, w_ada, b_ada)
    mods_p = mods_all[:, ms:ms + nb].reshape(depth * nb * N_MOD, 1, d)
    mods_s = mods_all[:, :ms]

    norm_pre3, norm_mix3, norm_post3 = (a.reshape(depth, 1, d) for a in (norm_pre, norm_mix, norm_post))

    o1, o2, o3 = 3 * fw, 3 * fw + H_FOX, 3 * fw + H_FOX + 3 * mw
    o4 = o3 + 2 * H_MLSTM
    ab_main = jnp.concatenate([ab_w_in[:, :, :o1], ab_w_in[:, :, o2:o3], ab_w_in[:, :, o4:]], axis=2)
    ab_gate = jnp.concatenate([ab_w_in[:, :, o1:o2], ab_w_in[:, :, o3:o4],
                               jnp.zeros((n_ab, d, LANES - H_FOX - 2 * H_MLSTM), F32)], axis=2)
    gate_bias = jnp.concatenate([fox_fgate_b, mlstm_igate_b, mlstm_fgate_b, jnp.zeros((n_ab, 8), F32)],
                                axis=1).reshape(n_ab, 32, 1)
    head_gain = mlstm_norm_g.reshape(n_ab, H_MLSTM, 1, HEAD_DIM)

    n_pool, page = cache_fox_k.shape[1], cache_fox_k.shape[2]
    prow = page * H_FOX
    k_pool = cache_fox_k.reshape(n_ab, n_pool, prow, HEAD_DIM)
    v_pool = cache_fox_v.reshape(n_ab, n_pool, prow, HEAD_DIM)
    lf_t = cache_fox_logf.transpose(0, 1, 3, 2).reshape(n_ab * n_pool * H_FOX, page)
    cl_t = _page_cumsum(lf_t).reshape(n_ab, n_pool, H_FOX, page)
    cl = cl_t.transpose(0, 1, 3, 2)
    tot = jnp.broadcast_to(cl[:, :, page - 1:page, :], cl.shape)
    cs_pool = jnp.stack([cl.reshape(n_ab, n_pool, prow), tot.reshape(n_ab, n_pool, prow)], axis=2)

    cb = cache_win_k.shape[2]
    k_win = cache_win_k.reshape(n_c, db, cb * H_DIL, HEAD_DIM)
    v_win = cache_win_v.reshape(n_c, db, cb * H_DIL, HEAD_DIM)

    zeros_c = jnp.zeros((nb, H_MLSTM, HEAD_DIM, HEAD_DIM), F32)
    zeros_n = jnp.zeros((nb, H_MLSTM, HEAD_DIM, 1), F32)
    zeros_m = jnp.zeros((nb, H_MLSTM, 1, 1), F32)

    outs = {name: [] for name in ("fk_p", "fv_p", "fl_p", "fk_s", "fv_s", "fl_s", "mc_p", "mn_p", "mm_p",
                                  "mc_s", "mn_s", "mm_s", "wk_p", "wv_p", "wk_new", "wv_new")}

    for l in range(depth):
        j = l // 2
        last = l == depth - 1
        mp = _Mods(mods_p, l, True, TM, nb, s)
        md = _Mods(mods_s, l, False, ms)

        xp = _ffn(xp, mp, (0, 1, 2), norm_pre3, ffn_pre_w13, ffn_pre_w2, l)
        xs = _ffn(xs, md, (0, 1, 2), norm_pre3, ffn_pre_w13, ffn_pre_w2, l)

        if l % 2 == 0:
            pp, gp = _proj_in(xp, mp, (3, 4), norm_mix3, l, ab_main, ab_gate, j)
            gt = _gates(gp, gate_bias[j], nb, s, s)
            gt4 = gt.reshape(nb, 32, 1, s)
            o_fox = _fox_prompt(pp, gt4, nb, s)
            hn, c_p, n_p, m_p = _mlstm(pp, gt4, zeros_c, zeros_n, zeros_m, head_gain, j, nb, s)
            xp = _proj_out(o_fox, 0, hn, 0, ab_w_out, j, xp, mp, 5)
            outs["fk_p"].append(pp[:, fw:2 * fw].reshape(nb, s, H_FOX, HEAD_DIM))
            outs["fv_p"].append(pp[:, 2 * fw:3 * fw].reshape(nb, s, H_FOX, HEAD_DIM))
            outs["fl_p"].append(gt[:, 24:32, :].transpose(0, 2, 1))
            outs["mc_p"].append(c_p)
            outs["mn_p"].append(n_p.reshape(nb, H_MLSTM, HEAD_DIM))
            outs["mm_p"].append(m_p.reshape(nb, H_MLSTM))

            ps, gs = _proj_in(xs, md, (3, 4), norm_mix3, l, ab_main, ab_gate, j)
            gts = _gates(_pad_rows(gs, db, t, CHUNK).reshape(db * CHUNK, LANES), gate_bias[j], db, CHUNK, t)
            gts4 = gts.reshape(db, 32, 1, CHUNK)
            nk = NEW_KEYS
            c_new = gts[:, 0:8, :nk].transpose(0, 2, 1).reshape(db, 1, nk * H_FOX)
            o_dec = _fox_decode(page_table, _head_rows(ps[:, :fw], db, t, H_FOX), k_pool, v_pool, cs_pool, j,
                                _pad_rows(_head_rows(ps[:, fw:2 * fw], db, t, H_FOX).reshape(db * t * H_FOX, HEAD_DIM),
                                          db, t * H_FOX, nk * H_FOX),
                                _pad_rows(_head_rows(ps[:, 2 * fw:3 * fw], db, t, H_FOX).reshape(db * t * H_FOX, HEAD_DIM),
                                          db, t * H_FOX, nk * H_FOX),
                                c_new)
            o_fox_s = o_dec.reshape(ms, fw)
            ps_pad = _pad_rows(ps, db, t, CHUNK).reshape(db * CHUNK, -1)
            hn_s, c_s, n_s, m_s = _mlstm(ps_pad, gts4, state_mlstm_c[j],
                                         state_mlstm_n[j].reshape(db, H_MLSTM, HEAD_DIM, 1),
                                         state_mlstm_m[j].reshape(db, H_MLSTM, 1, 1),
                                         head_gain, j, db, CHUNK)
            hn_s = hn_s.reshape(db, CHUNK, mw)[:, :t].reshape(ms, mw)
            xs = _proj_out(o_fox_s, 0, hn_s, 0, ab_w_out, j, xs, md, 5)
            outs["fk_s"].append(ps[:, fw:2 * fw].reshape(db, t, H_FOX, HEAD_DIM))
            outs["fv_s"].append(ps[:, 2 * fw:3 * fw].reshape(db, t, H_FOX, HEAD_DIM))
            outs["fl_s"].append(gts[:, 24:32, :t].transpose(0, 2, 1))
            outs["mc_s"].append(c_s)
            outs["mn_s"].append(n_s.reshape(db, H_MLSTM, HEAD_DIM))
            outs["mm_s"].append(m_s.reshape(db, H_MLSTM))
        else:
            pp, _ = _proj_in(xp, mp, (3, 4), norm_mix3, l, c_w_in, None, j)
            o_dil = _dil_prompt(pp, nb, s)
            xp = _proj_out(o_dil, 0, o_dil, 1, c_w_out, j, xp, mp, 5)
            keep = min(DIL_PATTERNS[-1][0], s)
            outs["wk_p"].append(pp[:, dw:2 * dw].reshape(nb, s, H_DIL, HEAD_DIM)[:, s - keep:])
            outs["wv_p"].append(pp[:, 2 * dw:].reshape(nb, s, H_DIL, HEAD_DIM)[:, s - keep:])

            ps, _ = _proj_in(xs, md, (3, 4), norm_mix3, l, c_w_in, None, j)
            k_new = _head_rows(ps[:, dw:2 * dw], db, t, H_DIL)
            v_new = _head_rows(ps[:, 2 * dw:], db, t, H_DIL)
            nrow = NEW_KEYS * H_DIL
            o_dec = _dil_decode(_head_rows(ps[:, :dw], db, t, H_DIL), k_win, v_win, j,
                                jnp.pad(k_new, ((0, 0), (0, nrow - t * H_DIL), (0, 0))),
                                jnp.pad(v_new, ((0, 0), (0, nrow - t * H_DIL), (0, 0))), t)
            o_dil_s = o_dec.reshape(ms, dw)
            xs = _proj_out(o_dil_s, 0, o_dil_s, 1, c_w_out, j, xs, md, 5)
            outs["wk_new"].append(k_new.reshape(db, t, H_DIL, HEAD_DIM))
            outs["wv_new"].append(v_new.reshape(db, t, H_DIL, HEAD_DIM))

        fin = norm_final if last else None
        xp = _ffn(xp, mp, (6, 7, 8), norm_post3, ffn_post_w13, ffn_post_w2, l, fin)
        xs = _ffn(xs, md, (6, 7, 8), norm_post3, ffn_post_w13, ffn_post_w2, l, fin)

    st = lambda name: jnp.stack(outs[name])
    keep = min(DIL_PATTERNS[-1][0], cb + t)
    wk_s = jnp.concatenate([cache_win_k, st("wk_new")], axis=2)[:, :, cb + t - keep:]
    wv_s = jnp.concatenate([cache_win_v, st("wv_new")], axis=2)[:, :, cb + t - keep:]
    return (xp.reshape(nb, s, d), xs.reshape(db, t, d),
            st("fk_p"), st("fv_p"), st("fl_p"), st("fk_s"), st("fv_s"), st("fl_s"),
            st("mc_p"), st("mn_p"), st("mm_p"), st("mc_s"), st("mn_s"), st("mm_s"),
            st("wk_p"), st("wv_p"), wk_s, wv_s)
```

```python
import functools

import jax
import jax.numpy as jnp
from jax import lax
from jax.experimental import pallas as pl
from jax.experimental.pallas import tpu as pltpu

F32 = jnp.float32
BF16 = jnp.bfloat16

HEAD_DIM = 128
H_FOX = 8
H_MLSTM = 8
H_DIL = 16
N_MOD = 9
EPS = 1e-6
FFN_RES = 0.5
QK_SCALE = HEAD_DIM ** -0.5
DIL_PATTERNS = ((128, 1), (512, 4), (2048, 16))
CHUNK = 128
NEG = -1e30

LANES = 128
SUBLANES = 8
VMEM_BIG = 60 * 1024 * 1024
VMEM_MID = 48 * 1024 * 1024

TM = 1024
TF = 256
TN = 512
TN_IN = 1024
TQ = 256
MLSTM_HEADS = 2
PAGES_PER_STEP = 8
DIL_TK = 512
NEW_KEYS = 16


def _params(sem, vmem=VMEM_MID):
    return pltpu.CompilerParams(dimension_semantics=sem, vmem_limit_bytes=vmem)


def _sigmoid(x):
    return 1.0 / (1.0 + jnp.exp(-x))


def _log_sigmoid(x):
    return jnp.minimum(x, 0.0) - jnp.log1p(jnp.exp(-jnp.abs(x)))


def _dot(a, b):
    return jnp.dot(a, b, preferred_element_type=F32)


def _dot_nt(a, b):
    return lax.dot_general(a, b, (((1,), (1,)), ((), ())), preferred_element_type=F32)


def _rms(x):
    return x * lax.rsqrt(jnp.mean(x * x, axis=-1, keepdims=True) + EPS)


def _norm_mod(x, g, shift, scale):
    return _rms(x) * g * (1.0 + scale) + shift


def _cumsum_lanes(x):
    lane = lax.broadcasted_iota(jnp.int32, x.shape, 1)
    sh = 1
    while sh < x.shape[1]:
        x = x + jnp.where(lane >= sh, pltpu.roll(x, sh, 1), 0.0)
        sh *= 2
    return x


class _Mods:
    def __init__(self, arr, layer, prompt, tm, n_seq=None, seq_rows=None):
        self.arr, self.layer, self.prompt, self.tm, self.n_seq, self.seq_rows = arr, layer, prompt, tm, n_seq, seq_rows

    def spec(self, k, tn=None):
        d = self.arr.shape[-1] if self.prompt else self.arr.shape[-1] // N_MOD
        l, tm = self.layer, self.tm
        if self.prompt:
            n_seq, seq_rows = self.n_seq, self.seq_rows
            row = lambda i: (l * n_seq + (i * tm) // seq_rows) * N_MOD + k
            if tn is None:
                return pl.BlockSpec((None, 1, d), lambda i, j: (row(i), 0, 0))
            return pl.BlockSpec((None, 1, tn), lambda i, j: (row(i), 0, j))
        if tn is None:
            return pl.BlockSpec((None, tm, d), lambda i, j: (l, 0, k))
        return pl.BlockSpec((None, tm, tn), lambda i, j: (l, 0, k * (d // tn) + j))


def _adaln_kernel(c_ref, w_ref, b_ref, o_ref):
    c = c_ref[...]
    s = (c * _sigmoid(c)).astype(BF16)
    o_ref[...] = _dot(s, w_ref[...].astype(BF16)) + b_ref[...]


def _adaln_all(c_rows, w_ada, b_ada):
    depth, d, n = w_ada.shape
    rows = c_rows.shape[0]
    tn = 1024
    return pl.pallas_call(
        _adaln_kernel,
        grid=(depth, n // tn),
        in_specs=[pl.BlockSpec((rows, d), lambda l, j: (0, 0)),
                  pl.BlockSpec((None, d, tn), lambda l, j: (l, 0, j)),
                  pl.BlockSpec((None, 1, tn), lambda l, j: (l, 0, j))],
        out_specs=pl.BlockSpec((None, rows, tn), lambda l, j: (l, 0, j)),
        out_shape=jax.ShapeDtypeStruct((depth, rows, n), F32),
        compiler_params=_params(("parallel", "parallel")),
        name="adaln",
    )(c_rows, w_ada, b_ada.reshape(depth, 1, n))


def _ffn_kernel(x_ref, sh_ref, sc_ref, gt_ref, xs_ref, shs_ref, scs_ref, gts_ref, g_ref, w1_ref, w3_ref, w2_ref,
                *rest, nf, tf, rem, final, tm, ni):
    if final:
        gf_ref, o_ref, os_ref, h_ref = rest
    else:
        o_ref, os_ref, h_ref = rest
    i = pl.program_id(0)
    f = pl.program_id(1)

    @pl.when(f == 0)
    def _():
        h_ref[:tm, :] = _norm_mod(x_ref[...], g_ref[...], sh_ref[...], sc_ref[...]).astype(BF16)

    @pl.when(jnp.logical_and(f == 0, i == ni - 1))
    def _():
        h_ref[tm:, :] = _norm_mod(xs_ref[...], g_ref[...], shs_ref[...], scs_ref[...]).astype(BF16)

    def finish(x, gate, acc):
        y = x + (FFN_RES * gate) * acc
        return _rms(y) * gf_ref[...] if final else y

    def steps(rows):
        with_sample = rows > tm

        def part(w1, w3, w2):
            h = h_ref[:rows, :]
            a = _dot(h, w1.astype(BF16))
            g = _dot(h, w3.astype(BF16))
            u = (a * _sigmoid(a) * g).astype(BF16)
            return _dot(u, w2.astype(BF16))

        @pl.when(f == 0)
        def _():
            p = part(w1_ref[...], w3_ref[...], w2_ref[...])
            o_ref[...] = p[:tm]
            if with_sample:
                os_ref[...] = p[tm:]

        @pl.when(jnp.logical_and(f > 0, f < nf - 1))
        def _():
            p = part(w1_ref[...], w3_ref[...], w2_ref[...])
            o_ref[...] += p[:tm]
            if with_sample:
                os_ref[...] += p[tm:]

        @pl.when(f == nf - 1)
        def _():
            p = part(w1_ref[:, :rem], w3_ref[:, tf - rem:], w2_ref[:rem, :])
            o_ref[...] = finish(x_ref[...], gt_ref[...], o_ref[...] + p[:tm])
            if with_sample:
                os_ref[...] = finish(xs_ref[...], gts_ref[...], os_ref[...] + p[tm:])

    @pl.when(i < ni - 1)
    def _():
        steps(tm)

    @pl.when(i == ni - 1)
    def _():
        steps(h_ref.shape[0])


def _ffn(x, xs, mods, mods_s, ks, g_all, w13_all, w2_all, layer, final_g=None):
    m, d = x.shape
    ms = xs.shape[0]
    tm = mods.tm
    d_ff = w2_all.shape[1]
    tf = TF
    nf = pl.cdiv(d_ff, tf)
    rem = d_ff - (nf - 1) * tf
    ni = m // tm
    assert nf >= 3 and m % tm == 0 and d_ff % LANES == 0 and ni >= 2 and mods_s.tm == ms
    final = final_g is not None
    in_specs = [pl.BlockSpec((tm, d), lambda i, f: (i, 0)),
                mods.spec(ks[0]), mods.spec(ks[1]), mods.spec(ks[2]),
                pl.BlockSpec((ms, d), lambda i, f: (0, 0)),
                mods_s.spec(ks[0]), mods_s.spec(ks[1]), mods_s.spec(ks[2]),
                pl.BlockSpec((None, 1, d), lambda i, f: (layer, 0, 0)),
                pl.BlockSpec((None, d, tf), lambda i, f: (layer, 0, f)),
                pl.BlockSpec((None, pl.Element(d), pl.Element(tf)),
                             lambda i, f: (layer, 0, LANES * jnp.minimum(d_ff // LANES + f * (tf // LANES),
                                                                        (2 * d_ff - tf) // LANES))),
                pl.BlockSpec((None, tf, d), lambda i, f: (layer, f, 0))]
    args = [x, mods.arr, mods.arr, mods.arr, xs, mods_s.arr, mods_s.arr, mods_s.arr, g_all, w13_all, w13_all, w2_all]
    if final:
        in_specs.append(pl.BlockSpec((1, d), lambda i, f: (0, 0)))
        args.append(final_g.reshape(1, d))
    return pl.pallas_call(
        functools.partial(_ffn_kernel, nf=nf, tf=tf, rem=rem, final=final, tm=tm, ni=ni),
        grid=(ni, nf),
        in_specs=in_specs,
        out_specs=[pl.BlockSpec((tm, d), lambda i, f: (i, 0)), pl.BlockSpec((ms, d), lambda i, f: (0, 0))],
        out_shape=[jax.ShapeDtypeStruct((m, d), F32), jax.ShapeDtypeStruct((ms, d), F32)],
        scratch_shapes=[pltpu.VMEM((tm + ms, d), BF16)],
        compiler_params=_params(("arbitrary", "arbitrary"), VMEM_BIG),
        name="ffn",
    )(*args)


def _proj_in_kernel(x_ref, sh_ref, sc_ref, g_ref, w_ref, *rest, gates, aliased):
    rest = list(rest)
    wg_ref = rest.pop(0) if gates else None
    if aliased:
        rest.pop(0)
    if gates:
        o_ref, og_ref, h_ref = rest
    else:
        o_ref, h_ref = rest
    n = pl.program_id(1)

    @pl.when(n == 0)
    def _():
        h = _norm_mod(x_ref[...], g_ref[...], sh_ref[...], sc_ref[...]).astype(BF16)
        h_ref[...] = h
        if gates:
            og_ref[...] = _dot(h, wg_ref[...].astype(BF16))

    o_ref[...] = _dot(h_ref[...], w_ref[...].astype(BF16))


def _proj_in(x, mods, ks, g_all, layer, w_all, wg_all, j, slot=0, n_slots=1, stacked=None):
    m, d = x.shape
    tm = mods.tm
    n = w_all.shape[2]
    tn = TN_IN
    assert n % tn == 0 and m % tm == 0 and (stacked is None) == (slot == 0)
    gates = wg_all is not None
    aliased = stacked is not None
    in_specs = [pl.BlockSpec((tm, d), lambda i, c: (i, 0)), mods.spec(ks[0]), mods.spec(ks[1]),
                pl.BlockSpec((None, 1, d), lambda i, c: (layer, 0, 0)),
                pl.BlockSpec((None, d, tn), lambda i, c: (j, 0, c))]
    out_specs = [pl.BlockSpec((None, tm, tn), lambda i, c: (slot, i, c))]
    out_shape = [jax.ShapeDtypeStruct((n_slots, m, n), F32)]
    args = [x, mods.arr, mods.arr, g_all, w_all]
    if gates:
        in_specs.append(pl.BlockSpec((None, d, LANES), lambda i, c: (j, 0, 0)))
        out_specs.append(pl.BlockSpec((tm, LANES), lambda i, c: (i, 0)))
        out_shape.append(jax.ShapeDtypeStruct((m, LANES), F32))
        args.append(wg_all)
    if aliased:
        in_specs.append(pl.BlockSpec(memory_space=pl.ANY))
        args.append(stacked)
    res = pl.pallas_call(
        functools.partial(_proj_in_kernel, gates=gates, aliased=aliased),
        grid=(m // tm, n // tn),
        in_specs=in_specs, out_specs=out_specs, out_shape=out_shape,
        input_output_aliases={len(args) - 1: 0} if aliased else {},
        scratch_shapes=[pltpu.VMEM((tm, d), BF16)],
        compiler_params=_params(("parallel", "arbitrary"), VMEM_BIG),
        name="proj_in",
    )(*args)
    return (res[0], res[1]) if gates else (res[0], None)


def _proj_out_kernel(a_ref, b_ref, wa_ref, wb_ref, x_ref, gt_ref, o_ref, ab_ref, bb_ref):
    n = pl.program_id(1)

    @pl.when(n == 0)
    def _():
        ab_ref[...] = a_ref[...].astype(BF16)
        bb_ref[...] = b_ref[...].astype(BF16)

    y = _dot(ab_ref[...], wa_ref[...].astype(BF16)) + _dot(bb_ref[...], wb_ref[...].astype(BF16))
    o_ref[...] = x_ref[...] + gt_ref[...] * y


def _proj_out(mix_a, a_blk, mix_b, b_blk, w_all, j, x, mods, k):
    m, d = x.shape
    tm = mods.tm
    kh = w_all.shape[1] // 2
    tn = TN
    return pl.pallas_call(
        _proj_out_kernel,
        grid=(m // tm, d // tn),
        in_specs=[pl.BlockSpec((tm, kh), lambda i, c: (i, a_blk)),
                  pl.BlockSpec((tm, kh), lambda i, c: (i, b_blk)),
                  pl.BlockSpec((None, kh, tn), lambda i, c: (j, 0, c)),
                  pl.BlockSpec((None, kh, tn), lambda i, c: (j, 1, c)),
                  pl.BlockSpec((tm, tn), lambda i, c: (i, c)),
                  mods.spec(k, tn)],
        out_specs=pl.BlockSpec((tm, tn), lambda i, c: (i, c)),
        out_shape=jax.ShapeDtypeStruct((m, d), F32),
        scratch_shapes=[pltpu.VMEM((tm, kh), BF16), pltpu.VMEM((tm, kh), BF16)],
        compiler_params=_params(("parallel", "arbitrary")),
        name="proj_out",
    )(mix_a, mix_b, w_all, w_all, x, mods.arr)


def _gates_kernel(p_ref, b_ref, o_ref, *, n_valid):
    s = p_ref.shape[0]
    lane = lax.broadcasted_iota(jnp.int32, (8, LANES), 1)
    carry = jnp.zeros((8, 1), F32)
    for blk in range(s // LANES):
        sl = slice(blk * LANES, (blk + 1) * LANES)
        pre = p_ref[sl, :].T[0:32] + b_ref[...]
        valid = (lane + blk * LANES) < n_valid
        lf_f = jnp.where(valid, _log_sigmoid(pre[0:8]), 0.0)
        ig = jnp.where(valid, pre[8:16], NEG)
        lf_m = jnp.where(valid, _log_sigmoid(pre[16:24]), 0.0)
        c_f = _cumsum_lanes(lf_f) + carry
        carry = c_f[:, LANES - 1:LANES]
        o_ref[0:8, sl] = c_f
        o_ref[8:16, sl] = ig
        o_ref[16:24, sl] = _cumsum_lanes(lf_m)
        o_ref[24:32, sl] = lf_f


def _gates(pre, bias, n_seq, s, n_valid):
    return pl.pallas_call(
        functools.partial(_gates_kernel, n_valid=n_valid),
        grid=(n_seq,),
        in_specs=[pl.BlockSpec((s, LANES), lambda b: (b, 0)),
                  pl.BlockSpec((32, 1), lambda b: (0, 0))],
        out_specs=pl.BlockSpec((None, 32, s), lambda b: (b, 0, 0)),
        out_shape=jax.ShapeDtypeStruct((n_seq, 32, s), F32),
        compiler_params=_params(("parallel",)),
        name="gates",
    )(pre, bias)


def _fox_kernel(q_ref, k_ref, v_ref, c_ref, o_ref, kb_ref, vb_ref, *, tq):
    s = q_ref.shape[0]
    kb_ref[...] = k_ref[...].astype(BF16)
    vb_ref[...] = v_ref[...].astype(BF16)
    row = lax.broadcasted_iota(jnp.int32, (tq, tq), 0)
    col = lax.broadcasted_iota(jnp.int32, (tq, tq), 1)
    for qi in range(s // tq):
        q = (q_ref[qi * tq:(qi + 1) * tq, :] * QK_SCALE).astype(BF16)
        m = l = acc = None
        for kb in range(qi + 1):
            ks = slice(kb * tq, (kb + 1) * tq)
            sc = _dot_nt(q, kb_ref[ks, :]) - c_ref[:, ks]
            if kb == qi:
                sc = jnp.where(col <= row, sc, -jnp.inf)
            bm = jnp.max(sc, axis=-1, keepdims=True)
            if m is None:
                m = bm
                p = jnp.exp(sc - m)
                l = jnp.sum(p, axis=-1, keepdims=True)
                acc = _dot(p.astype(BF16), vb_ref[ks, :])
            else:
                m_new = jnp.maximum(m, bm)
                alpha = jnp.exp(m - m_new)
                p = jnp.exp(sc - m_new)
                l = alpha * l + jnp.sum(p, axis=-1, keepdims=True)
                acc = alpha * acc + _dot(p.astype(BF16), vb_ref[ks, :])
                m = m_new
        o_ref[qi * tq:(qi + 1) * tq, :] = acc / l


def _fox_prompt(proj, slot, gt4, n_seq, s):
    col = lambda base: pl.BlockSpec((None, s, HEAD_DIM), lambda b, h: (slot, b, base + h))
    return pl.pallas_call(
        functools.partial(_fox_kernel, tq=TQ),
        grid=(n_seq, H_FOX),
        in_specs=[col(0), col(H_FOX), col(2 * H_FOX),
                  pl.BlockSpec((None, None, 1, s), lambda b, h: (b, h, 0, 0))],
        out_specs=pl.BlockSpec((s, HEAD_DIM), lambda b, h: (b, h)),
        out_shape=jax.ShapeDtypeStruct((n_seq * s, H_FOX * HEAD_DIM), F32),
        scratch_shapes=[pltpu.VMEM((s, HEAD_DIM), BF16), pltpu.VMEM((s, HEAD_DIM), BF16)],
        compiler_params=_params(("parallel", "parallel")),
        name="fox_prompt",
    )(proj, proj, proj, gt4)


def _page_cumsum_kernel(x_ref, o_ref):
    o_ref[...] = _cumsum_lanes(x_ref[...])


def _page_cumsum(lf_t):
    rows, page = lf_t.shape
    tr = 1024
    assert rows % tr == 0 and page == LANES
    return pl.pallas_call(
        _page_cumsum_kernel,
        grid=(rows // tr,),
        in_specs=[pl.BlockSpec((tr, page), lambda i: (i, 0))],
        out_specs=pl.BlockSpec((tr, page), lambda i: (i, 0)),
        out_shape=jax.ShapeDtypeStruct((rows, page), F32),
        compiler_params=_params(("parallel",)),
        name="page_cumsum",
    )(lf_t)


def _fox_decode_kernel(pt_ref, q_ref, *refs, g, n_steps):
    del pt_ref
    k_refs, v_refs, cs_refs = refs[:g], refs[g:2 * g], refs[2 * g:3 * g]
    kn_ref, vn_ref, cn_ref, o_ref, m_ref, l_ref, acc_ref, carry_ref = refs[3 * g:]
    p = pl.program_id(1)
    rows = q_ref.shape[0]
    hbits = H_FOX.bit_length() - 1

    @pl.when(p == 0)
    def _():
        m_ref[...] = jnp.full(m_ref.shape, -jnp.inf, F32)
        l_ref[...] = jnp.zeros(l_ref.shape, F32)
        acc_ref[...] = jnp.zeros(acc_ref.shape, F32)
        carry_ref[...] = jnp.zeros(carry_ref.shape, F32)

    def iotas(ncols):
        return (lax.broadcasted_iota(jnp.int32, (rows, ncols), 0),
                lax.broadcasted_iota(jnp.int32, (rows, ncols), 1))

    def attend(blocks, mask):
        q = (q_ref[...] * QK_SCALE).astype(BF16)
        scores = [jnp.where(mask, _dot_nt(q, k.astype(BF16)) - bias, -jnp.inf) for k, _, bias in blocks]
        m_old = m_ref[...]
        m_new = m_old
        for sc in scores:
            m_new = jnp.maximum(m_new, jnp.max(sc, axis=-1, keepdims=True))
        alpha = jnp.exp(m_old - m_new)
        l = alpha * l_ref[...]
        acc = alpha * acc_ref[...]
        for sc, (_, v, _) in zip(scores, blocks):
            pr = jnp.exp(sc - m_new)
            l = l + jnp.sum(pr, axis=-1, keepdims=True)
            acc = acc + _dot(pr.astype(BF16), v.astype(BF16))
        m_ref[...] = m_new
        l_ref[...] = l
        acc_ref[...] = acc

    @pl.when(p < n_steps)
    def _():
        r, c = iotas(k_refs[0].shape[0])
        mask = (r & (H_FOX - 1)) == (c & (H_FOX - 1))
        carry = carry_ref[...]
        blocks = []
        for i in range(g):
            cs = cs_refs[i][...]
            blocks.append((k_refs[i][...], v_refs[i][...], carry + cs[0:1]))
            carry = carry + cs[1:2]
        carry_ref[...] = carry
        attend(blocks, mask)

    @pl.when(p == n_steps)
    def _():
        ncol = kn_ref.shape[0]
        r, c = iotas(ncol)
        same_head = (r & (H_FOX - 1)) == (c & (H_FOX - 1))
        causal = lax.shift_right_logical(c, hbits) <= lax.shift_right_logical(r, hbits)
        bias = carry_ref[:, :ncol] + cn_ref[...]
        attend([(kn_ref[...], vn_ref[...], bias)], jnp.logical_and(same_head, causal))
        o_ref[...] = acc_ref[...] / l_ref[...]


def _fox_decode(page_table, q, k_pool, v_pool, cs_pool, j, k_new, v_new, c_new):
    db, rows, hd = q.shape
    n_pages = page_table.shape[1]
    prow = k_pool.shape[2]
    g = PAGES_PER_STEP
    assert n_pages % g == 0
    n_steps = n_pages // g
    nrow = k_new.shape[1]

    def pool_map(i):
        return lambda b, p, pt: (j, pt[b, jnp.minimum(p * g + i, n_pages - 1)], 0, 0)

    seq_map = lambda b, p, pt: (b, 0, 0)
    in_specs = [pl.BlockSpec((None, rows, hd), seq_map)]
    in_specs += [pl.BlockSpec((None, None, prow, hd), pool_map(i)) for i in range(g)]
    in_specs += [pl.BlockSpec((None, None, prow, hd), pool_map(i)) for i in range(g)]
    in_specs += [pl.BlockSpec((None, None, 2, prow), pool_map(i)) for i in range(g)]
    in_specs += [pl.BlockSpec((None, nrow, hd), seq_map), pl.BlockSpec((None, nrow, hd), seq_map),
                 pl.BlockSpec((None, 1, nrow), seq_map)]
    grid_spec = pltpu.PrefetchScalarGridSpec(
        num_scalar_prefetch=1,
        grid=(db, n_steps + 1),
        in_specs=in_specs,
        out_specs=pl.BlockSpec((None, rows, hd), seq_map),
        scratch_shapes=[pltpu.VMEM((rows, 1), F32), pltpu.VMEM((rows, 1), F32),
                        pltpu.VMEM((rows, hd), F32), pltpu.VMEM((1, prow), F32)])
    return pl.pallas_call(
        functools.partial(_fox_decode_kernel, g=g, n_steps=n_steps),
        grid_spec=grid_spec,
        out_shape=jax.ShapeDtypeStruct((db, rows, hd), F32),
        compiler_params=_params(("parallel", "arbitrary")),
        name="fox_decode",
    )(page_table, q, *([k_pool] * g), *([v_pool] * g), *([cs_pool] * g), k_new, v_new, c_new)


def _mlstm_kernel(q_ref, k_ref, v_ref, og_ref, *refs, nh):
    ig_refs, b_refs = refs[:nh], refs[nh:2 * nh]
    c0_ref, n0_ref, m0_ref, g_ref, hn_ref, c_ref, n_ref, m_ref = refs[2 * nh:]
    s = q_ref.shape[0]
    ln = CHUNK
    row = lax.broadcasted_iota(jnp.int32, (ln, ln), 0)
    col = lax.broadcasted_iota(jnp.int32, (ln, ln), 1)
    causal = col <= row

    last_lane = lax.broadcasted_iota(jnp.int32, (1, ln), 1) == ln - 1
    assert ln == HEAD_DIM == LANES
    for hh in range(nh):
        hs = slice(hh * HEAD_DIM, (hh + 1) * HEAD_DIM)
        c_st = c0_ref[hh]
        n_st = jnp.broadcast_to(n0_ref[hh], (HEAD_DIM, LANES))
        m_st = jnp.broadcast_to(m0_ref[hh], (1, LANES))
        for ci in range(s // ln):
            ts = slice(ci * ln, (ci + 1) * ln)
            qb = q_ref[ts, hs].astype(BF16)
            k_t = (k_ref[ts, hs] * QK_SCALE).T
            vb = v_ref[ts, hs].astype(BF16)
            b_row = b_refs[hh][:, ts]
            ig_row = ig_refs[hh][:, ts]
            b_mat = jnp.broadcast_to(b_row, (ln, ln))
            b_col = b_mat.T
            dmat = jnp.where(causal, b_col - b_mat + ig_row, NEG)
            inter = b_col + m_st
            m_tok = jnp.maximum(inter, jnp.max(dmat, axis=-1, keepdims=True))
            w_inter = jnp.exp(inter - m_tok)
            qk = _dot(qb, k_t.astype(BF16)) * jnp.exp(dmat - m_tok)
            num = w_inter * _dot(qb, c_st.astype(BF16)) + _dot(qk.astype(BF16), vb)
            den = w_inter * _dot(qb, n_st.astype(BF16)) + jnp.sum(qk, axis=-1, keepdims=True)
            h = num / jnp.maximum(jnp.abs(den), jnp.exp(-m_tok))
            hn_ref[ts, hs] = _rms(h) * g_ref[hh] * _sigmoid(og_ref[ts, hs])
            b_last = jnp.sum(jnp.where(last_lane, b_row, 0.0), axis=-1, keepdims=True)
            g_row = b_last - b_row + ig_row
            m_new = jnp.maximum(b_last + m_st, jnp.max(g_row, axis=-1, keepdims=True))
            a_prev = jnp.exp(b_last + m_st - m_new)
            kw = k_t * jnp.exp(g_row - m_new)
            c_st = a_prev * c_st + _dot(kw.astype(BF16), vb)
            n_st = a_prev * n_st + jnp.sum(kw, axis=-1, keepdims=True)
            m_st = m_new
        c_ref[hh] = c_st
        n_ref[hh] = n_st[:, 0:1]
        m_ref[hh] = m_st[:, 0:1]


def _mlstm(proj, slot, gt4, c0, n0, m0, g_all, j, n_seq, s):
    h, nh = H_MLSTM, MLSTM_HEADS
    w = nh * HEAD_DIM
    col = lambda base: pl.BlockSpec((None, s, w), lambda b, c: (slot, b, base // nh + c))
    row = lambda base, hh: pl.BlockSpec((None, None, 1, s), lambda b, c: (b, base + c * nh + hh, 0, 0))
    st = lambda shape: pl.BlockSpec((None, nh) + shape, lambda b, c: (b, c, 0, 0))
    return pl.pallas_call(
        functools.partial(_mlstm_kernel, nh=nh),
        grid=(n_seq, h // nh),
        in_specs=[col(24), col(32), col(40), col(48)]
        + [row(8, hh) for hh in range(nh)] + [row(16, hh) for hh in range(nh)]
        + [st((HEAD_DIM, HEAD_DIM)), st((HEAD_DIM, 1)), st((1, 1)),
           pl.BlockSpec((None, nh, 1, HEAD_DIM), lambda b, c: (j, c, 0, 0))],
        out_specs=[pl.BlockSpec((s, w), lambda b, c: (b, c)),
                   st((HEAD_DIM, HEAD_DIM)), st((HEAD_DIM, 1)), st((1, 1))],
        out_shape=[jax.ShapeDtypeStruct((n_seq * s, h * HEAD_DIM), F32),
                   jax.ShapeDtypeStruct((n_seq, h, HEAD_DIM, HEAD_DIM), F32),
                   jax.ShapeDtypeStruct((n_seq, h, HEAD_DIM, 1), F32),
                   jax.ShapeDtypeStruct((n_seq, h, 1, 1), F32)],
        compiler_params=_params(("parallel", "parallel")),
        name="mlstm",
    )(proj, proj, proj, proj, *([gt4] * (2 * nh)), c0, n0, m0, g_all)


def _dil_kernel(q_ref, k_ref, v_ref, o_ref, m_ref, l_ref, acc_ref):
    s = q_ref.shape[0]
    blk = CHUNK
    row = lax.broadcasted_iota(jnp.int32, (blk, blk), 0)
    col = lax.broadcasted_iota(jnp.int32, (blk, blk), 1)
    cur_mask = col <= row
    dist = blk + lax.broadcasted_iota(jnp.int32, (blk, 2 * blk), 0) - lax.broadcasted_iota(jnp.int32, (blk, 2 * blk), 1)
    band_mask = jnp.logical_and(dist >= 0, dist <= blk)

    def rows(start, size, d):
        return pl.ds(start, size) if d == 1 else pl.ds(start, size, stride=d)

    for pi, (window, d) in enumerate(DIL_PATTERNS):
        assert window // d == blk and s % (d * blk) == 0
        for r in range(d):
            for n in range(s // (d * blk)):
                start = r + d * blk * n
                q = (q_ref[rows(start, blk, d), :] * QK_SCALE).astype(BF16)
                if n == 0:
                    kv_rows, mask = rows(start, blk, d), cur_mask
                else:
                    kv_rows, mask = rows(start - d * blk, 2 * blk, d), band_mask
                sc = _dot_nt(q, k_ref[kv_rows, :].astype(BF16))
                sc = jnp.where(mask, sc, -jnp.inf)
                m = jnp.max(sc, axis=-1, keepdims=True)
                p = jnp.exp(sc - m)
                out_rows = rows(pi * s + start, blk, d)
                m_ref[out_rows, :] = jnp.broadcast_to(m, (blk, LANES))
                l_ref[out_rows, :] = jnp.broadcast_to(jnp.sum(p, axis=-1, keepdims=True), (blk, LANES))
                acc_ref[out_rows, :] = _dot(p.astype(BF16), v_ref[kv_rows, :].astype(BF16))

    n_pat = len(DIL_PATTERNS)
    ms = [m_ref[pi * s:(pi + 1) * s, :] for pi in range(n_pat)]
    m_all = functools.reduce(jnp.maximum, ms)
    den = jnp.zeros((s, LANES), F32)
    num = jnp.zeros((s, HEAD_DIM), F32)
    for pi in range(n_pat):
        w = jnp.exp(ms[pi] - m_all)
        den = den + w * l_ref[pi * s:(pi + 1) * s, :]
        num = num + w * acc_ref[pi * s:(pi + 1) * s, :]
    o_ref[...] = num / den


def _dil_prompt(proj, slot, n_seq, s):
    h = H_DIL
    n_pat = len(DIL_PATTERNS)
    col = lambda base: pl.BlockSpec((None, s, HEAD_DIM), lambda b, c: (slot, b, base + c))
    return pl.pallas_call(
        _dil_kernel,
        grid=(n_seq, h),
        in_specs=[col(0), col(h), col(2 * h)],
        out_specs=pl.BlockSpec((s, HEAD_DIM), lambda b, c: (b, c)),
        out_shape=jax.ShapeDtypeStruct((n_seq * s, h * HEAD_DIM), F32),
        scratch_shapes=[pltpu.VMEM((n_pat * s, LANES), F32), pltpu.VMEM((n_pat * s, LANES), F32),
                        pltpu.VMEM((n_pat * s, HEAD_DIM), F32)],
        compiler_params=_params(("parallel", "parallel")),
        name="dil_prompt",
    )(proj, proj, proj)


def _dil_decode_kernel(q_ref, k_ref, v_ref, cnt_ref, kn_ref, vn_ref, cntn_ref, o_ref, m_ref, l_ref, acc_ref,
                       *, n_tiles):
    p = pl.program_id(1)

    @pl.when(p == 0)
    def _():
        m_ref[...] = jnp.full(m_ref.shape, -jnp.inf, F32)
        l_ref[...] = jnp.zeros(l_ref.shape, F32)
        acc_ref[...] = jnp.zeros(acc_ref.shape, F32)

    def process(k, v, cnt):
        q = (q_ref[...] * QK_SCALE).astype(BF16)
        s = jnp.where(cnt > 0.0, _dot_nt(q, k.astype(BF16)), -jnp.inf)
        m_old = m_ref[...]
        m_new = jnp.maximum(m_old, jnp.max(s, axis=-1, keepdims=True))
        m_safe = jnp.where(m_new == -jnp.inf, 0.0, m_new)
        alpha = jnp.exp(m_old - m_safe)
        pr = cnt * jnp.exp(s - m_safe)
        l_ref[...] = alpha * l_ref[...] + jnp.sum(pr, axis=-1, keepdims=True)
        acc_ref[...] = alpha * acc_ref[...] + _dot(pr.astype(BF16), v.astype(BF16))
        m_ref[...] = m_new

    @pl.when(p < n_tiles)
    def _():
        process(k_ref[...], v_ref[...], cnt_ref[...])

    @pl.when(p == n_tiles)
    def _():
        process(kn_ref[...], vn_ref[...], cntn_ref[...])
        o_ref[...] = acc_ref[...] / l_ref[...]


def _dil_counts(t, cb, n_keys, key0):
    r = jnp.arange(t * H_DIL, dtype=jnp.int32)[:, None]
    c = jnp.arange(n_keys * H_DIL, dtype=jnp.int32)[None, :]
    delta = cb + r // H_DIL - (key0 + c // H_DIL)
    cnt = jnp.zeros(delta.shape, F32)
    for window, d in DIL_PATTERNS:
        cnt = cnt + ((delta >= 0) & (delta % d == 0) & (delta <= window)).astype(F32)
    return jnp.where(r % H_DIL == c % H_DIL, cnt, 0.0)


def _dil_decode(q, k_cache, v_cache, j, k_new, v_new, t):
    db, rows, hd = q.shape
    cb = k_cache.shape[2] // H_DIL
    tk = DIL_TK
    n_tiles = cb // tk
    nrow = k_new.shape[1]
    cnt = _dil_counts(t, cb, cb, 0)
    cnt_new = _dil_counts(t, cb, nrow // H_DIL, cb)
    seq_map = lambda b, p: (b, 0, 0)
    tile_map = lambda b, p: (j, b, jnp.minimum(p, n_tiles - 1), 0)
    return pl.pallas_call(
        functools.partial(_dil_decode_kernel, n_tiles=n_tiles),
        grid=(db, n_tiles + 1),
        in_specs=[pl.BlockSpec((None, rows, hd), seq_map),
                  pl.BlockSpec((None, None, tk * H_DIL, hd), tile_map),
                  pl.BlockSpec((None, None, tk * H_DIL, hd), tile_map),
                  pl.BlockSpec((rows, tk * H_DIL), lambda b, p: (0, jnp.minimum(p, n_tiles - 1))),
                  pl.BlockSpec((None, nrow, hd), seq_map),
                  pl.BlockSpec((None, nrow, hd), seq_map),
                  pl.BlockSpec((rows, nrow), lambda b, p: (0, 0))],
        out_specs=pl.BlockSpec((None, rows, hd), seq_map),
        out_shape=jax.ShapeDtypeStruct((db, rows, hd), F32),
        scratch_shapes=[pltpu.VMEM((rows, 1), F32), pltpu.VMEM((rows, 1), F32),
                        pltpu.VMEM((rows, hd), F32)],
        compiler_params=_params(("parallel", "arbitrary")),
        name="dil_decode",
    )(q, k_cache, v_cache, cnt, k_new, v_new, cnt_new)


def _pad_rows(a, n_seq, rows, pad):
    w = a.shape[-1]
    return jnp.pad(a.reshape(n_seq, rows, w), ((0, 0), (0, pad - rows), (0, 0)))


def _head_rows(a, n_seq, t, n_heads):
    return a.reshape(n_seq, t * n_heads, HEAD_DIM)


def kernel(x_prompt, x_sample, c_prompt, c_sample, page_table, cache_fox_k, cache_fox_v, cache_fox_logf, state_mlstm_c, state_mlstm_n, state_mlstm_m, cache_win_k, cache_win_v, w_ada, b_ada, norm_pre, norm_mix, norm_post, norm_final, ffn_pre_w13, ffn_pre_w2, ffn_post_w13, ffn_post_w2, ab_w_in, ab_w_out, fox_fgate_b, mlstm_igate_b, mlstm_fgate_b, mlstm_norm_g, c_w_in, c_w_out):
    nb, s, d = x_prompt.shape
    db, t, _ = x_sample.shape
    depth = w_ada.shape[0]
    n_ab, n_c = ab_w_in.shape[0], c_w_in.shape[0]
    ms = db * t
    fw = H_FOX * HEAD_DIM
    mw = H_MLSTM * HEAD_DIM
    dw = H_DIL * HEAD_DIM
    assert s % TM == 0 and ms % SUBLANES == 0 and t <= NEW_KEYS

    xp = x_prompt.reshape(nb * s, d)
    xs = x_sample.reshape(ms, d)

    pad_rows = -(ms + nb) % SUBLANES
    c_rows = jnp.concatenate([jnp.repeat(c_sample, t, axis=0), c_prompt, jnp.zeros((pad_rows, d), F32)], axis=0)
    mods_all = _adaln_all(c_rows, w_ada, b_ada)
    mods_p = mods_all[:, ms:ms + nb].reshape(depth * nb * N_MOD, 1, d)
    mods_s = mods_all[:, :ms]

    norm_pre3, norm_mix3, norm_post3 = (a.reshape(depth, 1, d) for a in (norm_pre, norm_mix, norm_post))

    o1, o2, o3 = 3 * fw, 3 * fw + H_FOX, 3 * fw + H_FOX + 3 * mw
    o4 = o3 + 2 * H_MLSTM
    ab_main = jnp.concatenate([ab_w_in[:, :, :o1], ab_w_in[:, :, o2:o3], ab_w_in[:, :, o4:]], axis=2)
    ab_gate = jnp.concatenate([ab_w_in[:, :, o1:o2], ab_w_in[:, :, o3:o4],
                               jnp.zeros((n_ab, d, LANES - H_FOX - 2 * H_MLSTM), F32)], axis=2)
    gate_bias = jnp.concatenate([fox_fgate_b, mlstm_igate_b, mlstm_fgate_b, jnp.zeros((n_ab, 8), F32)],
                                axis=1).reshape(n_ab, 32, 1)
    head_gain = mlstm_norm_g.reshape(n_ab, H_MLSTM, 1, HEAD_DIM)

    n_pool, page = cache_fox_k.shape[1], cache_fox_k.shape[2]
    prow = page * H_FOX
    k_pool = cache_fox_k.reshape(n_ab, n_pool, prow, HEAD_DIM)
    v_pool = cache_fox_v.reshape(n_ab, n_pool, prow, HEAD_DIM)
    lf_t = cache_fox_logf.transpose(0, 1, 3, 2).reshape(n_ab * n_pool * H_FOX, page)
    cl_t = _page_cumsum(lf_t).reshape(n_ab, n_pool, H_FOX, page)
    cl = cl_t.transpose(0, 1, 3, 2)
    tot = jnp.broadcast_to(cl[:, :, page - 1:page, :], cl.shape)
    cs_pool = jnp.stack([cl.reshape(n_ab, n_pool, prow), tot.reshape(n_ab, n_pool, prow)], axis=2)

    cb = cache_win_k.shape[2]
    k_win = cache_win_k.reshape(n_c, db, cb * H_DIL, HEAD_DIM)
    v_win = cache_win_v.reshape(n_c, db, cb * H_DIL, HEAD_DIM)

    zeros_c = jnp.zeros((nb, H_MLSTM, HEAD_DIM, HEAD_DIM), F32)
    zeros_n = jnp.zeros((nb, H_MLSTM, HEAD_DIM, 1), F32)
    zeros_m = jnp.zeros((nb, H_MLSTM, 1, 1), F32)

    outs = {name: [] for name in ("fl_p", "fk_s", "fv_s", "fl_s", "mc_p", "mn_p", "mm_p",
                                  "mc_s", "mn_s", "mm_s", "wk_new", "wv_new")}
    pp_ab = pp_c = None

    for l in range(depth):
        j = l // 2
        last = l == depth - 1
        mp = _Mods(mods_p, l, True, TM, nb, s)
        md = _Mods(mods_s, l, False, ms)

        xp, xs = _ffn(xp, xs, mp, md, (0, 1, 2), norm_pre3, ffn_pre_w13, ffn_pre_w2, l)

        if l % 2 == 0:
            pp_ab, gp = _proj_in(xp, mp, (3, 4), norm_mix3, l, ab_main, ab_gate, j, j, n_ab, pp_ab)
            gt = _gates(gp, gate_bias[j], nb, s, s)
            gt4 = gt.reshape(nb, 32, 1, s)
            o_fox = _fox_prompt(pp_ab, j, gt4, nb, s)
            hn, c_p, n_p, m_p = _mlstm(pp_ab, j, gt4, zeros_c, zeros_n, zeros_m, head_gain, j, nb, s)
            xp = _proj_out(o_fox, 0, hn, 0, ab_w_out, j, xp, mp, 5)
            outs["fl_p"].append(gt[:, 24:32, :].transpose(0, 2, 1))
            outs["mc_p"].append(c_p)
            outs["mn_p"].append(n_p.reshape(nb, H_MLSTM, HEAD_DIM))
            outs["mm_p"].append(m_p.reshape(nb, H_MLSTM))

            ps, gs = _proj_in(xs, md, (3, 4), norm_mix3, l, ab_main, ab_gate, j)
            ps = ps[0]
            gts = _gates(_pad_rows(gs, db, t, CHUNK).reshape(db * CHUNK, LANES), gate_bias[j], db, CHUNK, t)
            gts4 = gts.reshape(db, 32, 1, CHUNK)
            nk = NEW_KEYS
            c_new = gts[:, 0:8, :nk].transpose(0, 2, 1).reshape(db, 1, nk * H_FOX)
            o_dec = _fox_decode(page_table, _head_rows(ps[:, :fw], db, t, H_FOX), k_pool, v_pool, cs_pool, j,
                                _pad_rows(_head_rows(ps[:, fw:2 * fw], db, t, H_FOX).reshape(db * t * H_FOX, HEAD_DIM),
                                          db, t * H_FOX, nk * H_FOX),
                                _pad_rows(_head_rows(ps[:, 2 * fw:3 * fw], db, t, H_FOX).reshape(db * t * H_FOX, HEAD_DIM),
                                          db, t * H_FOX, nk * H_FOX),
                                c_new)
            o_fox_s = o_dec.reshape(ms, fw)
            ps_pad = _pad_rows(ps, db, t, CHUNK).reshape(1, db * CHUNK, -1)
            hn_s, c_s, n_s, m_s = _mlstm(ps_pad, 0, gts4, state_mlstm_c[j],
                                         state_mlstm_n[j].reshape(db, H_MLSTM, HEAD_DIM, 1),
                                         state_mlstm_m[j].reshape(db, H_MLSTM, 1, 1),
                                         head_gain, j, db, CHUNK)
            hn_s = hn_s.reshape(db, CHUNK, mw)[:, :t].reshape(ms, mw)
            xs = _proj_out(o_fox_s, 0, hn_s, 0, ab_w_out, j, xs, md, 5)
            outs["fk_s"].append(ps[:, fw:2 * fw].reshape(db, t, H_FOX, HEAD_DIM))
            outs["fv_s"].append(ps[:, 2 * fw:3 * fw].reshape(db, t, H_FOX, HEAD_DIM))
            outs["fl_s"].append(gts[:, 24:32, :t].transpose(0, 2, 1))
            outs["mc_s"].append(c_s)
            outs["mn_s"].append(n_s.reshape(db, H_MLSTM, HEAD_DIM))
            outs["mm_s"].append(m_s.reshape(db, H_MLSTM))
        else:
            pp_c, _ = _proj_in(xp, mp, (3, 4), norm_mix3, l, c_w_in, None, j, j, n_c, pp_c)
            o_dil = _dil_prompt(pp_c, j, nb, s)
            xp = _proj_out(o_dil, 0, o_dil, 1, c_w_out, j, xp, mp, 5)

            ps, _ = _proj_in(xs, md, (3, 4), norm_mix3, l, c_w_in, None, j)
            ps = ps[0]
            k_new = _head_rows(ps[:, dw:2 * dw], db, t, H_DIL)
            v_new = _head_rows(ps[:, 2 * dw:], db, t, H_DIL)
            nrow = NEW_KEYS * H_DIL
            o_dec = _dil_decode(_head_rows(ps[:, :dw], db, t, H_DIL), k_win, v_win, j,
                                jnp.pad(k_new, ((0, 0), (0, nrow - t * H_DIL), (0, 0))),
                                jnp.pad(v_new, ((0, 0), (0, nrow - t * H_DIL), (0, 0))), t)
            o_dil_s = o_dec.reshape(ms, dw)
            xs = _proj_out(o_dil_s, 0, o_dil_s, 1, c_w_out, j, xs, md, 5)
            outs["wk_new"].append(k_new.reshape(db, t, H_DIL, HEAD_DIM))
            outs["wv_new"].append(v_new.reshape(db, t, H_DIL, HEAD_DIM))

        fin = norm_final if last else None
        xp, xs = _ffn(xp, xs, mp, md, (6, 7, 8), norm_post3, ffn_post_w13, ffn_post_w2, l, fin)

    st = lambda name: jnp.stack(outs[name])
    keep = min(DIL_PATTERNS[-1][0], cb + t)
    wk_s = jnp.concatenate([cache_win_k, st("wk_new")], axis=2)[:, :, cb + t - keep:]
    wv_s = jnp.concatenate([cache_win_v, st("wv_new")], axis=2)[:, :, cb + t - keep:]
    keep_p = min(DIL_PATTERNS[-1][0], s)
    heads = lambda a, n_heads: a.reshape(a.shape[0], nb, s, n_heads, HEAD_DIM)
    return (xp.reshape(nb, s, d), xs.reshape(db, t, d),
            heads(pp_ab[:, :, fw:2 * fw], H_FOX), heads(pp_ab[:, :, 2 * fw:3 * fw], H_FOX),
            st("fl_p"), st("fk_s"), st("fv_s"), st("fl_s"),
            st("mc_p"), st("mn_p"), st("mm_p"), st("mc_s"), st("mn_s"), st("mm_s"),
            heads(pp_c[:, :, dw:2 * dw], H_DIL)[:, :, s - keep_p:], heads(pp_c[:, :, 2 * dw:], H_DIL)[:, :, s - keep_p:],
            wk_s, wv_s)
```

```python
import functools

import jax
import jax.numpy as jnp
from jax import lax
from jax.experimental import pallas as pl
from jax.experimental.pallas import tpu as pltpu

F32 = jnp.float32
BF16 = jnp.bfloat16

HEAD_DIM = 128
H_FOX = 8
H_MLSTM = 8
H_DIL = 16
N_MOD = 9
EPS = 1e-6
FFN_RES = 0.5
QK_SCALE = HEAD_DIM ** -0.5
DIL_PATTERNS = ((128, 1), (512, 4), (2048, 16))
CHUNK = 128
NEG = -1e30

LANES = 128
SUBLANES = 8
VMEM_BIG = 60 * 1024 * 1024
VMEM_MID = 48 * 1024 * 1024

TM = 1024
TF = 256
TN = 512
TN_IN = 1024
TQ = 256
MLSTM_HEADS = 2
PAGES_PER_STEP = 8
DIL_TK = 512
NEW_KEYS = 16


def _params(sem, vmem=VMEM_MID):
    return pltpu.CompilerParams(dimension_semantics=sem, vmem_limit_bytes=vmem)


def _sigmoid(x):
    return 1.0 / (1.0 + jnp.exp(-x))


def _log_sigmoid(x):
    return jnp.minimum(x, 0.0) - jnp.log1p(jnp.exp(-jnp.abs(x)))


def _dot(a, b):
    return jnp.dot(a, b, preferred_element_type=F32)


def _dot_nt(a, b):
    return lax.dot_general(a, b, (((1,), (1,)), ((), ())), preferred_element_type=F32)


def _rms(x):
    return x * lax.rsqrt(jnp.mean(x * x, axis=-1, keepdims=True) + EPS)


def _norm_mod(x, g, shift, scale):
    return _rms(x) * g * (1.0 + scale) + shift


def _cumsum_lanes(x):
    lane = lax.broadcasted_iota(jnp.int32, x.shape, 1)
    sh = 1
    while sh < x.shape[1]:
        x = x + jnp.where(lane >= sh, pltpu.roll(x, sh, 1), 0.0)
        sh *= 2
    return x


class _Mods:
    def __init__(self, arr, layer, prompt, tm, n_seq=None, seq_rows=None):
        self.arr, self.layer, self.prompt, self.tm, self.n_seq, self.seq_rows = arr, layer, prompt, tm, n_seq, seq_rows

    def spec(self, k, tn=None):
        d = self.arr.shape[-1] if self.prompt else self.arr.shape[-1] // N_MOD
        l, tm = self.layer, self.tm
        if self.prompt:
            n_seq, seq_rows = self.n_seq, self.seq_rows
            row = lambda i: (l * n_seq + (i * tm) // seq_rows) * N_MOD + k
            if tn is None:
                return pl.BlockSpec((None, 1, d), lambda i, *_: (row(i), 0, 0))
            return pl.BlockSpec((None, 1, tn), lambda i, j: (row(i), 0, j))
        if tn is None:
            return pl.BlockSpec((None, tm, d), lambda i, *_: (l, 0, k))
        return pl.BlockSpec((None, tm, tn), lambda i, j: (l, 0, k * (d // tn) + j))


def _adaln_kernel(c_ref, w_ref, b_ref, o_ref):
    c = c_ref[...]
    s = (c * _sigmoid(c)).astype(BF16)
    o_ref[...] = _dot(s, w_ref[...].astype(BF16)) + b_ref[...]


def _adaln_all(c_rows, w_ada, b_ada):
    depth, d, n = w_ada.shape
    rows = c_rows.shape[0]
    tn = 1024
    return pl.pallas_call(
        _adaln_kernel,
        grid=(depth, n // tn),
        in_specs=[pl.BlockSpec((rows, d), lambda l, j: (0, 0)),
                  pl.BlockSpec((None, d, tn), lambda l, j: (l, 0, j)),
                  pl.BlockSpec((None, 1, tn), lambda l, j: (l, 0, j))],
        out_specs=pl.BlockSpec((None, rows, tn), lambda l, j: (l, 0, j)),
        out_shape=jax.ShapeDtypeStruct((depth, rows, n), F32),
        compiler_params=_params(("parallel", "parallel")),
        name="adaln",
    )(c_rows, w_ada, b_ada.reshape(depth, 1, n))


def _ffn_kernel(x_ref, sh_ref, sc_ref, gt_ref, xs_ref, shs_ref, scs_ref, gts_ref, g_ref, w13_hbm, w2_hbm,
                *rest, layer, d_ff, nf, tf, rem, final, tm, ni):
    if final:
        gf_ref, o_ref, os_ref, h_ref, w1_buf, w3_buf, w2_buf, sem = rest
    else:
        o_ref, os_ref, h_ref, w1_buf, w3_buf, w2_buf, sem = rest
    i = pl.program_id(0)

    def tile_copies(f, slot, width):
        c0 = f * tf if isinstance(f, int) else pl.multiple_of(f * tf, tf)
        return (pltpu.make_async_copy(w13_hbm.at[layer, :, pl.ds(c0, width)],
                                      w1_buf.at[slot, :, pl.ds(0, width)], sem.at[slot, 0]),
                pltpu.make_async_copy(w13_hbm.at[layer, :, pl.ds(d_ff + c0, width)],
                                      w3_buf.at[slot, :, pl.ds(0, width)], sem.at[slot, 1]),
                pltpu.make_async_copy(w2_hbm.at[layer, pl.ds(c0, width), :],
                                      w2_buf.at[slot, pl.ds(0, width), :], sem.at[slot, 2]))

    def start(f, slot, width):
        for c in tile_copies(f, slot, width):
            c.start()

    def wait(f, slot, width):
        for c in tile_copies(f, slot, width):
            c.wait()

    def finish(x, gate, acc):
        y = x + (FFN_RES * gate) * acc
        return _rms(y) * gf_ref[...] if final else y

    def run(rows):
        with_sample = rows > tm

        def part(w1, w3, w2):
            h = h_ref[:rows, :]
            a = _dot(h, w1.astype(BF16))
            g = _dot(h, w3.astype(BF16))
            u = (a * _sigmoid(a) * g).astype(BF16)
            return _dot(u, w2.astype(BF16))

        start(0, 0, tf)
        h_ref[:tm, :] = _norm_mod(x_ref[...], g_ref[...], sh_ref[...], sc_ref[...]).astype(BF16)
        if with_sample:
            h_ref[tm:, :] = _norm_mod(xs_ref[...], g_ref[...], shs_ref[...], scs_ref[...]).astype(BF16)

        wait(0, 0, tf)
        start(1, 1, tf)
        p = part(w1_buf[0], w3_buf[0], w2_buf[0])
        o_ref[...] = p[:tm]
        if with_sample:
            os_ref[...] = p[tm:]

        def body(f, carry):
            slot = lax.rem(f, 2)
            wait(f, slot, tf)

            @pl.when(f + 1 < nf - 1)
            def _():
                start(f + 1, 1 - slot, tf)

            @pl.when(f + 1 == nf - 1)
            def _():
                start(nf - 1, 1 - slot, rem)

            p = part(w1_buf[slot], w3_buf[slot], w2_buf[slot])
            o_ref[...] += p[:tm]
            if with_sample:
                os_ref[...] += p[tm:]
            return carry

        lax.fori_loop(1, nf - 1, body, 0)

        slot = (nf - 1) % 2
        wait(nf - 1, slot, rem)
        p = part(w1_buf[slot, :, :rem], w3_buf[slot, :, :rem], w2_buf[slot, :rem, :])
        o_ref[...] = finish(x_ref[...], gt_ref[...], o_ref[...] + p[:tm])
        if with_sample:
            os_ref[...] = finish(xs_ref[...], gts_ref[...], os_ref[...] + p[tm:])

    @pl.when(i < ni - 1)
    def _():
        run(tm)

    @pl.when(i == ni - 1)
    def _():
        run(h_ref.shape[0])


def _ffn(x, xs, mods, mods_s, ks, g_all, w13_all, w2_all, layer, final_g=None):
    m, d = x.shape
    ms = xs.shape[0]
    tm = mods.tm
    d_ff = w2_all.shape[1]
    tf = TF
    nf = pl.cdiv(d_ff, tf)
    rem = d_ff - (nf - 1) * tf
    ni = m // tm
    assert nf >= 3 and m % tm == 0 and d_ff % LANES == 0 and ni >= 2 and mods_s.tm == ms
    final = final_g is not None
    in_specs = [pl.BlockSpec((tm, d), lambda i: (i, 0)),
                mods.spec(ks[0]), mods.spec(ks[1]), mods.spec(ks[2]),
                pl.BlockSpec((ms, d), lambda i: (0, 0)),
                mods_s.spec(ks[0]), mods_s.spec(ks[1]), mods_s.spec(ks[2]),
                pl.BlockSpec((None, 1, d), lambda i: (layer, 0, 0)),
                pl.BlockSpec(memory_space=pl.ANY),
                pl.BlockSpec(memory_space=pl.ANY)]
    args = [x, mods.arr, mods.arr, mods.arr, xs, mods_s.arr, mods_s.arr, mods_s.arr, g_all, w13_all, w2_all]
    if final:
        in_specs.append(pl.BlockSpec((1, d), lambda i: (0, 0)))
        args.append(final_g.reshape(1, d))
    return pl.pallas_call(
        functools.partial(_ffn_kernel, layer=layer, d_ff=d_ff, nf=nf, tf=tf, rem=rem, final=final, tm=tm, ni=ni),
        grid=(ni,),
        in_specs=in_specs,
        out_specs=[pl.BlockSpec((tm, d), lambda i: (i, 0)), pl.BlockSpec((ms, d), lambda i: (0, 0))],
        out_shape=[jax.ShapeDtypeStruct((m, d), F32), jax.ShapeDtypeStruct((ms, d), F32)],
        scratch_shapes=[pltpu.VMEM((tm + ms, d), BF16),
                        pltpu.VMEM((2, d, tf), F32), pltpu.VMEM((2, d, tf), F32), pltpu.VMEM((2, tf, d), F32),
                        pltpu.SemaphoreType.DMA((2, 3))],
        compiler_params=_params(("arbitrary",), VMEM_BIG),
        name="ffn",
    )(*args)


def _proj_in_kernel(x_ref, sh_ref, sc_ref, g_ref, w_ref, *rest, gates, aliased):
    rest = list(rest)
    wg_ref = rest.pop(0) if gates else None
    if aliased:
        rest.pop(0)
    if gates:
        o_ref, og_ref, h_ref = rest
    else:
        o_ref, h_ref = rest
    n = pl.program_id(1)

    @pl.when(n == 0)
    def _():
        h = _norm_mod(x_ref[...], g_ref[...], sh_ref[...], sc_ref[...]).astype(BF16)
        h_ref[...] = h
        if gates:
            og_ref[...] = _dot(h, wg_ref[...].astype(BF16))

    o_ref[...] = _dot(h_ref[...], w_ref[...].astype(BF16))


def _proj_in(x, mods, ks, g_all, layer, w_all, wg_all, j, slot=0, n_slots=1, stacked=None):
    m, d = x.shape
    tm = mods.tm
    n = w_all.shape[2]
    tn = TN_IN
    assert n % tn == 0 and m % tm == 0 and (stacked is None) == (slot == 0)
    gates = wg_all is not None
    aliased = stacked is not None
    in_specs = [pl.BlockSpec((tm, d), lambda i, c: (i, 0)), mods.spec(ks[0]), mods.spec(ks[1]),
                pl.BlockSpec((None, 1, d), lambda i, c: (layer, 0, 0)),
                pl.BlockSpec((None, d, tn), lambda i, c: (j, 0, c))]
    out_specs = [pl.BlockSpec((None, tm, tn), lambda i, c: (slot, i, c))]
    out_shape = [jax.ShapeDtypeStruct((n_slots, m, n), F32)]
    args = [x, mods.arr, mods.arr, g_all, w_all]
    if gates:
        in_specs.append(pl.BlockSpec((None, d, LANES), lambda i, c: (j, 0, 0)))
        out_specs.append(pl.BlockSpec((tm, LANES), lambda i, c: (i, 0)))
        out_shape.append(jax.ShapeDtypeStruct((m, LANES), F32))
        args.append(wg_all)
    if aliased:
        in_specs.append(pl.BlockSpec(memory_space=pl.ANY))
        args.append(stacked)
    res = pl.pallas_call(
        functools.partial(_proj_in_kernel, gates=gates, aliased=aliased),
        grid=(m // tm, n // tn),
        in_specs=in_specs, out_specs=out_specs, out_shape=out_shape,
        input_output_aliases={len(args) - 1: 0} if aliased else {},
        scratch_shapes=[pltpu.VMEM((tm, d), BF16)],
        compiler_params=_params(("parallel", "arbitrary"), VMEM_BIG),
        name="proj_in",
    )(*args)
    return (res[0], res[1]) if gates else (res[0], None)


def _proj_out_kernel(a_ref, b_ref, wa_ref, wb_ref, x_ref, gt_ref, o_ref, ab_ref, bb_ref):
    n = pl.program_id(1)

    @pl.when(n == 0)
    def _():
        ab_ref[...] = a_ref[...].astype(BF16)
        bb_ref[...] = b_ref[...].astype(BF16)

    y = _dot(ab_ref[...], wa_ref[...].astype(BF16)) + _dot(bb_ref[...], wb_ref[...].astype(BF16))
    o_ref[...] = x_ref[...] + gt_ref[...] * y


def _proj_out(mix_a, a_blk, mix_b, b_blk, w_all, j, x, mods, k):
    m, d = x.shape
    tm = mods.tm
    kh = w_all.shape[1] // 2
    tn = TN
    return pl.pallas_call(
        _proj_out_kernel,
        grid=(m // tm, d // tn),
        in_specs=[pl.BlockSpec((tm, kh), lambda i, c: (i, a_blk)),
                  pl.BlockSpec((tm, kh), lambda i, c: (i, b_blk)),
                  pl.BlockSpec((None, kh, tn), lambda i, c: (j, 0, c)),
                  pl.BlockSpec((None, kh, tn), lambda i, c: (j, 1, c)),
                  pl.BlockSpec((tm, tn), lambda i, c: (i, c)),
                  mods.spec(k, tn)],
        out_specs=pl.BlockSpec((tm, tn), lambda i, c: (i, c)),
        out_shape=jax.ShapeDtypeStruct((m, d), F32),
        scratch_shapes=[pltpu.VMEM((tm, kh), BF16), pltpu.VMEM((tm, kh), BF16)],
        compiler_params=_params(("parallel", "arbitrary")),
        name="proj_out",
    )(mix_a, mix_b, w_all, w_all, x, mods.arr)


def _gates_kernel(p_ref, b_ref, o_ref, *, n_valid):
    s = p_ref.shape[0]
    lane = lax.broadcasted_iota(jnp.int32, (8, LANES), 1)
    carry = jnp.zeros((8, 1), F32)
    for blk in range(s // LANES):
        sl = slice(blk * LANES, (blk + 1) * LANES)
        pre = p_ref[sl, :].T[0:32] + b_ref[...]
        valid = (lane + blk * LANES) < n_valid
        lf_f = jnp.where(valid, _log_sigmoid(pre[0:8]), 0.0)
        ig = jnp.where(valid, pre[8:16], NEG)
        lf_m = jnp.where(valid, _log_sigmoid(pre[16:24]), 0.0)
        c_f = _cumsum_lanes(lf_f) + carry
        carry = c_f[:, LANES - 1:LANES]
        o_ref[0:8, sl] = c_f
        o_ref[8:16, sl] = ig
        o_ref[16:24, sl] = _cumsum_lanes(lf_m)
        o_ref[24:32, sl] = lf_f


def _gates(pre, bias, n_seq, s, n_valid):
    return pl.pallas_call(
        functools.partial(_gates_kernel, n_valid=n_valid),
        grid=(n_seq,),
        in_specs=[pl.BlockSpec((s, LANES), lambda b: (b, 0)),
                  pl.BlockSpec((32, 1), lambda b: (0, 0))],
        out_specs=pl.BlockSpec((None, 32, s), lambda b: (b, 0, 0)),
        out_shape=jax.ShapeDtypeStruct((n_seq, 32, s), F32),
        compiler_params=_params(("parallel",)),
        name="gates",
    )(pre, bias)


def _fox_kernel(q_ref, k_ref, v_ref, c_ref, o_ref, kb_ref, vb_ref, *, tq):
    s = q_ref.shape[0]
    kb_ref[...] = k_ref[...].astype(BF16)
    vb_ref[...] = v_ref[...].astype(BF16)
    row = lax.broadcasted_iota(jnp.int32, (tq, tq), 0)
    col = lax.broadcasted_iota(jnp.int32, (tq, tq), 1)
    for qi in range(s // tq):
        q = (q_ref[qi * tq:(qi + 1) * tq, :] * QK_SCALE).astype(BF16)
        m = l = acc = None
        for kb in range(qi + 1):
            ks = slice(kb * tq, (kb + 1) * tq)
            sc = _dot_nt(q, kb_ref[ks, :]) - c_ref[:, ks]
            if kb == qi:
                sc = jnp.where(col <= row, sc, -jnp.inf)
            bm = jnp.max(sc, axis=-1, keepdims=True)
            if m is None:
                m = bm
                p = jnp.exp(sc - m)
                l = jnp.sum(p, axis=-1, keepdims=True)
                acc = _dot(p.astype(BF16), vb_ref[ks, :])
            else:
                m_new = jnp.maximum(m, bm)
                alpha = jnp.exp(m - m_new)
                p = jnp.exp(sc - m_new)
                l = alpha * l + jnp.sum(p, axis=-1, keepdims=True)
                acc = alpha * acc + _dot(p.astype(BF16), vb_ref[ks, :])
                m = m_new
        o_ref[qi * tq:(qi + 1) * tq, :] = acc / l


def _fox_prompt(proj, slot, gt4, n_seq, s):
    col = lambda base: pl.BlockSpec((None, s, HEAD_DIM), lambda b, h: (slot, b, base + h))
    return pl.pallas_call(
        functools.partial(_fox_kernel, tq=TQ),
        grid=(n_seq, H_FOX),
        in_specs=[col(0), col(H_FOX), col(2 * H_FOX),
                  pl.BlockSpec((None, None, 1, s), lambda b, h: (b, h, 0, 0))],
        out_specs=pl.BlockSpec((s, HEAD_DIM), lambda b, h: (b, h)),
        out_shape=jax.ShapeDtypeStruct((n_seq * s, H_FOX * HEAD_DIM), F32),
        scratch_shapes=[pltpu.VMEM((s, HEAD_DIM), BF16), pltpu.VMEM((s, HEAD_DIM), BF16)],
        compiler_params=_params(("parallel", "parallel")),
        name="fox_prompt",
    )(proj, proj, proj, gt4)


def _page_cumsum_kernel(x_ref, o_ref):
    o_ref[...] = _cumsum_lanes(x_ref[...])


def _page_cumsum(lf_t):
    rows, page = lf_t.shape
    tr = 1024
    assert rows % tr == 0 and page == LANES
    return pl.pallas_call(
        _page_cumsum_kernel,
        grid=(rows // tr,),
        in_specs=[pl.BlockSpec((tr, page), lambda i: (i, 0))],
        out_specs=pl.BlockSpec((tr, page), lambda i: (i, 0)),
        out_shape=jax.ShapeDtypeStruct((rows, page), F32),
        compiler_params=_params(("parallel",)),
        name="page_cumsum",
    )(lf_t)


def _fox_decode_kernel(pt_ref, q_ref, *refs, g, n_steps):
    del pt_ref
    k_refs, v_refs, cs_refs = refs[:g], refs[g:2 * g], refs[2 * g:3 * g]
    kn_ref, vn_ref, cn_ref, o_ref, m_ref, l_ref, acc_ref, carry_ref = refs[3 * g:]
    p = pl.program_id(1)
    rows = q_ref.shape[0]
    hbits = H_FOX.bit_length() - 1

    @pl.when(p == 0)
    def _():
        m_ref[...] = jnp.full(m_ref.shape, -jnp.inf, F32)
        l_ref[...] = jnp.zeros(l_ref.shape, F32)
        acc_ref[...] = jnp.zeros(acc_ref.shape, F32)
        carry_ref[...] = jnp.zeros(carry_ref.shape, F32)

    def iotas(ncols):
        return (lax.broadcasted_iota(jnp.int32, (rows, ncols), 0),
                lax.broadcasted_iota(jnp.int32, (rows, ncols), 1))

    def attend(blocks, mask):
        q = (q_ref[...] * QK_SCALE).astype(BF16)
        scores = [jnp.where(mask, _dot_nt(q, k.astype(BF16)) - bias, -jnp.inf) for k, _, bias in blocks]
        m_old = m_ref[...]
        m_new = m_old
        for sc in scores:
            m_new = jnp.maximum(m_new, jnp.max(sc, axis=-1, keepdims=True))
        alpha = jnp.exp(m_old - m_new)
        l = alpha * l_ref[...]
        acc = alpha * acc_ref[...]
        for sc, (_, v, _) in zip(scores, blocks):
            pr = jnp.exp(sc - m_new)
            l = l + jnp.sum(pr, axis=-1, keepdims=True)
            acc = acc + _dot(pr.astype(BF16), v.astype(BF16))
        m_ref[...] = m_new
        l_ref[...] = l
        acc_ref[...] = acc

    @pl.when(p < n_steps)
    def _():
        r, c = iotas(k_refs[0].shape[0])
        mask = (r & (H_FOX - 1)) == (c & (H_FOX - 1))
        carry = carry_ref[...]
        blocks = []
        for i in range(g):
            cs = cs_refs[i][...]
            blocks.append((k_refs[i][...], v_refs[i][...], carry + cs[0:1]))
            carry = carry + cs[1:2]
        carry_ref[...] = carry
        attend(blocks, mask)

    @pl.when(p == n_steps)
    def _():
        ncol = kn_ref.shape[0]
        r, c = iotas(ncol)
        same_head = (r & (H_FOX - 1)) == (c & (H_FOX - 1))
        causal = lax.shift_right_logical(c, hbits) <= lax.shift_right_logical(r, hbits)
        bias = carry_ref[:, :ncol] + cn_ref[...]
        attend([(kn_ref[...], vn_ref[...], bias)], jnp.logical_and(same_head, causal))
        o_ref[...] = acc_ref[...] / l_ref[...]


def _fox_decode(page_table, q, k_pool, v_pool, cs_pool, j, k_new, v_new, c_new):
    db, rows, hd = q.shape
    n_pages = page_table.shape[1]
    prow = k_pool.shape[2]
    g = PAGES_PER_STEP
    assert n_pages % g == 0
    n_steps = n_pages // g
    nrow = k_new.shape[1]

    def pool_map(i):
        return lambda b, p, pt: (j, pt[b, jnp.minimum(p * g + i, n_pages - 1)], 0, 0)

    seq_map = lambda b, p, pt: (b, 0, 0)
    in_specs = [pl.BlockSpec((None, rows, hd), seq_map)]
    in_specs += [pl.BlockSpec((None, None, prow, hd), pool_map(i)) for i in range(g)]
    in_specs += [pl.BlockSpec((None, None, prow, hd), pool_map(i)) for i in range(g)]
    in_specs += [pl.BlockSpec((None, None, 2, prow), pool_map(i)) for i in range(g)]
    in_specs += [pl.BlockSpec((None, nrow, hd), seq_map), pl.BlockSpec((None, nrow, hd), seq_map),
                 pl.BlockSpec((None, 1, nrow), seq_map)]
    grid_spec = pltpu.PrefetchScalarGridSpec(
        num_scalar_prefetch=1,
        grid=(db, n_steps + 1),
        in_specs=in_specs,
        out_specs=pl.BlockSpec((None, rows, hd), seq_map),
        scratch_shapes=[pltpu.VMEM((rows, 1), F32), pltpu.VMEM((rows, 1), F32),
                        pltpu.VMEM((rows, hd), F32), pltpu.VMEM((1, prow), F32)])
    return pl.pallas_call(
        functools.partial(_fox_decode_kernel, g=g, n_steps=n_steps),
        grid_spec=grid_spec,
        out_shape=jax.ShapeDtypeStruct((db, rows, hd), F32),
        compiler_params=_params(("parallel", "arbitrary")),
        name="fox_decode",
    )(page_table, q, *([k_pool] * g), *([v_pool] * g), *([cs_pool] * g), k_new, v_new, c_new)


def _mlstm_kernel(q_ref, k_ref, v_ref, og_ref, *refs, nh):
    ig_refs, b_refs = refs[:nh], refs[nh:2 * nh]
    c0_ref, n0_ref, m0_ref, g_ref, hn_ref, c_ref, n_ref, m_ref = refs[2 * nh:]
    s = q_ref.shape[0]
    ln = CHUNK
    row = lax.broadcasted_iota(jnp.int32, (ln, ln), 0)
    col = lax.broadcasted_iota(jnp.int32, (ln, ln), 1)
    causal = col <= row

    last_lane = lax.broadcasted_iota(jnp.int32, (1, ln), 1) == ln - 1
    assert ln == HEAD_DIM == LANES
    for hh in range(nh):
        hs = slice(hh * HEAD_DIM, (hh + 1) * HEAD_DIM)
        c_st = c0_ref[hh]
        n_st = jnp.broadcast_to(n0_ref[hh], (HEAD_DIM, LANES))
        m_st = jnp.broadcast_to(m0_ref[hh], (1, LANES))
        for ci in range(s // ln):
            ts = slice(ci * ln, (ci + 1) * ln)
            qb = q_ref[ts, hs].astype(BF16)
            k_t = (k_ref[ts, hs] * QK_SCALE).T
            vb = v_ref[ts, hs].astype(BF16)
            b_row = b_refs[hh][:, ts]
            ig_row = ig_refs[hh][:, ts]
            b_mat = jnp.broadcast_to(b_row, (ln, ln))
            b_col = b_mat.T
            dmat = jnp.where(causal, b_col - b_mat + ig_row, NEG)
            inter = b_col + m_st
            m_tok = jnp.maximum(inter, jnp.max(dmat, axis=-1, keepdims=True))
            w_inter = jnp.exp(inter - m_tok)
            qk = _dot(qb, k_t.astype(BF16)) * jnp.exp(dmat - m_tok)
            num = w_inter * _dot(qb, c_st.astype(BF16)) + _dot(qk.astype(BF16), vb)
            den = w_inter * _dot(qb, n_st.astype(BF16)) + jnp.sum(qk, axis=-1, keepdims=True)
            h = num / jnp.maximum(jnp.abs(den), jnp.exp(-m_tok))
            hn_ref[ts, hs] = _rms(h) * g_ref[hh] * _sigmoid(og_ref[ts, hs])
            b_last = jnp.sum(jnp.where(last_lane, b_row, 0.0), axis=-1, keepdims=True)
            g_row = b_last - b_row + ig_row
            m_new = jnp.maximum(b_last + m_st, jnp.max(g_row, axis=-1, keepdims=True))
            a_prev = jnp.exp(b_last + m_st - m_new)
            kw = k_t * jnp.exp(g_row - m_new)
            c_st = a_prev * c_st + _dot(kw.astype(BF16), vb)
            n_st = a_prev * n_st + jnp.sum(kw, axis=-1, keepdims=True)
            m_st = m_new
        c_ref[hh] = c_st
        n_ref[hh] = n_st[:, 0:1]
        m_ref[hh] = m_st[:, 0:1]


def _mlstm(proj, slot, gt4, c0, n0, m0, g_all, j, n_seq, s):
    h, nh = H_MLSTM, MLSTM_HEADS
    w = nh * HEAD_DIM
    col = lambda base: pl.BlockSpec((None, s, w), lambda b, c: (slot, b, base // nh + c))
    row = lambda base, hh: pl.BlockSpec((None, None, 1, s), lambda b, c: (b, base + c * nh + hh, 0, 0))
    st = lambda shape: pl.BlockSpec((None, nh) + shape, lambda b, c: (b, c, 0, 0))
    return pl.pallas_call(
        functools.partial(_mlstm_kernel, nh=nh),
        grid=(n_seq, h // nh),
        in_specs=[col(24), col(32), col(40), col(48)]
        + [row(8, hh) for hh in range(nh)] + [row(16, hh) for hh in range(nh)]
        + [st((HEAD_DIM, HEAD_DIM)), st((HEAD_DIM, 1)), st((1, 1)),
           pl.BlockSpec((None, nh, 1, HEAD_DIM), lambda b, c: (j, c, 0, 0))],
        out_specs=[pl.BlockSpec((s, w), lambda b, c: (b, c)),
                   st((HEAD_DIM, HEAD_DIM)), st((HEAD_DIM, 1)), st((1, 1))],
        out_shape=[jax.ShapeDtypeStruct((n_seq * s, h * HEAD_DIM), F32),
                   jax.ShapeDtypeStruct((n_seq, h, HEAD_DIM, HEAD_DIM), F32),
                   jax.ShapeDtypeStruct((n_seq, h, HEAD_DIM, 1), F32),
                   jax.ShapeDtypeStruct((n_seq, h, 1, 1), F32)],
        compiler_params=_params(("parallel", "parallel")),
        name="mlstm",
    )(proj, proj, proj, proj, *([gt4] * (2 * nh)), c0, n0, m0, g_all)


def _dil_kernel(q_ref, k_ref, v_ref, o_ref, m_ref, l_ref, acc_ref):
    s = q_ref.shape[0]
    blk = CHUNK
    row = lax.broadcasted_iota(jnp.int32, (blk, blk), 0)
    col = lax.broadcasted_iota(jnp.int32, (blk, blk), 1)
    cur_mask = col <= row
    dist = blk + lax.broadcasted_iota(jnp.int32, (blk, 2 * blk), 0) - lax.broadcasted_iota(jnp.int32, (blk, 2 * blk), 1)
    band_mask = jnp.logical_and(dist >= 0, dist <= blk)

    def rows(start, size, d):
        return pl.ds(start, size) if d == 1 else pl.ds(start, size, stride=d)

    for pi, (window, d) in enumerate(DIL_PATTERNS):
        assert window // d == blk and s % (d * blk) == 0
        for r in range(d):
            for n in range(s // (d * blk)):
                start = r + d * blk * n
                q = (q_ref[rows(start, blk, d), :] * QK_SCALE).astype(BF16)
                if n == 0:
                    kv_rows, mask = rows(start, blk, d), cur_mask
                else:
                    kv_rows, mask = rows(start - d * blk, 2 * blk, d), band_mask
                sc = _dot_nt(q, k_ref[kv_rows, :].astype(BF16))
                sc = jnp.where(mask, sc, -jnp.inf)
                m = jnp.max(sc, axis=-1, keepdims=True)
                p = jnp.exp(sc - m)
                out_rows = rows(pi * s + start, blk, d)
                m_ref[out_rows, :] = jnp.broadcast_to(m, (blk, LANES))
                l_ref[out_rows, :] = jnp.broadcast_to(jnp.sum(p, axis=-1, keepdims=True), (blk, LANES))
                acc_ref[out_rows, :] = _dot(p.astype(BF16), v_ref[kv_rows, :].astype(BF16))

    n_pat = len(DIL_PATTERNS)
    ms = [m_ref[pi * s:(pi + 1) * s, :] for pi in range(n_pat)]
    m_all = functools.reduce(jnp.maximum, ms)
    den = jnp.zeros((s, LANES), F32)
    num = jnp.zeros((s, HEAD_DIM), F32)
    for pi in range(n_pat):
        w = jnp.exp(ms[pi] - m_all)
        den = den + w * l_ref[pi * s:(pi + 1) * s, :]
        num = num + w * acc_ref[pi * s:(pi + 1) * s, :]
    o_ref[...] = num / den


def _dil_prompt(proj, slot, n_seq, s):
    h = H_DIL
    n_pat = len(DIL_PATTERNS)
    col = lambda base: pl.BlockSpec((None, s, HEAD_DIM), lambda b, c: (slot, b, base + c))
    return pl.pallas_call(
        _dil_kernel,
        grid=(n_seq, h),
        in_specs=[col(0), col(h), col(2 * h)],
        out_specs=pl.BlockSpec((s, HEAD_DIM), lambda b, c: (b, c)),
        out_shape=jax.ShapeDtypeStruct((n_seq * s, h * HEAD_DIM), F32),
        scratch_shapes=[pltpu.VMEM((n_pat * s, LANES), F32), pltpu.VMEM((n_pat * s, LANES), F32),
                        pltpu.VMEM((n_pat * s, HEAD_DIM), F32)],
        compiler_params=_params(("parallel", "parallel")),
        name="dil_prompt",
    )(proj, proj, proj)


def _dil_decode_kernel(q_ref, k_ref, v_ref, cnt_ref, kn_ref, vn_ref, cntn_ref, o_ref, m_ref, l_ref, acc_ref,
                       *, n_tiles):
    p = pl.program_id(1)

    @pl.when(p == 0)
    def _():
        m_ref[...] = jnp.full(m_ref.shape, -jnp.inf, F32)
        l_ref[...] = jnp.zeros(l_ref.shape, F32)
        acc_ref[...] = jnp.zeros(acc_ref.shape, F32)

    def process(k, v, cnt):
        q = (q_ref[...] * QK_SCALE).astype(BF16)
        s = jnp.where(cnt > 0.0, _dot_nt(q, k.astype(BF16)), -jnp.inf)
        m_old = m_ref[...]
        m_new = jnp.maximum(m_old, jnp.max(s, axis=-1, keepdims=True))
        m_safe = jnp.where(m_new == -jnp.inf, 0.0, m_new)
        alpha = jnp.exp(m_old - m_safe)
        pr = cnt * jnp.exp(s - m_safe)
        l_ref[...] = alpha * l_ref[...] + jnp.sum(pr, axis=-1, keepdims=True)
        acc_ref[...] = alpha * acc_ref[...] + _dot(pr.astype(BF16), v.astype(BF16))
        m_ref[...] = m_new

    @pl.when(p < n_tiles)
    def _():
        process(k_ref[...], v_ref[...], cnt_ref[...])

    @pl.when(p == n_tiles)
    def _():
        process(kn_ref[...], vn_ref[...], cntn_ref[...])
        o_ref[...] = acc_ref[...] / l_ref[...]


def _dil_counts(t, cb, n_keys, key0):
    r = jnp.arange(t * H_DIL, dtype=jnp.int32)[:, None]
    c = jnp.arange(n_keys * H_DIL, dtype=jnp.int32)[None, :]
    delta = cb + r // H_DIL - (key0 + c // H_DIL)
    cnt = jnp.zeros(delta.shape, F32)
    for window, d in DIL_PATTERNS:
        cnt = cnt + ((delta >= 0) & (delta % d == 0) & (delta <= window)).astype(F32)
    return jnp.where(r % H_DIL == c % H_DIL, cnt, 0.0)


def _dil_decode(q, k_cache, v_cache, j, k_new, v_new, t):
    db, rows, hd = q.shape
    cb = k_cache.shape[2] // H_DIL
    tk = DIL_TK
    n_tiles = cb // tk
    nrow = k_new.shape[1]
    cnt = _dil_counts(t, cb, cb, 0)
    cnt_new = _dil_counts(t, cb, nrow // H_DIL, cb)
    seq_map = lambda b, p: (b, 0, 0)
    tile_map = lambda b, p: (j, b, jnp.minimum(p, n_tiles - 1), 0)
    return pl.pallas_call(
        functools.partial(_dil_decode_kernel, n_tiles=n_tiles),
        grid=(db, n_tiles + 1),
        in_specs=[pl.BlockSpec((None, rows, hd), seq_map),
                  pl.BlockSpec((None, None, tk * H_DIL, hd), tile_map),
                  pl.BlockSpec((None, None, tk * H_DIL, hd), tile_map),
                  pl.BlockSpec((rows, tk * H_DIL), lambda b, p: (0, jnp.minimum(p, n_tiles - 1))),
                  pl.BlockSpec((None, nrow, hd), seq_map),
                  pl.BlockSpec((None, nrow, hd), seq_map),
                  pl.BlockSpec((rows, nrow), lambda b, p: (0, 0))],
        out_specs=pl.BlockSpec((None, rows, hd), seq_map),
        out_shape=jax.ShapeDtypeStruct((db, rows, hd), F32),
        scratch_shapes=[pltpu.VMEM((rows, 1), F32), pltpu.VMEM((rows, 1), F32),
                        pltpu.VMEM((rows, hd), F32)],
        compiler_params=_params(("parallel", "arbitrary")),
        name="dil_decode",
    )(q, k_cache, v_cache, cnt, k_new, v_new, cnt_new)


def _pad_rows(a, n_seq, rows, pad):
    w = a.shape[-1]
    return jnp.pad(a.reshape(n_seq, rows, w), ((0, 0), (0, pad - rows), (0, 0)))


def _head_rows(a, n_seq, t, n_heads):
    return a.reshape(n_seq, t * n_heads, HEAD_DIM)


def kernel(x_prompt, x_sample, c_prompt, c_sample, page_table, cache_fox_k, cache_fox_v, cache_fox_logf, state_mlstm_c, state_mlstm_n, state_mlstm_m, cache_win_k, cache_win_v, w_ada, b_ada, norm_pre, norm_mix, norm_post, norm_final, ffn_pre_w13, ffn_pre_w2, ffn_post_w13, ffn_post_w2, ab_w_in, ab_w_out, fox_fgate_b, mlstm_igate_b, mlstm_fgate_b, mlstm_norm_g, c_w_in, c_w_out):
    nb, s, d = x_prompt.shape
    db, t, _ = x_sample.shape
    depth = w_ada.shape[0]
    n_ab, n_c = ab_w_in.shape[0], c_w_in.shape[0]
    ms = db * t
    fw = H_FOX * HEAD_DIM
    mw = H_MLSTM * HEAD_DIM
    dw = H_DIL * HEAD_DIM
    assert s % TM == 0 and ms % SUBLANES == 0 and t <= NEW_KEYS

    xp = x_prompt.reshape(nb * s, d)
    xs = x_sample.reshape(ms, d)

    pad_rows = -(ms + nb) % SUBLANES
    c_rows = jnp.concatenate([jnp.repeat(c_sample, t, axis=0), c_prompt, jnp.zeros((pad_rows, d), F32)], axis=0)
    mods_all = _adaln_all(c_rows, w_ada, b_ada)
    mods_p = mods_all[:, ms:ms + nb].reshape(depth * nb * N_MOD, 1, d)
    mods_s = mods_all[:, :ms]

    norm_pre3, norm_mix3, norm_post3 = (a.reshape(depth, 1, d) for a in (norm_pre, norm_mix, norm_post))

    o1, o2, o3 = 3 * fw, 3 * fw + H_FOX, 3 * fw + H_FOX + 3 * mw
    o4 = o3 + 2 * H_MLSTM
    ab_main = jnp.concatenate([ab_w_in[:, :, :o1], ab_w_in[:, :, o2:o3], ab_w_in[:, :, o4:]], axis=2)
    ab_gate = jnp.concatenate([ab_w_in[:, :, o1:o2], ab_w_in[:, :, o3:o4],
                               jnp.zeros((n_ab, d, LANES - H_FOX - 2 * H_MLSTM), F32)], axis=2)
    gate_bias = jnp.concatenate([fox_fgate_b, mlstm_igate_b, mlstm_fgate_b, jnp.zeros((n_ab, 8), F32)],
                                axis=1).reshape(n_ab, 32, 1)
    head_gain = mlstm_norm_g.reshape(n_ab, H_MLSTM, 1, HEAD_DIM)

    n_pool, page = cache_fox_k.shape[1], cache_fox_k.shape[2]
    prow = page * H_FOX
    k_pool = cache_fox_k.reshape(n_ab, n_pool, prow, HEAD_DIM)
    v_pool = cache_fox_v.reshape(n_ab, n_pool, prow, HEAD_DIM)
    lf_t = cache_fox_logf.transpose(0, 1, 3, 2).reshape(n_ab * n_pool * H_FOX, page)
    cl_t = _page_cumsum(lf_t).reshape(n_ab, n_pool, H_FOX, page)
    cl = cl_t.transpose(0, 1, 3, 2)
    tot = jnp.broadcast_to(cl[:, :, page - 1:page, :], cl.shape)
    cs_pool = jnp.stack([cl.reshape(n_ab, n_pool, prow), tot.reshape(n_ab, n_pool, prow)], axis=2)

    cb = cache_win_k.shape[2]
    k_win = cache_win_k.reshape(n_c, db, cb * H_DIL, HEAD_DIM)
    v_win = cache_win_v.reshape(n_c, db, cb * H_DIL, HEAD_DIM)

    zeros_c = jnp.zeros((nb, H_MLSTM, HEAD_DIM, HEAD_DIM), F32)
    zeros_n = jnp.zeros((nb, H_MLSTM, HEAD_DIM, 1), F32)
    zeros_m = jnp.zeros((nb, H_MLSTM, 1, 1), F32)

    outs = {name: [] for name in ("fl_p", "fk_s", "fv_s", "fl_s", "mc_p", "mn_p", "mm_p",
                                  "mc_s", "mn_s", "mm_s", "wk_new", "wv_new")}
    pp_ab = pp_c = None

    for l in range(depth):
        j = l // 2
        last = l == depth - 1
        mp = _Mods(mods_p, l, True, TM, nb, s)
        md = _Mods(mods_s, l, False, ms)

        xp, xs = _ffn(xp, xs, mp, md, (0, 1, 2), norm_pre3, ffn_pre_w13, ffn_pre_w2, l)

        if l % 2 == 0:
            pp_ab, gp = _proj_in(xp, mp, (3, 4), norm_mix3, l, ab_main, ab_gate, j, j, n_ab, pp_ab)
            gt = _gates(gp, gate_bias[j], nb, s, s)
            gt4 = gt.reshape(nb, 32, 1, s)
            o_fox = _fox_prompt(pp_ab, j, gt4, nb, s)
            hn, c_p, n_p, m_p = _mlstm(pp_ab, j, gt4, zeros_c, zeros_n, zeros_m, head_gain, j, nb, s)
            xp = _proj_out(o_fox, 0, hn, 0, ab_w_out, j, xp, mp, 5)
            outs["fl_p"].append(gt[:, 24:32, :].transpose(0, 2, 1))
            outs["mc_p"].append(c_p)
            outs["mn_p"].append(n_p.reshape(nb, H_MLSTM, HEAD_DIM))
            outs["mm_p"].append(m_p.reshape(nb, H_MLSTM))

            ps, gs = _proj_in(xs, md, (3, 4), norm_mix3, l, ab_main, ab_gate, j)
            ps = ps[0]
            gts = _gates(_pad_rows(gs, db, t, CHUNK).reshape(db * CHUNK, LANES), gate_bias[j], db, CHUNK, t)
            gts4 = gts.reshape(db, 32, 1, CHUNK)
            nk = NEW_KEYS
            c_new = gts[:, 0:8, :nk].transpose(0, 2, 1).reshape(db, 1, nk * H_FOX)
            o_dec = _fox_decode(page_table, _head_rows(ps[:, :fw], db, t, H_FOX), k_pool, v_pool, cs_pool, j,
                                _pad_rows(_head_rows(ps[:, fw:2 * fw], db, t, H_FOX).reshape(db * t * H_FOX, HEAD_DIM),
                                          db, t * H_FOX, nk * H_FOX),
                                _pad_rows(_head_rows(ps[:, 2 * fw:3 * fw], db, t, H_FOX).reshape(db * t * H_FOX, HEAD_DIM),
                                          db, t * H_FOX, nk * H_FOX),
                                c_new)
            o_fox_s = o_dec.reshape(ms, fw)
            ps_pad = _pad_rows(ps, db, t, CHUNK).reshape(1, db * CHUNK, -1)
            hn_s, c_s, n_s, m_s = _mlstm(ps_pad, 0, gts4, state_mlstm_c[j],
                                         state_mlstm_n[j].reshape(db, H_MLSTM, HEAD_DIM, 1),
                                         state_mlstm_m[j].reshape(db, H_MLSTM, 1, 1),
                                         head_gain, j, db, CHUNK)
            hn_s = hn_s.reshape(db, CHUNK, mw)[:, :t].reshape(ms, mw)
            xs = _proj_out(o_fox_s, 0, hn_s, 0, ab_w_out, j, xs, md, 5)
            outs["fk_s"].append(ps[:, fw:2 * fw].reshape(db, t, H_FOX, HEAD_DIM))
            outs["fv_s"].append(ps[:, 2 * fw:3 * fw].reshape(db, t, H_FOX, HEAD_DIM))
            outs["fl_s"].append(gts[:, 24:32, :t].transpose(0, 2, 1))
            outs["mc_s"].append(c_s)
            outs["mn_s"].append(n_s.reshape(db, H_MLSTM, HEAD_DIM))
            outs["mm_s"].append(m_s.reshape(db, H_MLSTM))
        else:
            pp_c, _ = _proj_in(xp, mp, (3, 4), norm_mix3, l, c_w_in, None, j, j, n_c, pp_c)
            o_dil = _dil_prompt(pp_c, j, nb, s)
            xp = _proj_out(o_dil, 0, o_dil, 1, c_w_out, j, xp, mp, 5)

            ps, _ = _proj_in(xs, md, (3, 4), norm_mix3, l, c_w_in, None, j)
            ps = ps[0]
            k_new = _head_rows(ps[:, dw:2 * dw], db, t, H_DIL)
            v_new = _head_rows(ps[:, 2 * dw:], db, t, H_DIL)
            nrow = NEW_KEYS * H_DIL
            o_dec = _dil_decode(_head_rows(ps[:, :dw], db, t, H_DIL), k_win, v_win, j,
                                jnp.pad(k_new, ((0, 0), (0, nrow - t * H_DIL), (0, 0))),
                                jnp.pad(v_new, ((0, 0), (0, nrow - t * H_DIL), (0, 0))), t)
            o_dil_s = o_dec.reshape(ms, dw)
            xs = _proj_out(o_dil_s, 0, o_dil_s, 1, c_w_out, j, xs, md, 5)
            outs["wk_new"].append(k_new.reshape(db, t, H_DIL, HEAD_DIM))
            outs["wv_new"].append(v_new.reshape(db, t, H_DIL, HEAD_DIM))

        fin = norm_final if last else None
        xp, xs = _ffn(xp, xs, mp, md, (6, 7, 8), norm_post3, ffn_post_w13, ffn_post_w2, l, fin)

    st = lambda name: jnp.stack(outs[name])
    keep = min(DIL_PATTERNS[-1][0], cb + t)
    wk_s = jnp.concatenate([cache_win_k, st("wk_new")], axis=2)[:, :, cb + t - keep:]
    wv_s = jnp.concatenate([cache_win_v, st("wv_new")], axis=2)[:, :, cb + t - keep:]
    keep_p = min(DIL_PATTERNS[-1][0], s)
    heads = lambda a, n_heads: a.reshape(a.shape[0], nb, s, n_heads, HEAD_DIM)
    return (xp.reshape(nb, s, d), xs.reshape(db, t, d),
            heads(pp_ab[:, :, fw:2 * fw], H_FOX), heads(pp_ab[:, :, 2 * fw:3 * fw], H_FOX),
            st("fl_p"), st("fk_s"), st("fv_s"), st("fl_s"),
            st("mc_p"), st("mn_p"), st("mm_p"), st("mc_s"), st("mn_s"), st("mm_s"),
            heads(pp_c[:, :, dw:2 * dw], H_DIL)[:, :, s - keep_p:], heads(pp_c[:, :, 2 * dw:], H_DIL)[:, :, s - keep_p:],
            wk_s, wv_s)
```

```python
import functools

import jax
import jax.numpy as jnp
from jax import lax
from jax.experimental import pallas as pl
from jax.experimental.pallas import tpu as pltpu

F32 = jnp.float32
BF16 = jnp.bfloat16

HEAD_DIM = 128
H_FOX = 8
H_MLSTM = 8
H_DIL = 16
N_MOD = 9
EPS = 1e-6
FFN_RES = 0.5
QK_SCALE = HEAD_DIM ** -0.5
DIL_PATTERNS = ((128, 1), (512, 4), (2048, 16))
CHUNK = 128
NEG = -1e30

LANES = 128
SUBLANES = 8
VMEM_BIG = 60 * 1024 * 1024
VMEM_MID = 48 * 1024 * 1024

TM = 1024
TF = 256
TN = 512
TN_IN = 1024
TQ = 256
MLSTM_HEADS = 2
PAGES_PER_STEP = 8
DIL_TK = 512
NEW_KEYS = 16


def _params(sem, vmem=VMEM_MID):
    return pltpu.CompilerParams(dimension_semantics=sem, vmem_limit_bytes=vmem)


def _sigmoid(x):
    return 1.0 / (1.0 + jnp.exp(-x))


def _log_sigmoid(x):
    return jnp.minimum(x, 0.0) - jnp.log1p(jnp.exp(-jnp.abs(x)))


def _dot(a, b):
    return jnp.dot(a, b, preferred_element_type=F32)


def _dot_nt(a, b):
    return lax.dot_general(a, b, (((1,), (1,)), ((), ())), preferred_element_type=F32)


def _rms(x):
    return x * lax.rsqrt(jnp.mean(x * x, axis=-1, keepdims=True) + EPS)


def _norm_mod(x, g, shift, scale):
    return _rms(x) * g * (1.0 + scale) + shift


def _cumsum_lanes(x):
    lane = lax.broadcasted_iota(jnp.int32, x.shape, 1)
    sh = 1
    while sh < x.shape[1]:
        x = x + jnp.where(lane >= sh, pltpu.roll(x, sh, 1), 0.0)
        sh *= 2
    return x


class _Mods:
    def __init__(self, arr, layer, prompt, tm, n_seq=None, seq_rows=None):
        self.arr, self.layer, self.prompt, self.tm, self.n_seq, self.seq_rows = arr, layer, prompt, tm, n_seq, seq_rows

    def spec(self, k, tn=None):
        d = self.arr.shape[-1] if self.prompt else self.arr.shape[-1] // N_MOD
        l, tm = self.layer, self.tm
        if self.prompt:
            n_seq, seq_rows = self.n_seq, self.seq_rows
            row = lambda i: (l * n_seq + (i * tm) // seq_rows) * N_MOD + k
            if tn is None:
                return pl.BlockSpec((None, 1, d), lambda i, *_: (row(i), 0, 0))
            return pl.BlockSpec((None, 1, tn), lambda i, j: (row(i), 0, j))
        if tn is None:
            return pl.BlockSpec((None, tm, d), lambda i, *_: (l, 0, k))
        return pl.BlockSpec((None, tm, tn), lambda i, j: (l, 0, k * (d // tn) + j))


def _adaln_kernel(c_ref, w_ref, b_ref, o_ref):
    c = c_ref[...]
    s = (c * _sigmoid(c)).astype(BF16)
    o_ref[...] = _dot(s, w_ref[...].astype(BF16)) + b_ref[...]


def _adaln_all(c_rows, w_ada, b_ada):
    depth, d, n = w_ada.shape
    rows = c_rows.shape[0]
    tn = 1024
    return pl.pallas_call(
        _adaln_kernel,
        grid=(depth, n // tn),
        in_specs=[pl.BlockSpec((rows, d), lambda l, j: (0, 0)),
                  pl.BlockSpec((None, d, tn), lambda l, j: (l, 0, j)),
                  pl.BlockSpec((None, 1, tn), lambda l, j: (l, 0, j))],
        out_specs=pl.BlockSpec((None, rows, tn), lambda l, j: (l, 0, j)),
        out_shape=jax.ShapeDtypeStruct((depth, rows, n), F32),
        compiler_params=_params(("parallel", "parallel")),
        name="adaln",
    )(c_rows, w_ada, b_ada.reshape(depth, 1, n))


def _ffn_kernel(x_ref, sh_ref, sc_ref, gt_ref, xs_ref, shs_ref, scs_ref, gts_ref, g_ref, w13_hbm, w2_hbm,
                *rest, layer, d_ff, nf, tf, rem, final, tm, ni, shift_rows):
    rest = list(rest)
    gf_ref = rest.pop(0) if final else None
    shift_src = [rest.pop(0) for _ in shift_rows]
    o_ref, os_ref = rest.pop(0), rest.pop(0)
    shift_dst = [rest.pop(0) for _ in shift_rows]
    h_ref, w1_buf, w3_buf, w2_buf, sem = rest[:5]
    i = pl.program_id(0)

    def shift_copies():
        shift_sem = rest[5]
        out = []
        for a, (src, dst, (t, rows)) in enumerate(zip(shift_src, shift_dst, shift_rows)):
            for j in range(src.shape[0]):
                out.append(pltpu.make_async_copy(src.at[j, :, pl.ds(t, rows - t)], dst.at[j, :, pl.ds(0, rows - t)],
                                                 shift_sem.at[a, j]))
        return out

    if shift_rows:
        @pl.when(i == 0)
        def _():
            for c in shift_copies():
                c.start()

    def tile_copies(f, slot, width):
        c0 = f * tf if isinstance(f, int) else pl.multiple_of(f * tf, tf)
        return (pltpu.make_async_copy(w13_hbm.at[layer, :, pl.ds(c0, width)],
                                      w1_buf.at[slot, :, pl.ds(0, width)], sem.at[slot, 0]),
                pltpu.make_async_copy(w13_hbm.at[layer, :, pl.ds(d_ff + c0, width)],
                                      w3_buf.at[slot, :, pl.ds(0, width)], sem.at[slot, 1]),
                pltpu.make_async_copy(w2_hbm.at[layer, pl.ds(c0, width), :],
                                      w2_buf.at[slot, pl.ds(0, width), :], sem.at[slot, 2]))

    def start(f, slot, width):
        for c in tile_copies(f, slot, width):
            c.start()

    def wait(f, slot, width):
        for c in tile_copies(f, slot, width):
            c.wait()

    def finish(x, gate, acc):
        y = x + (FFN_RES * gate) * acc
        return _rms(y) * gf_ref[...] if final else y

    def run(rows):
        with_sample = rows > tm

        def part(w1, w3, w2):
            h = h_ref[:rows, :]
            a = _dot(h, w1.astype(BF16))
            g = _dot(h, w3.astype(BF16))
            u = (a * _sigmoid(a) * g).astype(BF16)
            return _dot(u, w2.astype(BF16))

        start(0, 0, tf)
        h_ref[:tm, :] = _norm_mod(x_ref[...], g_ref[...], sh_ref[...], sc_ref[...]).astype(BF16)
        if with_sample:
            h_ref[tm:, :] = _norm_mod(xs_ref[...], g_ref[...], shs_ref[...], scs_ref[...]).astype(BF16)

        wait(0, 0, tf)
        start(1, 1, tf)
        p = part(w1_buf[0], w3_buf[0], w2_buf[0])
        o_ref[...] = p[:tm]
        if with_sample:
            os_ref[...] = p[tm:]

        def body(f, carry):
            slot = lax.rem(f, 2)
            wait(f, slot, tf)

            @pl.when(f + 1 < nf - 1)
            def _():
                start(f + 1, 1 - slot, tf)

            @pl.when(f + 1 == nf - 1)
            def _():
                start(nf - 1, 1 - slot, rem)

            p = part(w1_buf[slot], w3_buf[slot], w2_buf[slot])
            o_ref[...] += p[:tm]
            if with_sample:
                os_ref[...] += p[tm:]
            return carry

        lax.fori_loop(1, nf - 1, body, 0)

        slot = (nf - 1) % 2
        wait(nf - 1, slot, rem)
        p = part(w1_buf[slot, :, :rem], w3_buf[slot, :, :rem], w2_buf[slot, :rem, :])
        o_ref[...] = finish(x_ref[...], gt_ref[...], o_ref[...] + p[:tm])
        if with_sample:
            os_ref[...] = finish(xs_ref[...], gts_ref[...], os_ref[...] + p[tm:])

    @pl.when(i < ni - 1)
    def _():
        run(tm)

    @pl.when(i == ni - 1)
    def _():
        run(h_ref.shape[0])
        for c in shift_copies() if shift_rows else ():
            c.wait()


def _ffn(x, xs, mods, mods_s, ks, g_all, w13_all, w2_all, layer, final_g=None, shift=()):
    m, d = x.shape
    ms = xs.shape[0]
    tm = mods.tm
    d_ff = w2_all.shape[1]
    tf = TF
    nf = pl.cdiv(d_ff, tf)
    rem = d_ff - (nf - 1) * tf
    ni = m // tm
    assert nf >= 3 and m % tm == 0 and d_ff % LANES == 0 and ni >= 2 and mods_s.tm == ms
    final = final_g is not None
    in_specs = [pl.BlockSpec((tm, d), lambda i: (i, 0)),
                mods.spec(ks[0]), mods.spec(ks[1]), mods.spec(ks[2]),
                pl.BlockSpec((ms, d), lambda i: (0, 0)),
                mods_s.spec(ks[0]), mods_s.spec(ks[1]), mods_s.spec(ks[2]),
                pl.BlockSpec((None, 1, d), lambda i: (layer, 0, 0)),
                pl.BlockSpec(memory_space=pl.ANY),
                pl.BlockSpec(memory_space=pl.ANY)]
    args = [x, mods.arr, mods.arr, mods.arr, xs, mods_s.arr, mods_s.arr, mods_s.arr, g_all, w13_all, w2_all]
    if final:
        in_specs.append(pl.BlockSpec((1, d), lambda i: (0, 0)))
        args.append(final_g.reshape(1, d))
    out_specs = [pl.BlockSpec((tm, d), lambda i: (i, 0)), pl.BlockSpec((ms, d), lambda i: (0, 0))]
    out_shape = [jax.ShapeDtypeStruct((m, d), F32), jax.ShapeDtypeStruct((ms, d), F32)]
    scratch = [pltpu.VMEM((tm + ms, d), BF16),
               pltpu.VMEM((2, d, tf), F32), pltpu.VMEM((2, d, tf), F32), pltpu.VMEM((2, tf, d), F32),
               pltpu.SemaphoreType.DMA((2, 3))]
    for buf, _ in shift:
        in_specs.append(pl.BlockSpec(memory_space=pl.ANY))
        args.append(buf)
        out_specs.append(pl.BlockSpec(memory_space=pl.ANY))
        out_shape.append(jax.ShapeDtypeStruct(buf.shape, buf.dtype))
    if shift:
        scratch.append(pltpu.SemaphoreType.DMA((len(shift), max(buf.shape[0] for buf, _ in shift))))
    shift_rows = tuple((t, buf.shape[2]) for buf, t in shift)
    return pl.pallas_call(
        functools.partial(_ffn_kernel, layer=layer, d_ff=d_ff, nf=nf, tf=tf, rem=rem, final=final, tm=tm, ni=ni,
                          shift_rows=shift_rows),
        grid=(ni,),
        in_specs=in_specs,
        out_specs=out_specs,
        out_shape=out_shape,
        scratch_shapes=scratch,
        compiler_params=_params(("arbitrary",), VMEM_BIG),
        name="ffn",
    )(*args)


def _proj_in_kernel(x_ref, sh_ref, sc_ref, xs_ref, shs_ref, scs_ref, g_ref, w_ref, *rest, gates, aliased, tm, ni):
    rest = list(rest)
    wg_ref = rest.pop(0) if gates else None
    if aliased:
        rest.pop(0)
    if gates:
        o_ref, os_ref, og_ref, ogs_ref, h_ref = rest
    else:
        o_ref, os_ref, h_ref = rest
    i = pl.program_id(0)
    n = pl.program_id(1)

    @pl.when(n == 0)
    def _():
        h = _norm_mod(x_ref[...], g_ref[...], sh_ref[...], sc_ref[...]).astype(BF16)
        h_ref[:tm, :] = h
        if gates:
            og_ref[...] = _dot(h, wg_ref[...].astype(BF16))

    @pl.when(jnp.logical_and(n == 0, i == ni - 1))
    def _():
        hs = _norm_mod(xs_ref[...], g_ref[...], shs_ref[...], scs_ref[...]).astype(BF16)
        h_ref[tm:, :] = hs
        if gates:
            ogs_ref[...] = _dot(hs, wg_ref[...].astype(BF16))

    @pl.when(i < ni - 1)
    def _():
        o_ref[...] = _dot(h_ref[:tm, :], w_ref[...].astype(BF16))

    @pl.when(i == ni - 1)
    def _():
        y = _dot(h_ref[...], w_ref[...].astype(BF16))
        o_ref[...] = y[:tm]
        os_ref[...] = y[tm:]


def _proj_in(x, xs, mods, mods_s, ks, g_all, layer, w_all, wg_all, j, slot=0, n_slots=1, stacked=None):
    m, d = x.shape
    ms = xs.shape[0]
    tm = mods.tm
    n = w_all.shape[2]
    tn = TN_IN
    ni = m // tm
    assert n % tn == 0 and m % tm == 0 and (stacked is None) == (slot == 0) and ni >= 2 and mods_s.tm == ms
    gates = wg_all is not None
    aliased = stacked is not None
    s_col = lambda i, c: jnp.where(i == ni - 1, c, 0)
    in_specs = [pl.BlockSpec((tm, d), lambda i, c: (i, 0)), mods.spec(ks[0]), mods.spec(ks[1]),
                pl.BlockSpec((ms, d), lambda i, c: (0, 0)), mods_s.spec(ks[0]), mods_s.spec(ks[1]),
                pl.BlockSpec((None, 1, d), lambda i, c: (layer, 0, 0)),
                pl.BlockSpec((None, d, tn), lambda i, c: (j, 0, c))]
    out_specs = [pl.BlockSpec((None, tm, tn), lambda i, c: (slot, i, c)),
                 pl.BlockSpec((ms, tn), lambda i, c: (0, s_col(i, c)))]
    out_shape = [jax.ShapeDtypeStruct((n_slots, m, n), F32), jax.ShapeDtypeStruct((ms, n), F32)]
    args = [x, mods.arr, mods.arr, xs, mods_s.arr, mods_s.arr, g_all, w_all]
    if gates:
        in_specs.append(pl.BlockSpec((None, d, LANES), lambda i, c: (j, 0, 0)))
        out_specs += [pl.BlockSpec((tm, LANES), lambda i, c: (i, 0)), pl.BlockSpec((ms, LANES), lambda i, c: (0, 0))]
        out_shape += [jax.ShapeDtypeStruct((m, LANES), F32), jax.ShapeDtypeStruct((ms, LANES), F32)]
        args.append(wg_all)
    if aliased:
        in_specs.append(pl.BlockSpec(memory_space=pl.ANY))
        args.append(stacked)
    return pl.pallas_call(
        functools.partial(_proj_in_kernel, gates=gates, aliased=aliased, tm=tm, ni=ni),
        grid=(ni, n // tn),
        in_specs=in_specs, out_specs=out_specs, out_shape=out_shape,
        input_output_aliases={len(args) - 1: 0} if aliased else {},
        scratch_shapes=[pltpu.VMEM((tm + ms, d), BF16)],
        compiler_params=_params(("arbitrary", "arbitrary"), VMEM_BIG),
        name="proj_in",
    )(*args)


def _proj_out_kernel(a_ref, b_ref, wa_ref, wb_ref, x_ref, gt_ref, o_ref, ab_ref, bb_ref):
    n = pl.program_id(1)

    @pl.when(n == 0)
    def _():
        ab_ref[...] = a_ref[...].astype(BF16)
        bb_ref[...] = b_ref[...].astype(BF16)

    y = _dot(ab_ref[...], wa_ref[...].astype(BF16)) + _dot(bb_ref[...], wb_ref[...].astype(BF16))
    o_ref[...] = x_ref[...] + gt_ref[...] * y


def _proj_out(mix_a, a_blk, mix_b, b_blk, w_all, j, x, mods, k):
    m, d = x.shape
    tm = mods.tm
    kh = w_all.shape[1] // 2
    tn = TN
    return pl.pallas_call(
        _proj_out_kernel,
        grid=(m // tm, d // tn),
        in_specs=[pl.BlockSpec((tm, kh), lambda i, c: (i, a_blk)),
                  pl.BlockSpec((tm, kh), lambda i, c: (i, b_blk)),
                  pl.BlockSpec((None, kh, tn), lambda i, c: (j, 0, c)),
                  pl.BlockSpec((None, kh, tn), lambda i, c: (j, 1, c)),
                  pl.BlockSpec((tm, tn), lambda i, c: (i, c)),
                  mods.spec(k, tn)],
        out_specs=pl.BlockSpec((tm, tn), lambda i, c: (i, c)),
        out_shape=jax.ShapeDtypeStruct((m, d), F32),
        scratch_shapes=[pltpu.VMEM((tm, kh), BF16), pltpu.VMEM((tm, kh), BF16)],
        compiler_params=_params(("parallel", "arbitrary")),
        name="proj_out",
    )(mix_a, mix_b, w_all, w_all, x, mods.arr)


def _gates_kernel(p_ref, b_ref, o_ref, *, n_valid):
    s = p_ref.shape[0]
    lane = lax.broadcasted_iota(jnp.int32, (8, LANES), 1)
    carry = jnp.zeros((8, 1), F32)
    for blk in range(s // LANES):
        sl = slice(blk * LANES, (blk + 1) * LANES)
        pre = p_ref[sl, :].T[0:32] + b_ref[...]
        valid = (lane + blk * LANES) < n_valid
        lf_f = jnp.where(valid, _log_sigmoid(pre[0:8]), 0.0)
        ig = jnp.where(valid, pre[8:16], NEG)
        lf_m = jnp.where(valid, _log_sigmoid(pre[16:24]), 0.0)
        c_f = _cumsum_lanes(lf_f) + carry
        carry = c_f[:, LANES - 1:LANES]
        o_ref[0:8, sl] = c_f
        o_ref[8:16, sl] = ig
        o_ref[16:24, sl] = _cumsum_lanes(lf_m)
        o_ref[24:32, sl] = lf_f


def _gates(pre, bias, n_seq, s, n_valid):
    return pl.pallas_call(
        functools.partial(_gates_kernel, n_valid=n_valid),
        grid=(n_seq,),
        in_specs=[pl.BlockSpec((s, LANES), lambda b: (b, 0)),
                  pl.BlockSpec((32, 1), lambda b: (0, 0))],
        out_specs=pl.BlockSpec((None, 32, s), lambda b: (b, 0, 0)),
        out_shape=jax.ShapeDtypeStruct((n_seq, 32, s), F32),
        compiler_params=_params(("parallel",)),
        name="gates",
    )(pre, bias)


def _fox_kernel(q_ref, k_ref, v_ref, c_ref, o_ref, kb_ref, vb_ref, *, tq):
    s = q_ref.shape[0]
    kb_ref[...] = k_ref[...].astype(BF16)
    vb_ref[...] = v_ref[...].astype(BF16)
    row = lax.broadcasted_iota(jnp.int32, (tq, tq), 0)
    col = lax.broadcasted_iota(jnp.int32, (tq, tq), 1)
    for qi in range(s // tq):
        q = (q_ref[qi * tq:(qi + 1) * tq, :] * QK_SCALE).astype(BF16)
        m = l = acc = None
        for kb in range(qi + 1):
            ks = slice(kb * tq, (kb + 1) * tq)
            sc = _dot_nt(q, kb_ref[ks, :]) - c_ref[:, ks]
            if kb == qi:
                sc = jnp.where(col <= row, sc, -jnp.inf)
            bm = jnp.max(sc, axis=-1, keepdims=True)
            if m is None:
                m = bm
                p = jnp.exp(sc - m)
                l = jnp.sum(p, axis=-1, keepdims=True)
                acc = _dot(p.astype(BF16), vb_ref[ks, :])
            else:
                m_new = jnp.maximum(m, bm)
                alpha = jnp.exp(m - m_new)
                p = jnp.exp(sc - m_new)
                l = alpha * l + jnp.sum(p, axis=-1, keepdims=True)
                acc = alpha * acc + _dot(p.astype(BF16), vb_ref[ks, :])
                m = m_new
        o_ref[qi * tq:(qi + 1) * tq, :] = acc / l


def _fox_prompt(proj, slot, gt4, n_seq, s):
    col = lambda base: pl.BlockSpec((None, s, HEAD_DIM), lambda b, h: (slot, b, base + h))
    return pl.pallas_call(
        functools.partial(_fox_kernel, tq=TQ),
        grid=(n_seq, H_FOX),
        in_specs=[col(0), col(H_FOX), col(2 * H_FOX),
                  pl.BlockSpec((None, None, 1, s), lambda b, h: (b, h, 0, 0))],
        out_specs=pl.BlockSpec((s, HEAD_DIM), lambda b, h: (b, h)),
        out_shape=jax.ShapeDtypeStruct((n_seq * s, H_FOX * HEAD_DIM), F32),
        scratch_shapes=[pltpu.VMEM((s, HEAD_DIM), BF16), pltpu.VMEM((s, HEAD_DIM), BF16)],
        compiler_params=_params(("parallel", "parallel")),
        name="fox_prompt",
    )(proj, proj, proj, gt4)


def _page_cumsum_kernel(x_ref, o_ref):
    o_ref[...] = _cumsum_lanes(x_ref[...])


def _page_cumsum(lf_t):
    rows, page = lf_t.shape
    tr = 1024
    assert rows % tr == 0 and page == LANES
    return pl.pallas_call(
        _page_cumsum_kernel,
        grid=(rows // tr,),
        in_specs=[pl.BlockSpec((tr, page), lambda i: (i, 0))],
        out_specs=pl.BlockSpec((tr, page), lambda i: (i, 0)),
        out_shape=jax.ShapeDtypeStruct((rows, page), F32),
        compiler_params=_params(("parallel",)),
        name="page_cumsum",
    )(lf_t)


def _fox_decode_kernel(pt_ref, q_ref, *refs, g, n_steps):
    del pt_ref
    k_refs, v_refs, cs_refs = refs[:g], refs[g:2 * g], refs[2 * g:3 * g]
    kn_ref, vn_ref, cn_ref, o_ref, m_ref, l_ref, acc_ref, carry_ref = refs[3 * g:]
    p = pl.program_id(1)
    rows = q_ref.shape[0]
    hbits = H_FOX.bit_length() - 1

    @pl.when(p == 0)
    def _():
        m_ref[...] = jnp.full(m_ref.shape, -jnp.inf, F32)
        l_ref[...] = jnp.zeros(l_ref.shape, F32)
        acc_ref[...] = jnp.zeros(acc_ref.shape, F32)
        carry_ref[...] = jnp.zeros(carry_ref.shape, F32)

    def iotas(ncols):
        return (lax.broadcasted_iota(jnp.int32, (rows, ncols), 0),
                lax.broadcasted_iota(jnp.int32, (rows, ncols), 1))

    def attend(blocks, mask):
        q = (q_ref[...] * QK_SCALE).astype(BF16)
        scores = [jnp.where(mask, _dot_nt(q, k.astype(BF16)) - bias, -jnp.inf) for k, _, bias in blocks]
        m_old = m_ref[...]
        m_new = m_old
        for sc in scores:
            m_new = jnp.maximum(m_new, jnp.max(sc, axis=-1, keepdims=True))
        alpha = jnp.exp(m_old - m_new)
        l = alpha * l_ref[...]
        acc = alpha * acc_ref[...]
        for sc, (_, v, _) in zip(scores, blocks):
            pr = jnp.exp(sc - m_new)
            l = l + jnp.sum(pr, axis=-1, keepdims=True)
            acc = acc + _dot(pr.astype(BF16), v.astype(BF16))
        m_ref[...] = m_new
        l_ref[...] = l
        acc_ref[...] = acc

    @pl.when(p < n_steps)
    def _():
        r, c = iotas(k_refs[0].shape[0])
        mask = (r & (H_FOX - 1)) == (c & (H_FOX - 1))
        carry = carry_ref[...]
        blocks = []
        for i in range(g):
            cs = cs_refs[i][...]
            blocks.append((k_refs[i][...], v_refs[i][...], carry + cs[0:1]))
            carry = carry + cs[1:2]
        carry_ref[...] = carry
        attend(blocks, mask)

    @pl.when(p == n_steps)
    def _():
        ncol = kn_ref.shape[0]
        r, c = iotas(ncol)
        same_head = (r & (H_FOX - 1)) == (c & (H_FOX - 1))
        causal = lax.shift_right_logical(c, hbits) <= lax.shift_right_logical(r, hbits)
        bias = carry_ref[:, :ncol] + cn_ref[...]
        attend([(kn_ref[...], vn_ref[...], bias)], jnp.logical_and(same_head, causal))
        o_ref[...] = acc_ref[...] / l_ref[...]


def _fox_decode(page_table, q, k_pool, v_pool, cs_pool, j, k_new, v_new, c_new):
    db, rows, hd = q.shape
    n_pages = page_table.shape[1]
    prow = k_pool.shape[2]
    g = PAGES_PER_STEP
    assert n_pages % g == 0
    n_steps = n_pages // g
    nrow = k_new.shape[1]

    def pool_map(i):
        return lambda b, p, pt: (j, pt[b, jnp.minimum(p * g + i, n_pages - 1)], 0, 0)

    seq_map = lambda b, p, pt: (b, 0, 0)
    in_specs = [pl.BlockSpec((None, rows, hd), seq_map)]
    in_specs += [pl.BlockSpec((None, None, prow, hd), pool_map(i)) for i in range(g)]
    in_specs += [pl.BlockSpec((None, None, prow, hd), pool_map(i)) for i in range(g)]
    in_specs += [pl.BlockSpec((None, None, 2, prow), pool_map(i)) for i in range(g)]
    in_specs += [pl.BlockSpec((None, nrow, hd), seq_map), pl.BlockSpec((None, nrow, hd), seq_map),
                 pl.BlockSpec((None, 1, nrow), seq_map)]
    grid_spec = pltpu.PrefetchScalarGridSpec(
        num_scalar_prefetch=1,
        grid=(db, n_steps + 1),
        in_specs=in_specs,
        out_specs=pl.BlockSpec((None, rows, hd), seq_map),
        scratch_shapes=[pltpu.VMEM((rows, 1), F32), pltpu.VMEM((rows, 1), F32),
                        pltpu.VMEM((rows, hd), F32), pltpu.VMEM((1, prow), F32)])
    return pl.pallas_call(
        functools.partial(_fox_decode_kernel, g=g, n_steps=n_steps),
        grid_spec=grid_spec,
        out_shape=jax.ShapeDtypeStruct((db, rows, hd), F32),
        compiler_params=_params(("parallel", "arbitrary")),
        name="fox_decode",
    )(page_table, q, *([k_pool] * g), *([v_pool] * g), *([cs_pool] * g), k_new, v_new, c_new)


def _mlstm_kernel(q_ref, k_ref, v_ref, og_ref, *refs, nh):
    ig_refs, b_refs = refs[:nh], refs[nh:2 * nh]
    c0_ref, n0_ref, m0_ref, g_ref, hn_ref, c_ref, n_ref, m_ref = refs[2 * nh:]
    s = q_ref.shape[0]
    ln = CHUNK
    row = lax.broadcasted_iota(jnp.int32, (ln, ln), 0)
    col = lax.broadcasted_iota(jnp.int32, (ln, ln), 1)
    causal = col <= row

    last_lane = lax.broadcasted_iota(jnp.int32, (1, ln), 1) == ln - 1
    assert ln == HEAD_DIM == LANES
    for hh in range(nh):
        hs = slice(hh * HEAD_DIM, (hh + 1) * HEAD_DIM)
        c_st = c0_ref[hh]
        n_st = jnp.broadcast_to(n0_ref[hh], (HEAD_DIM, LANES))
        m_st = jnp.broadcast_to(m0_ref[hh], (1, LANES))
        for ci in range(s // ln):
            ts = slice(ci * ln, (ci + 1) * ln)
            qb = q_ref[ts, hs].astype(BF16)
            k_t = (k_ref[ts, hs] * QK_SCALE).T
            vb = v_ref[ts, hs].astype(BF16)
            b_row = b_refs[hh][:, ts]
            ig_row = ig_refs[hh][:, ts]
            b_mat = jnp.broadcast_to(b_row, (ln, ln))
            b_col = b_mat.T
            dmat = jnp.where(causal, b_col - b_mat + ig_row, NEG)
            inter = b_col + m_st
            m_tok = jnp.maximum(inter, jnp.max(dmat, axis=-1, keepdims=True))
            w_inter = jnp.exp(inter - m_tok)
            qk = _dot(qb, k_t.astype(BF16)) * jnp.exp(dmat - m_tok)
            num = w_inter * _dot(qb, c_st.astype(BF16)) + _dot(qk.astype(BF16), vb)
            den = w_inter * _dot(qb, n_st.astype(BF16)) + jnp.sum(qk, axis=-1, keepdims=True)
            h = num / jnp.maximum(jnp.abs(den), jnp.exp(-m_tok))
            hn_ref[ts, hs] = _rms(h) * g_ref[hh] * _sigmoid(og_ref[ts, hs])
            b_last = jnp.sum(jnp.where(last_lane, b_row, 0.0), axis=-1, keepdims=True)
            g_row = b_last - b_row + ig_row
            m_new = jnp.maximum(b_last + m_st, jnp.max(g_row, axis=-1, keepdims=True))
            a_prev = jnp.exp(b_last + m_st - m_new)
            kw = k_t * jnp.exp(g_row - m_new)
            c_st = a_prev * c_st + _dot(kw.astype(BF16), vb)
            n_st = a_prev * n_st + jnp.sum(kw, axis=-1, keepdims=True)
            m_st = m_new
        c_ref[hh] = c_st
        n_ref[hh] = n_st[:, 0:1]
        m_ref[hh] = m_st[:, 0:1]


def _mlstm(proj, slot, gt4, c0, n0, m0, g_all, j, n_seq, s):
    h, nh = H_MLSTM, MLSTM_HEADS
    w = nh * HEAD_DIM
    col = lambda base: pl.BlockSpec((None, s, w), lambda b, c: (slot, b, base // nh + c))
    row = lambda base, hh: pl.BlockSpec((None, None, 1, s), lambda b, c: (b, base + c * nh + hh, 0, 0))
    st = lambda shape: pl.BlockSpec((None, nh) + shape, lambda b, c: (b, c, 0, 0))
    return pl.pallas_call(
        functools.partial(_mlstm_kernel, nh=nh),
        grid=(n_seq, h // nh),
        in_specs=[col(24), col(32), col(40), col(48)]
        + [row(8, hh) for hh in range(nh)] + [row(16, hh) for hh in range(nh)]
        + [st((HEAD_DIM, HEAD_DIM)), st((HEAD_DIM, 1)), st((1, 1)),
           pl.BlockSpec((None, nh, 1, HEAD_DIM), lambda b, c: (j, c, 0, 0))],
        out_specs=[pl.BlockSpec((s, w), lambda b, c: (b, c)),
                   st((HEAD_DIM, HEAD_DIM)), st((HEAD_DIM, 1)), st((1, 1))],
        out_shape=[jax.ShapeDtypeStruct((n_seq * s, h * HEAD_DIM), F32),
                   jax.ShapeDtypeStruct((n_seq, h, HEAD_DIM, HEAD_DIM), F32),
                   jax.ShapeDtypeStruct((n_seq, h, HEAD_DIM, 1), F32),
                   jax.ShapeDtypeStruct((n_seq, h, 1, 1), F32)],
        compiler_params=_params(("parallel", "parallel")),
        name="mlstm",
    )(proj, proj, proj, proj, *([gt4] * (2 * nh)), c0, n0, m0, g_all)


def _dil_kernel(q_ref, k_ref, v_ref, o_ref, m_ref, l_ref, acc_ref):
    s = q_ref.shape[0]
    blk = CHUNK
    row = lax.broadcasted_iota(jnp.int32, (blk, blk), 0)
    col = lax.broadcasted_iota(jnp.int32, (blk, blk), 1)
    cur_mask = col <= row
    dist = blk + lax.broadcasted_iota(jnp.int32, (blk, 2 * blk), 0) - lax.broadcasted_iota(jnp.int32, (blk, 2 * blk), 1)
    band_mask = jnp.logical_and(dist >= 0, dist <= blk)

    def rows(start, size, d):
        return pl.ds(start, size) if d == 1 else pl.ds(start, size, stride=d)

    for pi, (window, d) in enumerate(DIL_PATTERNS):
        assert window // d == blk and s % (d * blk) == 0
        for r in range(d):
            for n in range(s // (d * blk)):
                start = r + d * blk * n
                q = (q_ref[rows(start, blk, d), :] * QK_SCALE).astype(BF16)
                if n == 0:
                    kv_rows, mask = rows(start, blk, d), cur_mask
                else:
                    kv_rows, mask = rows(start - d * blk, 2 * blk, d), band_mask
                sc = _dot_nt(q, k_ref[kv_rows, :].astype(BF16))
                sc = jnp.where(mask, sc, -jnp.inf)
                m = jnp.max(sc, axis=-1, keepdims=True)
                p = jnp.exp(sc - m)
                out_rows = rows(pi * s + start, blk, d)
                m_ref[out_rows, :] = jnp.broadcast_to(m, (blk, LANES))
                l_ref[out_rows, :] = jnp.broadcast_to(jnp.sum(p, axis=-1, keepdims=True), (blk, LANES))
                acc_ref[out_rows, :] = _dot(p.astype(BF16), v_ref[kv_rows, :].astype(BF16))

    n_pat = len(DIL_PATTERNS)
    ms = [m_ref[pi * s:(pi + 1) * s, :] for pi in range(n_pat)]
    m_all = functools.reduce(jnp.maximum, ms)
    den = jnp.zeros((s, LANES), F32)
    num = jnp.zeros((s, HEAD_DIM), F32)
    for pi in range(n_pat):
        w = jnp.exp(ms[pi] - m_all)
        den = den + w * l_ref[pi * s:(pi + 1) * s, :]
        num = num + w * acc_ref[pi * s:(pi + 1) * s, :]
    o_ref[...] = num / den


def _dil_prompt(proj, slot, n_seq, s):
    h = H_DIL
    n_pat = len(DIL_PATTERNS)
    col = lambda base: pl.BlockSpec((None, s, HEAD_DIM), lambda b, c: (slot, b, base + c))
    return pl.pallas_call(
        _dil_kernel,
        grid=(n_seq, h),
        in_specs=[col(0), col(h), col(2 * h)],
        out_specs=pl.BlockSpec((s, HEAD_DIM), lambda b, c: (b, c)),
        out_shape=jax.ShapeDtypeStruct((n_seq * s, h * HEAD_DIM), F32),
        scratch_shapes=[pltpu.VMEM((n_pat * s, LANES), F32), pltpu.VMEM((n_pat * s, LANES), F32),
                        pltpu.VMEM((n_pat * s, HEAD_DIM), F32)],
        compiler_params=_params(("parallel", "parallel")),
        name="dil_prompt",
    )(proj, proj, proj)


def _dil_decode_kernel(q_ref, k_ref, v_ref, cnt_ref, kn_ref, vn_ref, cntn_ref, o_ref, m_ref, l_ref, acc_ref,
                       *, n_tiles):
    p = pl.program_id(1)

    @pl.when(p == 0)
    def _():
        m_ref[...] = jnp.full(m_ref.shape, -jnp.inf, F32)
        l_ref[...] = jnp.zeros(l_ref.shape, F32)
        acc_ref[...] = jnp.zeros(acc_ref.shape, F32)

    def process(k, v, cnt):
        q = (q_ref[...] * QK_SCALE).astype(BF16)
        s = jnp.where(cnt > 0.0, _dot_nt(q, k.astype(BF16)), -jnp.inf)
        m_old = m_ref[...]
        m_new = jnp.maximum(m_old, jnp.max(s, axis=-1, keepdims=True))
        m_safe = jnp.where(m_new == -jnp.inf, 0.0, m_new)
        alpha = jnp.exp(m_old - m_safe)
        pr = cnt * jnp.exp(s - m_safe)
        l_ref[...] = alpha * l_ref[...] + jnp.sum(pr, axis=-1, keepdims=True)
        acc_ref[...] = alpha * acc_ref[...] + _dot(pr.astype(BF16), v.astype(BF16))
        m_ref[...] = m_new

    @pl.when(p < n_tiles)
    def _():
        process(k_ref[...], v_ref[...], cnt_ref[...])

    @pl.when(p == n_tiles)
    def _():
        process(kn_ref[...], vn_ref[...], cntn_ref[...])
        o_ref[...] = acc_ref[...] / l_ref[...]


def _dil_counts(t, cb, n_keys, key0):
    r = jnp.arange(t * H_DIL, dtype=jnp.int32)[:, None]
    c = jnp.arange(n_keys * H_DIL, dtype=jnp.int32)[None, :]
    delta = cb + r // H_DIL - (key0 + c // H_DIL)
    cnt = jnp.zeros(delta.shape, F32)
    for window, d in DIL_PATTERNS:
        cnt = cnt + ((delta >= 0) & (delta % d == 0) & (delta <= window)).astype(F32)
    return jnp.where(r % H_DIL == c % H_DIL, cnt, 0.0)


def _dil_decode(q, k_cache, v_cache, j, k_new, v_new, t):
    db, rows, hd = q.shape
    cb = k_cache.shape[2] // H_DIL
    tk = DIL_TK
    n_tiles = cb // tk
    nrow = k_new.shape[1]
    cnt = _dil_counts(t, cb, cb, 0)
    cnt_new = _dil_counts(t, cb, nrow // H_DIL, cb)
    seq_map = lambda b, p: (b, 0, 0)
    tile_map = lambda b, p: (j, b, jnp.minimum(p, n_tiles - 1), 0)
    return pl.pallas_call(
        functools.partial(_dil_decode_kernel, n_tiles=n_tiles),
        grid=(db, n_tiles + 1),
        in_specs=[pl.BlockSpec((None, rows, hd), seq_map),
                  pl.BlockSpec((None, None, tk * H_DIL, hd), tile_map),
                  pl.BlockSpec((None, None, tk * H_DIL, hd), tile_map),
                  pl.BlockSpec((rows, tk * H_DIL), lambda b, p: (0, jnp.minimum(p, n_tiles - 1))),
                  pl.BlockSpec((None, nrow, hd), seq_map),
                  pl.BlockSpec((None, nrow, hd), seq_map),
                  pl.BlockSpec((rows, nrow), lambda b, p: (0, 0))],
        out_specs=pl.BlockSpec((None, rows, hd), seq_map),
        out_shape=jax.ShapeDtypeStruct((db, rows, hd), F32),
        scratch_shapes=[pltpu.VMEM((rows, 1), F32), pltpu.VMEM((rows, 1), F32),
                        pltpu.VMEM((rows, hd), F32)],
        compiler_params=_params(("parallel", "arbitrary")),
        name="dil_decode",
    )(q, k_cache, v_cache, cnt, k_new, v_new, cnt_new)


def _pad_rows(a, n_seq, rows, pad):
    w = a.shape[-1]
    return jnp.pad(a.reshape(n_seq, rows, w), ((0, 0), (0, pad - rows), (0, 0)))


def _head_rows(a, n_seq, t, n_heads):
    return a.reshape(n_seq, t * n_heads, HEAD_DIM)


def kernel(x_prompt, x_sample, c_prompt, c_sample, page_table, cache_fox_k, cache_fox_v, cache_fox_logf, state_mlstm_c, state_mlstm_n, state_mlstm_m, cache_win_k, cache_win_v, w_ada, b_ada, norm_pre, norm_mix, norm_post, norm_final, ffn_pre_w13, ffn_pre_w2, ffn_post_w13, ffn_post_w2, ab_w_in, ab_w_out, fox_fgate_b, mlstm_igate_b, mlstm_fgate_b, mlstm_norm_g, c_w_in, c_w_out):
    nb, s, d = x_prompt.shape
    db, t, _ = x_sample.shape
    depth = w_ada.shape[0]
    n_ab, n_c = ab_w_in.shape[0], c_w_in.shape[0]
    ms = db * t
    fw = H_FOX * HEAD_DIM
    mw = H_MLSTM * HEAD_DIM
    dw = H_DIL * HEAD_DIM
    assert s % TM == 0 and ms % SUBLANES == 0 and t <= NEW_KEYS

    xp = x_prompt.reshape(nb * s, d)
    xs = x_sample.reshape(ms, d)

    pad_rows = -(ms + nb) % SUBLANES
    c_rows = jnp.concatenate([jnp.repeat(c_sample, t, axis=0), c_prompt, jnp.zeros((pad_rows, d), F32)], axis=0)
    mods_all = _adaln_all(c_rows, w_ada, b_ada)
    mods_p = mods_all[:, ms:ms + nb].reshape(depth * nb * N_MOD, 1, d)
    mods_s = mods_all[:, :ms]

    norm_pre3, norm_mix3, norm_post3 = (a.reshape(depth, 1, d) for a in (norm_pre, norm_mix, norm_post))

    o1, o2, o3 = 3 * fw, 3 * fw + H_FOX, 3 * fw + H_FOX + 3 * mw
    o4 = o3 + 2 * H_MLSTM
    ab_main = jnp.concatenate([ab_w_in[:, :, :o1], ab_w_in[:, :, o2:o3], ab_w_in[:, :, o4:]], axis=2)
    ab_gate = jnp.concatenate([ab_w_in[:, :, o1:o2], ab_w_in[:, :, o3:o4],
                               jnp.zeros((n_ab, d, LANES - H_FOX - 2 * H_MLSTM), F32)], axis=2)
    gate_bias = jnp.concatenate([fox_fgate_b, mlstm_igate_b, mlstm_fgate_b, jnp.zeros((n_ab, 8), F32)],
                                axis=1).reshape(n_ab, 32, 1)
    head_gain = mlstm_norm_g.reshape(n_ab, H_MLSTM, 1, HEAD_DIM)

    n_pool, page = cache_fox_k.shape[1], cache_fox_k.shape[2]
    prow = page * H_FOX
    k_pool = cache_fox_k.reshape(n_ab, n_pool, prow, HEAD_DIM)
    v_pool = cache_fox_v.reshape(n_ab, n_pool, prow, HEAD_DIM)
    lf_t = cache_fox_logf.transpose(0, 1, 3, 2).reshape(n_ab * n_pool * H_FOX, page)
    cl_t = _page_cumsum(lf_t).reshape(n_ab, n_pool, H_FOX, page)
    cl = cl_t.transpose(0, 1, 3, 2)
    tot = jnp.broadcast_to(cl[:, :, page - 1:page, :], cl.shape)
    cs_pool = jnp.stack([cl.reshape(n_ab, n_pool, prow), tot.reshape(n_ab, n_pool, prow)], axis=2)

    cb = cache_win_k.shape[2]
    k_win = cache_win_k.reshape(n_c, db, cb * H_DIL, HEAD_DIM)
    v_win = cache_win_v.reshape(n_c, db, cb * H_DIL, HEAD_DIM)

    zeros_c = jnp.zeros((nb, H_MLSTM, HEAD_DIM, HEAD_DIM), F32)
    zeros_n = jnp.zeros((nb, H_MLSTM, HEAD_DIM, 1), F32)
    zeros_m = jnp.zeros((nb, H_MLSTM, 1, 1), F32)

    outs = {name: [] for name in ("fl_p", "fk_s", "fv_s", "fl_s", "mc_p", "mn_p", "mm_p",
                                  "mc_s", "mn_s", "mm_s", "wk_new", "wv_new")}
    pp_ab = pp_c = None

    for l in range(depth):
        j = l // 2
        last = l == depth - 1
        mp = _Mods(mods_p, l, True, TM, nb, s)
        md = _Mods(mods_s, l, False, ms)

        if l == 0:
            xp, xs, wk_s, wv_s = _ffn(xp, xs, mp, md, (0, 1, 2), norm_pre3, ffn_pre_w13, ffn_pre_w2, l,
                                      shift=((cache_win_k, t), (cache_win_v, t)))
        else:
            xp, xs = _ffn(xp, xs, mp, md, (0, 1, 2), norm_pre3, ffn_pre_w13, ffn_pre_w2, l)

        if l % 2 == 0:
            pp_ab, ps, gp, gs = _proj_in(xp, xs, mp, md, (3, 4), norm_mix3, l, ab_main, ab_gate, j, j, n_ab, pp_ab)
            gt = _gates(gp, gate_bias[j], nb, s, s)
            gt4 = gt.reshape(nb, 32, 1, s)
            o_fox = _fox_prompt(pp_ab, j, gt4, nb, s)
            hn, c_p, n_p, m_p = _mlstm(pp_ab, j, gt4, zeros_c, zeros_n, zeros_m, head_gain, j, nb, s)
            xp = _proj_out(o_fox, 0, hn, 0, ab_w_out, j, xp, mp, 5)
            outs["fl_p"].append(gt[:, 24:32, :].transpose(0, 2, 1))
            outs["mc_p"].append(c_p)
            outs["mn_p"].append(n_p.reshape(nb, H_MLSTM, HEAD_DIM))
            outs["mm_p"].append(m_p.reshape(nb, H_MLSTM))

            gts = _gates(_pad_rows(gs, db, t, CHUNK).reshape(db * CHUNK, LANES), gate_bias[j], db, CHUNK, t)
            gts4 = gts.reshape(db, 32, 1, CHUNK)
            nk = NEW_KEYS
            c_new = gts[:, 0:8, :nk].transpose(0, 2, 1).reshape(db, 1, nk * H_FOX)
            o_dec = _fox_decode(page_table, _head_rows(ps[:, :fw], db, t, H_FOX), k_pool, v_pool, cs_pool, j,
                                _pad_rows(_head_rows(ps[:, fw:2 * fw], db, t, H_FOX).reshape(db * t * H_FOX, HEAD_DIM),
                                          db, t * H_FOX, nk * H_FOX),
                                _pad_rows(_head_rows(ps[:, 2 * fw:3 * fw], db, t, H_FOX).reshape(db * t * H_FOX, HEAD_DIM),
                                          db, t * H_FOX, nk * H_FOX),
                                c_new)
            o_fox_s = o_dec.reshape(ms, fw)
            ps_pad = _pad_rows(ps, db, t, CHUNK).reshape(1, db * CHUNK, -1)
            hn_s, c_s, n_s, m_s = _mlstm(ps_pad, 0, gts4, state_mlstm_c[j],
                                         state_mlstm_n[j].reshape(db, H_MLSTM, HEAD_DIM, 1),
                                         state_mlstm_m[j].reshape(db, H_MLSTM, 1, 1),
                                         head_gain, j, db, CHUNK)
            hn_s = hn_s.reshape(db, CHUNK, mw)[:, :t].reshape(ms, mw)
            xs = _proj_out(o_fox_s, 0, hn_s, 0, ab_w_out, j, xs, md, 5)
            outs["fk_s"].append(ps[:, fw:2 * fw].reshape(db, t, H_FOX, HEAD_DIM))
            outs["fv_s"].append(ps[:, 2 * fw:3 * fw].reshape(db, t, H_FOX, HEAD_DIM))
            outs["fl_s"].append(gts[:, 24:32, :t].transpose(0, 2, 1))
            outs["mc_s"].append(c_s)
            outs["mn_s"].append(n_s.reshape(db, H_MLSTM, HEAD_DIM))
            outs["mm_s"].append(m_s.reshape(db, H_MLSTM))
        else:
            pp_c, ps = _proj_in(xp, xs, mp, md, (3, 4), norm_mix3, l, c_w_in, None, j, j, n_c, pp_c)
            o_dil = _dil_prompt(pp_c, j, nb, s)
            xp = _proj_out(o_dil, 0, o_dil, 1, c_w_out, j, xp, mp, 5)

            k_new = _head_rows(ps[:, dw:2 * dw], db, t, H_DIL)
            v_new = _head_rows(ps[:, 2 * dw:], db, t, H_DIL)
            nrow = NEW_KEYS * H_DIL
            o_dec = _dil_decode(_head_rows(ps[:, :dw], db, t, H_DIL), k_win, v_win, j,
                                jnp.pad(k_new, ((0, 0), (0, nrow - t * H_DIL), (0, 0))),
                                jnp.pad(v_new, ((0, 0), (0, nrow - t * H_DIL), (0, 0))), t)
            o_dil_s = o_dec.reshape(ms, dw)
            xs = _proj_out(o_dil_s, 0, o_dil_s, 1, c_w_out, j, xs, md, 5)
            outs["wk_new"].append(k_new.reshape(db, t, H_DIL, HEAD_DIM))
            outs["wv_new"].append(v_new.reshape(db, t, H_DIL, HEAD_DIM))

        fin = norm_final if last else None
        xp, xs = _ffn(xp, xs, mp, md, (6, 7, 8), norm_post3, ffn_post_w13, ffn_post_w2, l, fin)

    st = lambda name: jnp.stack(outs[name])
    assert cb == DIL_PATTERNS[-1][0]
    wk_s = lax.dynamic_update_slice(wk_s, st("wk_new"), (0, 0, cb - t, 0, 0))
    wv_s = lax.dynamic_update_slice(wv_s, st("wv_new"), (0, 0, cb - t, 0, 0))
    keep_p = min(DIL_PATTERNS[-1][0], s)
    heads = lambda a, n_heads: a.reshape(a.shape[0], nb, s, n_heads, HEAD_DIM)
    return (xp.reshape(nb, s, d), xs.reshape(db, t, d),
            heads(pp_ab[:, :, fw:2 * fw], H_FOX), heads(pp_ab[:, :, 2 * fw:3 * fw], H_FOX),
            st("fl_p"), st("fk_s"), st("fv_s"), st("fl_s"),
            st("mc_p"), st("mn_p"), st("mm_p"), st("mc_s"), st("mn_s"), st("mm_s"),
            heads(pp_c[:, :, dw:2 * dw], H_DIL)[:, :, s - keep_p:], heads(pp_c[:, :, 2 * dw:], H_DIL)[:, :, s - keep_p:],
            wk_s, wv_s)
```

```python
import functools

import jax
import jax.numpy as jnp
from jax import lax
from jax.experimental import pallas as pl
from jax.experimental.pallas import tpu as pltpu

F32 = jnp.float32
BF16 = jnp.bfloat16

HEAD_DIM = 128
H_FOX = 8
H_MLSTM = 8
H_DIL = 16
N_MOD = 9
EPS = 1e-6
FFN_RES = 0.5
QK_SCALE = HEAD_DIM ** -0.5
DIL_PATTERNS = ((128, 1), (512, 4), (2048, 16))
CHUNK = 128
NEG = -1e30

LANES = 128
SUBLANES = 8
VMEM_BIG = 60 * 1024 * 1024
VMEM_MID = 48 * 1024 * 1024

TM = 1024
TF = 256
TN = 512
TN_IN = 1024
ADALN_TN = 2304
TQ = 256
MLSTM_HEADS = 2
PAGES_PER_STEP = 8
DIL_TK = 512
NEW_KEYS = 16


def _params(sem, vmem=VMEM_MID):
    return pltpu.CompilerParams(dimension_semantics=sem, vmem_limit_bytes=vmem)


def _sigmoid(x):
    return 1.0 / (1.0 + jnp.exp(-x))


def _log_sigmoid(x):
    return jnp.minimum(x, 0.0) - jnp.log1p(jnp.exp(-jnp.abs(x)))


def _dot(a, b):
    return jnp.dot(a, b, preferred_element_type=F32)


def _dot_nt(a, b):
    return lax.dot_general(a, b, (((1,), (1,)), ((), ())), preferred_element_type=F32)


def _rms(x):
    return x * lax.rsqrt(jnp.mean(x * x, axis=-1, keepdims=True) + EPS)


def _norm_mod(x, g, shift, scale):
    return _rms(x) * g * (1.0 + scale) + shift


def _cumsum_lanes(x):
    lane = lax.broadcasted_iota(jnp.int32, x.shape, 1)
    sh = 1
    while sh < x.shape[1]:
        x = x + jnp.where(lane >= sh, pltpu.roll(x, sh, 1), 0.0)
        sh *= 2
    return x


class _Mods:
    def __init__(self, arr, layer, prompt, tm, n_seq=None, seq_rows=None):
        self.arr, self.layer, self.prompt, self.tm, self.n_seq, self.seq_rows = arr, layer, prompt, tm, n_seq, seq_rows

    def spec(self, k, tn=None):
        d = self.arr.shape[-1] if self.prompt else self.arr.shape[-1] // N_MOD
        l, tm = self.layer, self.tm
        if self.prompt:
            n_seq, seq_rows = self.n_seq, self.seq_rows
            row = lambda i: (l * n_seq + (i * tm) // seq_rows) * N_MOD + k
            if tn is None:
                return pl.BlockSpec((None, 1, d), lambda i, *_: (row(i), 0, 0))
            return pl.BlockSpec((None, 1, tn), lambda i, j: (row(i), 0, j))
        if tn is None:
            return pl.BlockSpec((None, tm, d), lambda i, *_: (l, 0, k))
        return pl.BlockSpec((None, tm, tn), lambda i, j: (l, 0, k * (d // tn) + j))


def _adaln_kernel(c_ref, w_ref, b_ref, o_ref):
    c = c_ref[...]
    s = (c * _sigmoid(c)).astype(BF16)
    o_ref[...] = _dot(s, w_ref[...].astype(BF16)) + b_ref[...]


def _adaln_all(c_rows, w_ada, b_ada):
    depth, d, n = w_ada.shape
    rows = c_rows.shape[0]
    tn = ADALN_TN
    assert n % tn == 0
    return pl.pallas_call(
        _adaln_kernel,
        grid=(depth, n // tn),
        in_specs=[pl.BlockSpec((rows, d), lambda l, j: (0, 0)),
                  pl.BlockSpec((None, d, tn), lambda l, j: (l, 0, j)),
                  pl.BlockSpec((None, 1, tn), lambda l, j: (l, 0, j))],
        out_specs=pl.BlockSpec((None, rows, tn), lambda l, j: (l, 0, j)),
        out_shape=jax.ShapeDtypeStruct((depth, rows, n), F32),
        compiler_params=_params(("parallel", "parallel")),
        name="adaln",
    )(c_rows, w_ada, b_ada.reshape(depth, 1, n))


def _ffn_kernel(x_ref, sh_ref, sc_ref, gt_ref, xs_ref, shs_ref, scs_ref, gts_ref, g_ref, w13_hbm, w2_hbm,
                *rest, layer, d_ff, nf, tf, rem, final, tm, ni):
    if final:
        gf_ref, o_ref, os_ref, h_ref, w1_buf, w3_buf, w2_buf, sem = rest
    else:
        o_ref, os_ref, h_ref, w1_buf, w3_buf, w2_buf, sem = rest
    i = pl.program_id(0)

    def tile_copies(f, slot, width):
        c0 = f * tf if isinstance(f, int) else pl.multiple_of(f * tf, tf)
        return (pltpu.make_async_copy(w13_hbm.at[layer, :, pl.ds(c0, width)],
                                      w1_buf.at[slot, :, pl.ds(0, width)], sem.at[slot, 0]),
                pltpu.make_async_copy(w13_hbm.at[layer, :, pl.ds(d_ff + c0, width)],
                                      w3_buf.at[slot, :, pl.ds(0, width)], sem.at[slot, 1]),
                pltpu.make_async_copy(w2_hbm.at[layer, pl.ds(c0, width), :],
                                      w2_buf.at[slot, pl.ds(0, width), :], sem.at[slot, 2]))

    def start(f, slot, width):
        for c in tile_copies(f, slot, width):
            c.start()

    def wait(f, slot, width):
        for c in tile_copies(f, slot, width):
            c.wait()

    def finish(x, gate, acc):
        y = x + (FFN_RES * gate) * acc
        return _rms(y) * gf_ref[...] if final else y

    def run(rows):
        with_sample = rows > tm

        def part(w1, w3, w2):
            h = h_ref[:rows, :]
            a = _dot(h, w1.astype(BF16))
            g = _dot(h, w3.astype(BF16))
            u = (a * _sigmoid(a) * g).astype(BF16)
            return _dot(u, w2.astype(BF16))

        start(0, 0, tf)
        h_ref[:tm, :] = _norm_mod(x_ref[...], g_ref[...], sh_ref[...], sc_ref[...]).astype(BF16)
        if with_sample:
            h_ref[tm:, :] = _norm_mod(xs_ref[...], g_ref[...], shs_ref[...], scs_ref[...]).astype(BF16)

        wait(0, 0, tf)
        start(1, 1, tf)
        p = part(w1_buf[0], w3_buf[0], w2_buf[0])
        o_ref[...] = p[:tm]
        if with_sample:
            os_ref[...] = p[tm:]

        def body(f, carry):
            slot = lax.rem(f, 2)
            wait(f, slot, tf)

            @pl.when(f + 1 < nf - 1)
            def _():
                start(f + 1, 1 - slot, tf)

            @pl.when(f + 1 == nf - 1)
            def _():
                start(nf - 1, 1 - slot, rem)

            p = part(w1_buf[slot], w3_buf[slot], w2_buf[slot])
            o_ref[...] += p[:tm]
            if with_sample:
                os_ref[...] += p[tm:]
            return carry

        lax.fori_loop(1, nf - 1, body, 0)

        slot = (nf - 1) % 2
        wait(nf - 1, slot, rem)
        p = part(w1_buf[slot, :, :rem], w3_buf[slot, :, :rem], w2_buf[slot, :rem, :])
        o_ref[...] = finish(x_ref[...], gt_ref[...], o_ref[...] + p[:tm])
        if with_sample:
            os_ref[...] = finish(xs_ref[...], gts_ref[...], os_ref[...] + p[tm:])

    @pl.when(i < ni - 1)
    def _():
        run(tm)

    @pl.when(i == ni - 1)
    def _():
        run(h_ref.shape[0])


def _ffn(x, xs, mods, mods_s, ks, g_all, w13_all, w2_all, layer, final_g=None):
    m, d = x.shape
    ms = xs.shape[0]
    tm = mods.tm
    d_ff = w2_all.shape[1]
    tf = TF
    nf = pl.cdiv(d_ff, tf)
    rem = d_ff - (nf - 1) * tf
    ni = m // tm
    assert nf >= 3 and m % tm == 0 and d_ff % LANES == 0 and ni >= 2 and mods_s.tm == ms
    final = final_g is not None
    in_specs = [pl.BlockSpec((tm, d), lambda i: (i, 0)),
                mods.spec(ks[0]), mods.spec(ks[1]), mods.spec(ks[2]),
                pl.BlockSpec((ms, d), lambda i: (0, 0)),
                mods_s.spec(ks[0]), mods_s.spec(ks[1]), mods_s.spec(ks[2]),
                pl.BlockSpec((None, 1, d), lambda i: (layer, 0, 0)),
                pl.BlockSpec(memory_space=pl.ANY),
                pl.BlockSpec(memory_space=pl.ANY)]
    args = [x, mods.arr, mods.arr, mods.arr, xs, mods_s.arr, mods_s.arr, mods_s.arr, g_all, w13_all, w2_all]
    if final:
        in_specs.append(pl.BlockSpec((1, d), lambda i: (0, 0)))
        args.append(final_g.reshape(1, d))
    return pl.pallas_call(
        functools.partial(_ffn_kernel, layer=layer, d_ff=d_ff, nf=nf, tf=tf, rem=rem, final=final, tm=tm, ni=ni),
        grid=(ni,),
        in_specs=in_specs,
        out_specs=[pl.BlockSpec((tm, d), lambda i: (i, 0)), pl.BlockSpec((ms, d), lambda i: (0, 0))],
        out_shape=[jax.ShapeDtypeStruct((m, d), F32), jax.ShapeDtypeStruct((ms, d), F32)],
        scratch_shapes=[pltpu.VMEM((tm + ms, d), BF16),
                        pltpu.VMEM((2, d, tf), F32), pltpu.VMEM((2, d, tf), F32), pltpu.VMEM((2, tf, d), F32),
                        pltpu.SemaphoreType.DMA((2, 3))],
        compiler_params=_params(("arbitrary",), VMEM_BIG),
        name="ffn",
    )(*args)


def _proj_in_kernel(x_ref, sh_ref, sc_ref, xs_ref, shs_ref, scs_ref, g_ref, w_ref, *rest, gates, aliased, tm, ni):
    rest = list(rest)
    wg_ref = rest.pop(0) if gates else None
    if aliased:
        rest.pop(0)
    if gates:
        o_ref, os_ref, og_ref, ogs_ref, h_ref = rest
    else:
        o_ref, os_ref, h_ref = rest
    i = pl.program_id(0)
    n = pl.program_id(1)

    @pl.when(n == 0)
    def _():
        h = _norm_mod(x_ref[...], g_ref[...], sh_ref[...], sc_ref[...]).astype(BF16)
        h_ref[:tm, :] = h
        if gates:
            og_ref[...] = _dot(h, wg_ref[...].astype(BF16))

    @pl.when(jnp.logical_and(n == 0, i == ni - 1))
    def _():
        hs = _norm_mod(xs_ref[...], g_ref[...], shs_ref[...], scs_ref[...]).astype(BF16)
        h_ref[tm:, :] = hs
        if gates:
            ogs_ref[...] = _dot(hs, wg_ref[...].astype(BF16))

    @pl.when(i < ni - 1)
    def _():
        o_ref[...] = _dot(h_ref[:tm, :], w_ref[...].astype(BF16))

    @pl.when(i == ni - 1)
    def _():
        y = _dot(h_ref[...], w_ref[...].astype(BF16))
        o_ref[...] = y[:tm]
        os_ref[...] = y[tm:]


def _proj_in(x, xs, mods, mods_s, ks, g_all, layer, w_all, wg_all, j, slot=0, n_slots=1, stacked=None):
    m, d = x.shape
    ms = xs.shape[0]
    tm = mods.tm
    n = w_all.shape[2]
    tn = TN_IN
    ni = m // tm
    assert n % tn == 0 and m % tm == 0 and (stacked is None) == (slot == 0) and ni >= 2 and mods_s.tm == ms
    gates = wg_all is not None
    aliased = stacked is not None
    s_col = lambda i, c: jnp.where(i == ni - 1, c, 0)
    in_specs = [pl.BlockSpec((tm, d), lambda i, c: (i, 0)), mods.spec(ks[0]), mods.spec(ks[1]),
                pl.BlockSpec((ms, d), lambda i, c: (0, 0)), mods_s.spec(ks[0]), mods_s.spec(ks[1]),
                pl.BlockSpec((None, 1, d), lambda i, c: (layer, 0, 0)),
                pl.BlockSpec((None, d, tn), lambda i, c: (j, 0, c))]
    out_specs = [pl.BlockSpec((None, tm, tn), lambda i, c: (slot, i, c)),
                 pl.BlockSpec((ms, tn), lambda i, c: (0, s_col(i, c)))]
    out_shape = [jax.ShapeDtypeStruct((n_slots, m, n), F32), jax.ShapeDtypeStruct((ms, n), F32)]
    args = [x, mods.arr, mods.arr, xs, mods_s.arr, mods_s.arr, g_all, w_all]
    if gates:
        in_specs.append(pl.BlockSpec((None, d, LANES), lambda i, c: (j, 0, 0)))
        out_specs += [pl.BlockSpec((tm, LANES), lambda i, c: (i, 0)), pl.BlockSpec((ms, LANES), lambda i, c: (0, 0))]
        out_shape += [jax.ShapeDtypeStruct((m, LANES), F32), jax.ShapeDtypeStruct((ms, LANES), F32)]
        args.append(wg_all)
    if aliased:
        in_specs.append(pl.BlockSpec(memory_space=pl.ANY))
        args.append(stacked)
    return pl.pallas_call(
        functools.partial(_proj_in_kernel, gates=gates, aliased=aliased, tm=tm, ni=ni),
        grid=(ni, n // tn),
        in_specs=in_specs, out_specs=out_specs, out_shape=out_shape,
        input_output_aliases={len(args) - 1: 0} if aliased else {},
        scratch_shapes=[pltpu.VMEM((tm + ms, d), BF16)],
        compiler_params=_params(("arbitrary", "arbitrary"), VMEM_BIG),
        name="proj_in",
    )(*args)


def _proj_out_kernel(a_ref, b_ref, wa_ref, wb_ref, x_ref, gt_ref, o_ref, ab_ref, bb_ref):
    n = pl.program_id(1)

    @pl.when(n == 0)
    def _():
        ab_ref[...] = a_ref[...].astype(BF16)
        bb_ref[...] = b_ref[...].astype(BF16)

    y = _dot(ab_ref[...], wa_ref[...].astype(BF16)) + _dot(bb_ref[...], wb_ref[...].astype(BF16))
    o_ref[...] = x_ref[...] + gt_ref[...] * y


def _proj_out(mix_a, a_blk, mix_b, b_blk, w_all, j, x, mods, k):
    m, d = x.shape
    tm = mods.tm
    kh = w_all.shape[1] // 2
    tn = TN
    return pl.pallas_call(
        _proj_out_kernel,
        grid=(m // tm, d // tn),
        in_specs=[pl.BlockSpec((tm, kh), lambda i, c: (i, a_blk)),
                  pl.BlockSpec((tm, kh), lambda i, c: (i, b_blk)),
                  pl.BlockSpec((None, kh, tn), lambda i, c: (j, 0, c)),
                  pl.BlockSpec((None, kh, tn), lambda i, c: (j, 1, c)),
                  pl.BlockSpec((tm, tn), lambda i, c: (i, c)),
                  mods.spec(k, tn)],
        out_specs=pl.BlockSpec((tm, tn), lambda i, c: (i, c)),
        out_shape=jax.ShapeDtypeStruct((m, d), F32),
        scratch_shapes=[pltpu.VMEM((tm, kh), BF16), pltpu.VMEM((tm, kh), BF16)],
        compiler_params=_params(("parallel", "arbitrary")),
        name="proj_out",
    )(mix_a, mix_b, w_all, w_all, x, mods.arr)


def _gates_kernel(p_ref, b_ref, o_ref, *, n_valid):
    s = p_ref.shape[0]
    lane = lax.broadcasted_iota(jnp.int32, (8, LANES), 1)
    carry = jnp.zeros((8, 1), F32)
    for blk in range(s // LANES):
        sl = slice(blk * LANES, (blk + 1) * LANES)
        pre = p_ref[sl, :].T[0:32] + b_ref[...]
        valid = (lane + blk * LANES) < n_valid
        lf_f = jnp.where(valid, _log_sigmoid(pre[0:8]), 0.0)
        ig = jnp.where(valid, pre[8:16], NEG)
        lf_m = jnp.where(valid, _log_sigmoid(pre[16:24]), 0.0)
        c_f = _cumsum_lanes(lf_f) + carry
        carry = c_f[:, LANES - 1:LANES]
        o_ref[0:8, sl] = c_f
        o_ref[8:16, sl] = ig
        o_ref[16:24, sl] = _cumsum_lanes(lf_m)
        o_ref[24:32, sl] = lf_f


def _gates(pre, bias, n_seq, s, n_valid):
    return pl.pallas_call(
        functools.partial(_gates_kernel, n_valid=n_valid),
        grid=(n_seq,),
        in_specs=[pl.BlockSpec((s, LANES), lambda b: (b, 0)),
                  pl.BlockSpec((32, 1), lambda b: (0, 0))],
        out_specs=pl.BlockSpec((None, 32, s), lambda b: (b, 0, 0)),
        out_shape=jax.ShapeDtypeStruct((n_seq, 32, s), F32),
        compiler_params=_params(("parallel",)),
        name="gates",
    )(pre, bias)


def _fox_kernel(q_ref, k_ref, v_ref, c_ref, o_ref, kb_ref, vb_ref, *, tq):
    s = q_ref.shape[0]
    kb_ref[...] = k_ref[...].astype(BF16)
    vb_ref[...] = v_ref[...].astype(BF16)
    row = lax.broadcasted_iota(jnp.int32, (tq, tq), 0)
    col = lax.broadcasted_iota(jnp.int32, (tq, tq), 1)
    for qi in range(s // tq):
        q = (q_ref[qi * tq:(qi + 1) * tq, :] * QK_SCALE).astype(BF16)
        m = l = acc = None
        for kb in range(qi + 1):
            ks = slice(kb * tq, (kb + 1) * tq)
            sc = _dot_nt(q, kb_ref[ks, :]) - c_ref[:, ks]
            if kb == qi:
                sc = jnp.where(col <= row, sc, -jnp.inf)
            bm = jnp.max(sc, axis=-1, keepdims=True)
            if m is None:
                m = bm
                p = jnp.exp(sc - m)
                l = jnp.sum(p, axis=-1, keepdims=True)
                acc = _dot(p.astype(BF16), vb_ref[ks, :])
            else:
                m_new = jnp.maximum(m, bm)
                alpha = jnp.exp(m - m_new)
                p = jnp.exp(sc - m_new)
                l = alpha * l + jnp.sum(p, axis=-1, keepdims=True)
                acc = alpha * acc + _dot(p.astype(BF16), vb_ref[ks, :])
                m = m_new
        o_ref[qi * tq:(qi + 1) * tq, :] = acc / l


def _fox_prompt(proj, slot, gt4, n_seq, s):
    col = lambda base: pl.BlockSpec((None, s, HEAD_DIM), lambda b, h: (slot, b, base + h))
    return pl.pallas_call(
        functools.partial(_fox_kernel, tq=TQ),
        grid=(n_seq, H_FOX),
        in_specs=[col(0), col(H_FOX), col(2 * H_FOX),
                  pl.BlockSpec((None, None, 1, s), lambda b, h: (b, h, 0, 0))],
        out_specs=pl.BlockSpec((s, HEAD_DIM), lambda b, h: (b, h)),
        out_shape=jax.ShapeDtypeStruct((n_seq * s, H_FOX * HEAD_DIM), F32),
        scratch_shapes=[pltpu.VMEM((s, HEAD_DIM), BF16), pltpu.VMEM((s, HEAD_DIM), BF16)],
        compiler_params=_params(("parallel", "parallel")),
        name="fox_prompt",
    )(proj, proj, proj, gt4)


def _page_cumsum_kernel(x_ref, o_ref):
    o_ref[...] = _cumsum_lanes(x_ref[...])


def _page_cumsum(lf_t):
    rows, page = lf_t.shape
    tr = 1024
    assert rows % tr == 0 and page == LANES
    return pl.pallas_call(
        _page_cumsum_kernel,
        grid=(rows // tr,),
        in_specs=[pl.BlockSpec((tr, page), lambda i: (i, 0))],
        out_specs=pl.BlockSpec((tr, page), lambda i: (i, 0)),
        out_shape=jax.ShapeDtypeStruct((rows, page), F32),
        compiler_params=_params(("parallel",)),
        name="page_cumsum",
    )(lf_t)


def _fox_decode_kernel(pt_ref, q_ref, *refs, g, n_steps):
    del pt_ref
    k_refs, v_refs, cs_refs = refs[:g], refs[g:2 * g], refs[2 * g:3 * g]
    kn_ref, vn_ref, cn_ref, o_ref, m_ref, l_ref, acc_ref, carry_ref = refs[3 * g:]
    p = pl.program_id(1)
    rows = q_ref.shape[0]
    hbits = H_FOX.bit_length() - 1

    @pl.when(p == 0)
    def _():
        m_ref[...] = jnp.full(m_ref.shape, -jnp.inf, F32)
        l_ref[...] = jnp.zeros(l_ref.shape, F32)
        acc_ref[...] = jnp.zeros(acc_ref.shape, F32)
        carry_ref[...] = jnp.zeros(carry_ref.shape, F32)

    def iotas(ncols):
        return (lax.broadcasted_iota(jnp.int32, (rows, ncols), 0),
                lax.broadcasted_iota(jnp.int32, (rows, ncols), 1))

    def attend(blocks, mask):
        q = (q_ref[...] * QK_SCALE).astype(BF16)
        scores = [jnp.where(mask, _dot_nt(q, k.astype(BF16)) - bias, -jnp.inf) for k, _, bias in blocks]
        m_old = m_ref[...]
        m_new = m_old
        for sc in scores:
            m_new = jnp.maximum(m_new, jnp.max(sc, axis=-1, keepdims=True))
        alpha = jnp.exp(m_old - m_new)
        l = alpha * l_ref[...]
        acc = alpha * acc_ref[...]
        for sc, (_, v, _) in zip(scores, blocks):
            pr = jnp.exp(sc - m_new)
            l = l + jnp.sum(pr, axis=-1, keepdims=True)
            acc = acc + _dot(pr.astype(BF16), v.astype(BF16))
        m_ref[...] = m_new
        l_ref[...] = l
        acc_ref[...] = acc

    @pl.when(p < n_steps)
    def _():
        r, c = iotas(k_refs[0].shape[0])
        mask = (r & (H_FOX - 1)) == (c & (H_FOX - 1))
        carry = carry_ref[...]
        blocks = []
        for i in range(g):
            cs = cs_refs[i][...]
            blocks.append((k_refs[i][...], v_refs[i][...], carry + cs[0:1]))
            carry = carry + cs[1:2]
        carry_ref[...] = carry
        attend(blocks, mask)

    @pl.when(p == n_steps)
    def _():
        ncol = kn_ref.shape[0]
        r, c = iotas(ncol)
        same_head = (r & (H_FOX - 1)) == (c & (H_FOX - 1))
        causal = lax.shift_right_logical(c, hbits) <= lax.shift_right_logical(r, hbits)
        bias = carry_ref[:, :ncol] + cn_ref[...]
        attend([(kn_ref[...], vn_ref[...], bias)], jnp.logical_and(same_head, causal))
        o_ref[...] = acc_ref[...] / l_ref[...]


def _fox_decode(page_table, q, k_pool, v_pool, cs_pool, j, k_new, v_new, c_new):
    db, rows, hd = q.shape
    n_pages = page_table.shape[1]
    prow = k_pool.shape[2]
    g = PAGES_PER_STEP
    assert n_pages % g == 0
    n_steps = n_pages // g
    nrow = k_new.shape[1]

    def pool_map(i):
        return lambda b, p, pt: (j, pt[b, jnp.minimum(p * g + i, n_pages - 1)], 0, 0)

    seq_map = lambda b, p, pt: (b, 0, 0)
    in_specs = [pl.BlockSpec((None, rows, hd), seq_map)]
    in_specs += [pl.BlockSpec((None, None, prow, hd), pool_map(i)) for i in range(g)]
    in_specs += [pl.BlockSpec((None, None, prow, hd), pool_map(i)) for i in range(g)]
    in_specs += [pl.BlockSpec((None, None, 2, prow), pool_map(i)) for i in range(g)]
    in_specs += [pl.BlockSpec((None, nrow, hd), seq_map), pl.BlockSpec((None, nrow, hd), seq_map),
                 pl.BlockSpec((None, 1, nrow), seq_map)]
    grid_spec = pltpu.PrefetchScalarGridSpec(
        num_scalar_prefetch=1,
        grid=(db, n_steps + 1),
        in_specs=in_specs,
        out_specs=pl.BlockSpec((None, rows, hd), seq_map),
        scratch_shapes=[pltpu.VMEM((rows, 1), F32), pltpu.VMEM((rows, 1), F32),
                        pltpu.VMEM((rows, hd), F32), pltpu.VMEM((1, prow), F32)])
    return pl.pallas_call(
        functools.partial(_fox_decode_kernel, g=g, n_steps=n_steps),
        grid_spec=grid_spec,
        out_shape=jax.ShapeDtypeStruct((db, rows, hd), F32),
        compiler_params=_params(("parallel", "arbitrary")),
        name="fox_decode",
    )(page_table, q, *([k_pool] * g), *([v_pool] * g), *([cs_pool] * g), k_new, v_new, c_new)


def _mlstm_kernel(q_ref, k_ref, v_ref, og_ref, *refs, nh):
    ig_refs, b_refs = refs[:nh], refs[nh:2 * nh]
    c0_ref, n0_ref, m0_ref, g_ref, hn_ref, c_ref, n_ref, m_ref = refs[2 * nh:]
    s = q_ref.shape[0]
    ln = CHUNK
    row = lax.broadcasted_iota(jnp.int32, (ln, ln), 0)
    col = lax.broadcasted_iota(jnp.int32, (ln, ln), 1)
    causal = col <= row

    last_lane = lax.broadcasted_iota(jnp.int32, (1, ln), 1) == ln - 1
    assert ln == HEAD_DIM == LANES
    for hh in range(nh):
        hs = slice(hh * HEAD_DIM, (hh + 1) * HEAD_DIM)
        c_st = c0_ref[hh]
        n_st = jnp.broadcast_to(n0_ref[hh], (HEAD_DIM, LANES))
        m_st = jnp.broadcast_to(m0_ref[hh], (1, LANES))
        for ci in range(s // ln):
            ts = slice(ci * ln, (ci + 1) * ln)
            qb = q_ref[ts, hs].astype(BF16)
            k_t = (k_ref[ts, hs] * QK_SCALE).T
            vb = v_ref[ts, hs].astype(BF16)
            b_row = b_refs[hh][:, ts]
            ig_row = ig_refs[hh][:, ts]
            b_mat = jnp.broadcast_to(b_row, (ln, ln))
            b_col = b_mat.T
            dmat = jnp.where(causal, b_col - b_mat + ig_row, NEG)
            inter = b_col + m_st
            m_tok = jnp.maximum(inter, jnp.max(dmat, axis=-1, keepdims=True))
            w_inter = jnp.exp(inter - m_tok)
            qk = _dot(qb, k_t.astype(BF16)) * jnp.exp(dmat - m_tok)
            num = w_inter * _dot(qb, c_st.astype(BF16)) + _dot(qk.astype(BF16), vb)
            den = w_inter * _dot(qb, n_st.astype(BF16)) + jnp.sum(qk, axis=-1, keepdims=True)
            h = num / jnp.maximum(jnp.abs(den), jnp.exp(-m_tok))
            hn_ref[ts, hs] = _rms(h) * g_ref[hh] * _sigmoid(og_ref[ts, hs])
            b_last = jnp.sum(jnp.where(last_lane, b_row, 0.0), axis=-1, keepdims=True)
            g_row = b_last - b_row + ig_row
            m_new = jnp.maximum(b_last + m_st, jnp.max(g_row, axis=-1, keepdims=True))
            a_prev = jnp.exp(b_last + m_st - m_new)
            kw = k_t * jnp.exp(g_row - m_new)
            c_st = a_prev * c_st + _dot(kw.astype(BF16), vb)
            n_st = a_prev * n_st + jnp.sum(kw, axis=-1, keepdims=True)
            m_st = m_new
        c_ref[hh] = c_st
        n_ref[hh] = n_st[:, 0:1]
        m_ref[hh] = m_st[:, 0:1]


def _mlstm(proj, slot, gt4, c0, n0, m0, g_all, j, n_seq, s):
    h, nh = H_MLSTM, MLSTM_HEADS
    w = nh * HEAD_DIM
    col = lambda base: pl.BlockSpec((None, s, w), lambda b, c: (slot, b, base // nh + c))
    row = lambda base, hh: pl.BlockSpec((None, None, 1, s), lambda b, c: (b, base + c * nh + hh, 0, 0))
    st = lambda shape: pl.BlockSpec((None, nh) + shape, lambda b, c: (b, c, 0, 0))
    return pl.pallas_call(
        functools.partial(_mlstm_kernel, nh=nh),
        grid=(n_seq, h // nh),
        in_specs=[col(24), col(32), col(40), col(48)]
        + [row(8, hh) for hh in range(nh)] + [row(16, hh) for hh in range(nh)]
        + [st((HEAD_DIM, HEAD_DIM)), st((HEAD_DIM, 1)), st((1, 1)),
           pl.BlockSpec((None, nh, 1, HEAD_DIM), lambda b, c: (j, c, 0, 0))],
        out_specs=[pl.BlockSpec((s, w), lambda b, c: (b, c)),
                   st((HEAD_DIM, HEAD_DIM)), st((HEAD_DIM, 1)), st((1, 1))],
        out_shape=[jax.ShapeDtypeStruct((n_seq * s, h * HEAD_DIM), F32),
                   jax.ShapeDtypeStruct((n_seq, h, HEAD_DIM, HEAD_DIM), F32),
                   jax.ShapeDtypeStruct((n_seq, h, HEAD_DIM, 1), F32),
                   jax.ShapeDtypeStruct((n_seq, h, 1, 1), F32)],
        compiler_params=_params(("parallel", "parallel")),
        name="mlstm",
    )(proj, proj, proj, proj, *([gt4] * (2 * nh)), c0, n0, m0, g_all)


def _dil_kernel(q_ref, k_ref, v_ref, o_ref, m_ref, l_ref, acc_ref):
    s = q_ref.shape[0]
    blk = CHUNK
    row = lax.broadcasted_iota(jnp.int32, (blk, blk), 0)
    col = lax.broadcasted_iota(jnp.int32, (blk, blk), 1)
    cur_mask = col <= row
    dist = blk + lax.broadcasted_iota(jnp.int32, (blk, 2 * blk), 0) - lax.broadcasted_iota(jnp.int32, (blk, 2 * blk), 1)
    band_mask = jnp.logical_and(dist >= 0, dist <= blk)

    def rows(start, size, d):
        return pl.ds(start, size) if d == 1 else pl.ds(start, size, stride=d)

    for pi, (window, d) in enumerate(DIL_PATTERNS):
        assert window // d == blk and s % (d * blk) == 0
        for r in range(d):
            for n in range(s // (d * blk)):
                start = r + d * blk * n
                q = (q_ref[rows(start, blk, d), :] * QK_SCALE).astype(BF16)
                if n == 0:
                    kv_rows, mask = rows(start, blk, d), cur_mask
                else:
                    kv_rows, mask = rows(start - d * blk, 2 * blk, d), band_mask
                sc = _dot_nt(q, k_ref[kv_rows, :].astype(BF16))
                sc = jnp.where(mask, sc, -jnp.inf)
                m = jnp.max(sc, axis=-1, keepdims=True)
                p = jnp.exp(sc - m)
                out_rows = rows(pi * s + start, blk, d)
                m_ref[out_rows, :] = jnp.broadcast_to(m, (blk, LANES))
                l_ref[out_rows, :] = jnp.broadcast_to(jnp.sum(p, axis=-1, keepdims=True), (blk, LANES))
                acc_ref[out_rows, :] = _dot(p.astype(BF16), v_ref[kv_rows, :].astype(BF16))

    n_pat = len(DIL_PATTERNS)
    ms = [m_ref[pi * s:(pi + 1) * s, :] for pi in range(n_pat)]
    m_all = functools.reduce(jnp.maximum, ms)
    den = jnp.zeros((s, LANES), F32)
    num = jnp.zeros((s, HEAD_DIM), F32)
    for pi in range(n_pat):
        w = jnp.exp(ms[pi] - m_all)
        den = den + w * l_ref[pi * s:(pi + 1) * s, :]
        num = num + w * acc_ref[pi * s:(pi + 1) * s, :]
    o_ref[...] = num / den


def _dil_prompt(proj, slot, n_seq, s):
    h = H_DIL
    n_pat = len(DIL_PATTERNS)
    col = lambda base: pl.BlockSpec((None, s, HEAD_DIM), lambda b, c: (slot, b, base + c))
    return pl.pallas_call(
        _dil_kernel,
        grid=(n_seq, h),
        in_specs=[col(0), col(h), col(2 * h)],
        out_specs=pl.BlockSpec((s, HEAD_DIM), lambda b, c: (b, c)),
        out_shape=jax.ShapeDtypeStruct((n_seq * s, h * HEAD_DIM), F32),
        scratch_shapes=[pltpu.VMEM((n_pat * s, LANES), F32), pltpu.VMEM((n_pat * s, LANES), F32),
                        pltpu.VMEM((n_pat * s, HEAD_DIM), F32)],
        compiler_params=_params(("parallel", "parallel")),
        name="dil_prompt",
    )(proj, proj, proj)


def _dil_decode_kernel(q_ref, k_ref, v_ref, cnt_ref, kn_ref, vn_ref, cntn_ref, o_ref, m_ref, l_ref, acc_ref,
                       *, n_tiles):
    p = pl.program_id(1)

    @pl.when(p == 0)
    def _():
        m_ref[...] = jnp.full(m_ref.shape, -jnp.inf, F32)
        l_ref[...] = jnp.zeros(l_ref.shape, F32)
        acc_ref[...] = jnp.zeros(acc_ref.shape, F32)

    def process(k, v, cnt):
        q = (q_ref[...] * QK_SCALE).astype(BF16)
        s = jnp.where(cnt > 0.0, _dot_nt(q, k.astype(BF16)), -jnp.inf)
        m_old = m_ref[...]
        m_new = jnp.maximum(m_old, jnp.max(s, axis=-1, keepdims=True))
        m_safe = jnp.where(m_new == -jnp.inf, 0.0, m_new)
        alpha = jnp.exp(m_old - m_safe)
        pr = cnt * jnp.exp(s - m_safe)
        l_ref[...] = alpha * l_ref[...] + jnp.sum(pr, axis=-1, keepdims=True)
        acc_ref[...] = alpha * acc_ref[...] + _dot(pr.astype(BF16), v.astype(BF16))
        m_ref[...] = m_new

    @pl.when(p < n_tiles)
    def _():
        process(k_ref[...], v_ref[...], cnt_ref[...])

    @pl.when(p == n_tiles)
    def _():
        process(kn_ref[...], vn_ref[...], cntn_ref[...])
        o_ref[...] = acc_ref[...] / l_ref[...]


def _dil_counts(t, cb, n_keys, key0):
    r = jnp.arange(t * H_DIL, dtype=jnp.int32)[:, None]
    c = jnp.arange(n_keys * H_DIL, dtype=jnp.int32)[None, :]
    delta = cb + r // H_DIL - (key0 + c // H_DIL)
    cnt = jnp.zeros(delta.shape, F32)
    for window, d in DIL_PATTERNS:
        cnt = cnt + ((delta >= 0) & (delta % d == 0) & (delta <= window)).astype(F32)
    return jnp.where(r % H_DIL == c % H_DIL, cnt, 0.0)


def _dil_decode(q, k_cache, v_cache, j, k_new, v_new, t):
    db, rows, hd = q.shape
    cb = k_cache.shape[2] // H_DIL
    tk = DIL_TK
    n_tiles = cb // tk
    nrow = k_new.shape[1]
    cnt = _dil_counts(t, cb, cb, 0)
    cnt_new = _dil_counts(t, cb, nrow // H_DIL, cb)
    seq_map = lambda b, p: (b, 0, 0)
    tile_map = lambda b, p: (j, b, jnp.minimum(p, n_tiles - 1), 0)
    return pl.pallas_call(
        functools.partial(_dil_decode_kernel, n_tiles=n_tiles),
        grid=(db, n_tiles + 1),
        in_specs=[pl.BlockSpec((None, rows, hd), seq_map),
                  pl.BlockSpec((None, None, tk * H_DIL, hd), tile_map),
                  pl.BlockSpec((None, None, tk * H_DIL, hd), tile_map),
                  pl.BlockSpec((rows, tk * H_DIL), lambda b, p: (0, jnp.minimum(p, n_tiles - 1))),
                  pl.BlockSpec((None, nrow, hd), seq_map),
                  pl.BlockSpec((None, nrow, hd), seq_map),
                  pl.BlockSpec((rows, nrow), lambda b, p: (0, 0))],
        out_specs=pl.BlockSpec((None, rows, hd), seq_map),
        out_shape=jax.ShapeDtypeStruct((db, rows, hd), F32),
        scratch_shapes=[pltpu.VMEM((rows, 1), F32), pltpu.VMEM((rows, 1), F32),
                        pltpu.VMEM((rows, hd), F32)],
        compiler_params=_params(("parallel", "arbitrary")),
        name="dil_decode",
    )(q, k_cache, v_cache, cnt, k_new, v_new, cnt_new)


def _pad_rows(a, n_seq, rows, pad):
    w = a.shape[-1]
    return jnp.pad(a.reshape(n_seq, rows, w), ((0, 0), (0, pad - rows), (0, 0)))


def _head_rows(a, n_seq, t, n_heads):
    return a.reshape(n_seq, t * n_heads, HEAD_DIM)


def kernel(x_prompt, x_sample, c_prompt, c_sample, page_table, cache_fox_k, cache_fox_v, cache_fox_logf, state_mlstm_c, state_mlstm_n, state_mlstm_m, cache_win_k, cache_win_v, w_ada, b_ada, norm_pre, norm_mix, norm_post, norm_final, ffn_pre_w13, ffn_pre_w2, ffn_post_w13, ffn_post_w2, ab_w_in, ab_w_out, fox_fgate_b, mlstm_igate_b, mlstm_fgate_b, mlstm_norm_g, c_w_in, c_w_out):
    nb, s, d = x_prompt.shape
    db, t, _ = x_sample.shape
    depth = w_ada.shape[0]
    n_ab, n_c = ab_w_in.shape[0], c_w_in.shape[0]
    ms = db * t
    fw = H_FOX * HEAD_DIM
    mw = H_MLSTM * HEAD_DIM
    dw = H_DIL * HEAD_DIM
    assert s % TM == 0 and ms % SUBLANES == 0 and t <= NEW_KEYS

    xp = x_prompt.reshape(nb * s, d)
    xs = x_sample.reshape(ms, d)

    pad_rows = -(ms + nb) % SUBLANES
    c_rows = jnp.concatenate([jnp.repeat(c_sample, t, axis=0), c_prompt, jnp.zeros((pad_rows, d), F32)], axis=0)
    mods_all = _adaln_all(c_rows, w_ada, b_ada)
    mods_p = mods_all[:, ms:ms + nb].reshape(depth * nb * N_MOD, 1, d)
    mods_s = mods_all[:, :ms]

    norm_pre3, norm_mix3, norm_post3 = (a.reshape(depth, 1, d) for a in (norm_pre, norm_mix, norm_post))

    o1, o2, o3 = 3 * fw, 3 * fw + H_FOX, 3 * fw + H_FOX + 3 * mw
    o4 = o3 + 2 * H_MLSTM
    ab_main = jnp.concatenate([ab_w_in[:, :, :o1], ab_w_in[:, :, o2:o3], ab_w_in[:, :, o4:]], axis=2)
    ab_gate = jnp.concatenate([ab_w_in[:, :, o1:o2], ab_w_in[:, :, o3:o4],
                               jnp.zeros((n_ab, d, LANES - H_FOX - 2 * H_MLSTM), F32)], axis=2)
    gate_bias = jnp.concatenate([fox_fgate_b, mlstm_igate_b, mlstm_fgate_b, jnp.zeros((n_ab, 8), F32)],
                                axis=1).reshape(n_ab, 32, 1)
    head_gain = mlstm_norm_g.reshape(n_ab, H_MLSTM, 1, HEAD_DIM)

    n_pool, page = cache_fox_k.shape[1], cache_fox_k.shape[2]
    prow = page * H_FOX
    k_pool = cache_fox_k.reshape(n_ab, n_pool, prow, HEAD_DIM)
    v_pool = cache_fox_v.reshape(n_ab, n_pool, prow, HEAD_DIM)
    lf_t = cache_fox_logf.transpose(0, 1, 3, 2).reshape(n_ab * n_pool * H_FOX, page)
    cl_t = _page_cumsum(lf_t).reshape(n_ab, n_pool, H_FOX, page)
    cl = cl_t.transpose(0, 1, 3, 2)
    tot = jnp.broadcast_to(cl[:, :, page - 1:page, :], cl.shape)
    cs_pool = jnp.stack([cl.reshape(n_ab, n_pool, prow), tot.reshape(n_ab, n_pool, prow)], axis=2)

    cb = cache_win_k.shape[2]
    k_win = cache_win_k.reshape(n_c, db, cb * H_DIL, HEAD_DIM)
    v_win = cache_win_v.reshape(n_c, db, cb * H_DIL, HEAD_DIM)

    zeros_c = jnp.zeros((nb, H_MLSTM, HEAD_DIM, HEAD_DIM), F32)
    zeros_n = jnp.zeros((nb, H_MLSTM, HEAD_DIM, 1), F32)
    zeros_m = jnp.zeros((nb, H_MLSTM, 1, 1), F32)

    outs = {name: [] for name in ("fl_p", "fk_s", "fv_s", "fl_s", "mc_p", "mn_p", "mm_p",
                                  "mc_s", "mn_s", "mm_s", "wk_new", "wv_new")}
    pp_ab = pp_c = None

    for l in range(depth):
        j = l // 2
        last = l == depth - 1
        mp = _Mods(mods_p, l, True, TM, nb, s)
        md = _Mods(mods_s, l, False, ms)

        xp, xs = _ffn(xp, xs, mp, md, (0, 1, 2), norm_pre3, ffn_pre_w13, ffn_pre_w2, l)

        if l % 2 == 0:
            pp_ab, ps, gp, gs = _proj_in(xp, xs, mp, md, (3, 4), norm_mix3, l, ab_main, ab_gate, j, j, n_ab, pp_ab)
            gt = _gates(gp, gate_bias[j], nb, s, s)
            gt4 = gt.reshape(nb, 32, 1, s)
            o_fox = _fox_prompt(pp_ab, j, gt4, nb, s)
            hn, c_p, n_p, m_p = _mlstm(pp_ab, j, gt4, zeros_c, zeros_n, zeros_m, head_gain, j, nb, s)
            xp = _proj_out(o_fox, 0, hn, 0, ab_w_out, j, xp, mp, 5)
            outs["fl_p"].append(gt[:, 24:32, :].transpose(0, 2, 1))
            outs["mc_p"].append(c_p)
            outs["mn_p"].append(n_p.reshape(nb, H_MLSTM, HEAD_DIM))
            outs["mm_p"].append(m_p.reshape(nb, H_MLSTM))

            gts = _gates(_pad_rows(gs, db, t, CHUNK).reshape(db * CHUNK, LANES), gate_bias[j], db, CHUNK, t)
            gts4 = gts.reshape(db, 32, 1, CHUNK)
            nk = NEW_KEYS
            c_new = gts[:, 0:8, :nk].transpose(0, 2, 1).reshape(db, 1, nk * H_FOX)
            o_dec = _fox_decode(page_table, _head_rows(ps[:, :fw], db, t, H_FOX), k_pool, v_pool, cs_pool, j,
                                _pad_rows(_head_rows(ps[:, fw:2 * fw], db, t, H_FOX).reshape(db * t * H_FOX, HEAD_DIM),
                                          db, t * H_FOX, nk * H_FOX),
                                _pad_rows(_head_rows(ps[:, 2 * fw:3 * fw], db, t, H_FOX).reshape(db * t * H_FOX, HEAD_DIM),
                                          db, t * H_FOX, nk * H_FOX),
                                c_new)
            o_fox_s = o_dec.reshape(ms, fw)
            ps_pad = _pad_rows(ps, db, t, CHUNK).reshape(1, db * CHUNK, -1)
            hn_s, c_s, n_s, m_s = _mlstm(ps_pad, 0, gts4, state_mlstm_c[j],
                                         state_mlstm_n[j].reshape(db, H_MLSTM, HEAD_DIM, 1),
                                         state_mlstm_m[j].reshape(db, H_MLSTM, 1, 1),
                                         head_gain, j, db, CHUNK)
            hn_s = hn_s.reshape(db, CHUNK, mw)[:, :t].reshape(ms, mw)
            xs = _proj_out(o_fox_s, 0, hn_s, 0, ab_w_out, j, xs, md, 5)
            outs["fk_s"].append(ps[:, fw:2 * fw].reshape(db, t, H_FOX, HEAD_DIM))
            outs["fv_s"].append(ps[:, 2 * fw:3 * fw].reshape(db, t, H_FOX, HEAD_DIM))
            outs["fl_s"].append(gts[:, 24:32, :t].transpose(0, 2, 1))
            outs["mc_s"].append(c_s)
            outs["mn_s"].append(n_s.reshape(db, H_MLSTM, HEAD_DIM))
            outs["mm_s"].append(m_s.reshape(db, H_MLSTM))
        else:
            pp_c, ps = _proj_in(xp, xs, mp, md, (3, 4), norm_mix3, l, c_w_in, None, j, j, n_c, pp_c)
            o_dil = _dil_prompt(pp_c, j, nb, s)
            xp = _proj_out(o_dil, 0, o_dil, 1, c_w_out, j, xp, mp, 5)

            k_new = _head_rows(ps[:, dw:2 * dw], db, t, H_DIL)
            v_new = _head_rows(ps[:, 2 * dw:], db, t, H_DIL)
            nrow = NEW_KEYS * H_DIL
            o_dec = _dil_decode(_head_rows(ps[:, :dw], db, t, H_DIL), k_win, v_win, j,
                                jnp.pad(k_new, ((0, 0), (0, nrow - t * H_DIL), (0, 0))),
                                jnp.pad(v_new, ((0, 0), (0, nrow - t * H_DIL), (0, 0))), t)
            o_dil_s = o_dec.reshape(ms, dw)
            xs = _proj_out(o_dil_s, 0, o_dil_s, 1, c_w_out, j, xs, md, 5)
            outs["wk_new"].append(k_new.reshape(db, t, H_DIL, HEAD_DIM))
            outs["wv_new"].append(v_new.reshape(db, t, H_DIL, HEAD_DIM))

        fin = norm_final if last else None
        xp, xs = _ffn(xp, xs, mp, md, (6, 7, 8), norm_post3, ffn_post_w13, ffn_post_w2, l, fin)

    st = lambda name: jnp.stack(outs[name])
    keep = min(DIL_PATTERNS[-1][0], cb + t)
    wk_s = jnp.concatenate([cache_win_k, st("wk_new")], axis=2)[:, :, cb + t - keep:]
    wv_s = jnp.concatenate([cache_win_v, st("wv_new")], axis=2)[:, :, cb + t - keep:]
    keep_p = min(DIL_PATTERNS[-1][0], s)
    heads = lambda a, n_heads: a.reshape(a.shape[0], nb, s, n_heads, HEAD_DIM)
    return (xp.reshape(nb, s, d), xs.reshape(db, t, d),
            heads(pp_ab[:, :, fw:2 * fw], H_FOX), heads(pp_ab[:, :, 2 * fw:3 * fw], H_FOX),
            st("fl_p"), st("fk_s"), st("fv_s"), st("fl_s"),
            st("mc_p"), st("mn_p"), st("mm_p"), st("mc_s"), st("mn_s"), st("mm_s"),
            heads(pp_c[:, :, dw:2 * dw], H_DIL)[:, :, s - keep_p:], heads(pp_c[:, :, 2 * dw:], H_DIL)[:, :, s - keep_p:],
            wk_s, wv_s)
```

```python
import functools

import jax
import jax.numpy as jnp
from jax import lax
from jax.experimental import pallas as pl
from jax.experimental.pallas import tpu as pltpu

F32 = jnp.float32
BF16 = jnp.bfloat16

HEAD_DIM = 128
H_FOX = 8
H_MLSTM = 8
H_DIL = 16
N_MOD = 9
EPS = 1e-6
FFN_RES = 0.5
QK_SCALE = HEAD_DIM ** -0.5
DIL_PATTERNS = ((128, 1), (512, 4), (2048, 16))
CHUNK = 128
NEG = -1e30

LANES = 128
SUBLANES = 8
VMEM_BIG = 60 * 1024 * 1024
VMEM_MID = 48 * 1024 * 1024

TM = 1024
TF = 256
TN = 1024
TN_IN = 1024
ADALN_TN = 2304
TQ = 256
MLSTM_HEADS = 2
PAGES_PER_STEP = 8
DIL_TK = 512
NEW_KEYS = 16


def _params(sem, vmem=VMEM_MID):
    return pltpu.CompilerParams(dimension_semantics=sem, vmem_limit_bytes=vmem)


def _sigmoid(x):
    return 1.0 / (1.0 + jnp.exp(-x))


def _log_sigmoid(x):
    return jnp.minimum(x, 0.0) - jnp.log1p(jnp.exp(-jnp.abs(x)))


def _dot(a, b):
    return jnp.dot(a, b, preferred_element_type=F32)


def _dot_nt(a, b):
    return lax.dot_general(a, b, (((1,), (1,)), ((), ())), preferred_element_type=F32)


def _rms(x):
    return x * lax.rsqrt(jnp.mean(x * x, axis=-1, keepdims=True) + EPS)


def _norm_mod(x, g, shift, scale):
    return _rms(x) * g * (1.0 + scale) + shift


def _cumsum_lanes(x):
    lane = lax.broadcasted_iota(jnp.int32, x.shape, 1)
    sh = 1
    while sh < x.shape[1]:
        x = x + jnp.where(lane >= sh, pltpu.roll(x, sh, 1), 0.0)
        sh *= 2
    return x


class _Mods:
    def __init__(self, arr, layer, prompt, tm, n_seq=None, seq_rows=None):
        self.arr, self.layer, self.prompt, self.tm, self.n_seq, self.seq_rows = arr, layer, prompt, tm, n_seq, seq_rows

    def spec(self, k, tn=None):
        d = self.arr.shape[-1] if self.prompt else self.arr.shape[-1] // N_MOD
        l, tm = self.layer, self.tm
        if self.prompt:
            n_seq, seq_rows = self.n_seq, self.seq_rows
            row = lambda i: (l * n_seq + (i * tm) // seq_rows) * N_MOD + k
            if tn is None:
                return pl.BlockSpec((None, 1, d), lambda i, *_: (row(i), 0, 0))
            return pl.BlockSpec((None, 1, tn), lambda i, j: (row(i), 0, j))
        if tn is None:
            return pl.BlockSpec((None, tm, d), lambda i, *_: (l, 0, k))
        return pl.BlockSpec((None, tm, tn), lambda i, j: (l, 0, k * (d // tn) + j))


def _adaln_kernel(c_ref, w_ref, b_ref, o_ref):
    c = c_ref[...]
    s = (c * _sigmoid(c)).astype(BF16)
    o_ref[...] = _dot(s, w_ref[...].astype(BF16)) + b_ref[...]


def _adaln_all(c_rows, w_ada, b_ada):
    depth, d, n = w_ada.shape
    rows = c_rows.shape[0]
    tn = ADALN_TN
    assert n % tn == 0
    return pl.pallas_call(
        _adaln_kernel,
        grid=(depth, n // tn),
        in_specs=[pl.BlockSpec((rows, d), lambda l, j: (0, 0)),
                  pl.BlockSpec((None, d, tn), lambda l, j: (l, 0, j)),
                  pl.BlockSpec((None, 1, tn), lambda l, j: (l, 0, j))],
        out_specs=pl.BlockSpec((None, rows, tn), lambda l, j: (l, 0, j)),
        out_shape=jax.ShapeDtypeStruct((depth, rows, n), F32),
        compiler_params=_params(("parallel", "parallel")),
        name="adaln",
    )(c_rows, w_ada, b_ada.reshape(depth, 1, n))


def _ffn_kernel(x_ref, sh_ref, sc_ref, gt_ref, xs_ref, shs_ref, scs_ref, gts_ref, g_ref, w13_hbm, w2_hbm,
                *rest, layer, d_ff, nf, tf, rem, final, tm, ni):
    if final:
        gf_ref, o_ref, os_ref, h_ref, w1_buf, w3_buf, w2_buf, sem = rest
    else:
        o_ref, os_ref, h_ref, w1_buf, w3_buf, w2_buf, sem = rest
    i = pl.program_id(0)

    def tile_copies(f, slot, width):
        c0 = f * tf if isinstance(f, int) else pl.multiple_of(f * tf, tf)
        return (pltpu.make_async_copy(w13_hbm.at[layer, :, pl.ds(c0, width)],
                                      w1_buf.at[slot, :, pl.ds(0, width)], sem.at[slot, 0]),
                pltpu.make_async_copy(w13_hbm.at[layer, :, pl.ds(d_ff + c0, width)],
                                      w3_buf.at[slot, :, pl.ds(0, width)], sem.at[slot, 1]),
                pltpu.make_async_copy(w2_hbm.at[layer, pl.ds(c0, width), :],
                                      w2_buf.at[slot, pl.ds(0, width), :], sem.at[slot, 2]))

    def start(f, slot, width):
        for c in tile_copies(f, slot, width):
            c.start()

    def wait(f, slot, width):
        for c in tile_copies(f, slot, width):
            c.wait()

    def finish(x, gate, acc):
        y = x + (FFN_RES * gate) * acc
        return _rms(y) * gf_ref[...] if final else y

    def run(rows):
        with_sample = rows > tm

        def part(w1, w3, w2):
            h = h_ref[:rows, :]
            a = _dot(h, w1.astype(BF16))
            g = _dot(h, w3.astype(BF16))
            u = (a * _sigmoid(a) * g).astype(BF16)
            return _dot(u, w2.astype(BF16))

        start(0, 0, tf)
        h_ref[:tm, :] = _norm_mod(x_ref[...], g_ref[...], sh_ref[...], sc_ref[...]).astype(BF16)
        if with_sample:
            h_ref[tm:, :] = _norm_mod(xs_ref[...], g_ref[...], shs_ref[...], scs_ref[...]).astype(BF16)

        wait(0, 0, tf)
        start(1, 1, tf)
        p = part(w1_buf[0], w3_buf[0], w2_buf[0])
        o_ref[...] = p[:tm]
        if with_sample:
            os_ref[...] = p[tm:]

        def body(f, carry):
            slot = lax.rem(f, 2)
            wait(f, slot, tf)

            @pl.when(f + 1 < nf - 1)
            def _():
                start(f + 1, 1 - slot, tf)

            @pl.when(f + 1 == nf - 1)
            def _():
                start(nf - 1, 1 - slot, rem)

            p = part(w1_buf[slot], w3_buf[slot], w2_buf[slot])
            o_ref[...] += p[:tm]
            if with_sample:
                os_ref[...] += p[tm:]
            return carry

        lax.fori_loop(1, nf - 1, body, 0)

        slot = (nf - 1) % 2
        wait(nf - 1, slot, rem)
        p = part(w1_buf[slot, :, :rem], w3_buf[slot, :, :rem], w2_buf[slot, :rem, :])
        o_ref[...] = finish(x_ref[...], gt_ref[...], o_ref[...] + p[:tm])
        if with_sample:
            os_ref[...] = finish(xs_ref[...], gts_ref[...], os_ref[...] + p[tm:])

    @pl.when(i < ni - 1)
    def _():
        run(tm)

    @pl.when(i == ni - 1)
    def _():
        run(h_ref.shape[0])


def _ffn(x, xs, mods, mods_s, ks, g_all, w13_all, w2_all, layer, final_g=None):
    m, d = x.shape
    ms = xs.shape[0]
    tm = mods.tm
    d_ff = w2_all.shape[1]
    tf = TF
    nf = pl.cdiv(d_ff, tf)
    rem = d_ff - (nf - 1) * tf
    ni = m // tm
    assert nf >= 3 and m % tm == 0 and d_ff % LANES == 0 and ni >= 2 and mods_s.tm == ms
    final = final_g is not None
    in_specs = [pl.BlockSpec((tm, d), lambda i: (i, 0)),
                mods.spec(ks[0]), mods.spec(ks[1]), mods.spec(ks[2]),
                pl.BlockSpec((ms, d), lambda i: (0, 0)),
                mods_s.spec(ks[0]), mods_s.spec(ks[1]), mods_s.spec(ks[2]),
                pl.BlockSpec((None, 1, d), lambda i: (layer, 0, 0)),
                pl.BlockSpec(memory_space=pl.ANY),
                pl.BlockSpec(memory_space=pl.ANY)]
    args = [x, mods.arr, mods.arr, mods.arr, xs, mods_s.arr, mods_s.arr, mods_s.arr, g_all, w13_all, w2_all]
    if final:
        in_specs.append(pl.BlockSpec((1, d), lambda i: (0, 0)))
        args.append(final_g.reshape(1, d))
    return pl.pallas_call(
        functools.partial(_ffn_kernel, layer=layer, d_ff=d_ff, nf=nf, tf=tf, rem=rem, final=final, tm=tm, ni=ni),
        grid=(ni,),
        in_specs=in_specs,
        out_specs=[pl.BlockSpec((tm, d), lambda i: (i, 0)), pl.BlockSpec((ms, d), lambda i: (0, 0))],
        out_shape=[jax.ShapeDtypeStruct((m, d), F32), jax.ShapeDtypeStruct((ms, d), F32)],
        scratch_shapes=[pltpu.VMEM((tm + ms, d), BF16),
                        pltpu.VMEM((2, d, tf), F32), pltpu.VMEM((2, d, tf), F32), pltpu.VMEM((2, tf, d), F32),
                        pltpu.SemaphoreType.DMA((2, 3))],
        compiler_params=_params(("arbitrary",), VMEM_BIG),
        name="ffn",
    )(*args)


def _proj_in_kernel(x_ref, sh_ref, sc_ref, xs_ref, shs_ref, scs_ref, g_ref, w_ref, *rest, gates, aliased, tm, ni):
    rest = list(rest)
    wg_ref = rest.pop(0) if gates else None
    if aliased:
        rest.pop(0)
    if gates:
        o_ref, os_ref, og_ref, ogs_ref, h_ref = rest
    else:
        o_ref, os_ref, h_ref = rest
    i = pl.program_id(0)
    n = pl.program_id(1)

    @pl.when(n == 0)
    def _():
        h = _norm_mod(x_ref[...], g_ref[...], sh_ref[...], sc_ref[...]).astype(BF16)
        h_ref[:tm, :] = h
        if gates:
            og_ref[...] = _dot(h, wg_ref[...].astype(BF16))

    @pl.when(jnp.logical_and(n == 0, i == ni - 1))
    def _():
        hs = _norm_mod(xs_ref[...], g_ref[...], shs_ref[...], scs_ref[...]).astype(BF16)
        h_ref[tm:, :] = hs
        if gates:
            ogs_ref[...] = _dot(hs, wg_ref[...].astype(BF16))

    @pl.when(i < ni - 1)
    def _():
        o_ref[...] = _dot(h_ref[:tm, :], w_ref[...].astype(BF16))

    @pl.when(i == ni - 1)
    def _():
        y = _dot(h_ref[...], w_ref[...].astype(BF16))
        o_ref[...] = y[:tm]
        os_ref[...] = y[tm:]


def _proj_in(x, xs, mods, mods_s, ks, g_all, layer, w_all, wg_all, j, slot=0, n_slots=1, stacked=None):
    m, d = x.shape
    ms = xs.shape[0]
    tm = mods.tm
    n = w_all.shape[2]
    tn = TN_IN
    ni = m // tm
    assert n % tn == 0 and m % tm == 0 and (stacked is None) == (slot == 0) and ni >= 2 and mods_s.tm == ms
    gates = wg_all is not None
    aliased = stacked is not None
    s_col = lambda i, c: jnp.where(i == ni - 1, c, 0)
    in_specs = [pl.BlockSpec((tm, d), lambda i, c: (i, 0)), mods.spec(ks[0]), mods.spec(ks[1]),
                pl.BlockSpec((ms, d), lambda i, c: (0, 0)), mods_s.spec(ks[0]), mods_s.spec(ks[1]),
                pl.BlockSpec((None, 1, d), lambda i, c: (layer, 0, 0)),
                pl.BlockSpec((None, d, tn), lambda i, c: (j, 0, c))]
    out_specs = [pl.BlockSpec((None, tm, tn), lambda i, c: (slot, i, c)),
                 pl.BlockSpec((ms, tn), lambda i, c: (0, s_col(i, c)))]
    out_shape = [jax.ShapeDtypeStruct((n_slots, m, n), F32), jax.ShapeDtypeStruct((ms, n), F32)]
    args = [x, mods.arr, mods.arr, xs, mods_s.arr, mods_s.arr, g_all, w_all]
    if gates:
        in_specs.append(pl.BlockSpec((None, d, LANES), lambda i, c: (j, 0, 0)))
        out_specs += [pl.BlockSpec((tm, LANES), lambda i, c: (i, 0)), pl.BlockSpec((ms, LANES), lambda i, c: (0, 0))]
        out_shape += [jax.ShapeDtypeStruct((m, LANES), F32), jax.ShapeDtypeStruct((ms, LANES), F32)]
        args.append(wg_all)
    if aliased:
        in_specs.append(pl.BlockSpec(memory_space=pl.ANY))
        args.append(stacked)
    return pl.pallas_call(
        functools.partial(_proj_in_kernel, gates=gates, aliased=aliased, tm=tm, ni=ni),
        grid=(ni, n // tn),
        in_specs=in_specs, out_specs=out_specs, out_shape=out_shape,
        input_output_aliases={len(args) - 1: 0} if aliased else {},
        scratch_shapes=[pltpu.VMEM((tm + ms, d), BF16)],
        compiler_params=_params(("arbitrary", "arbitrary"), VMEM_BIG),
        name="proj_in",
    )(*args)


def _proj_out_kernel(a_ref, b_ref, as_ref, bs_ref, wa_ref, wb_ref, x_ref, gt_ref, xs_ref, gts_ref, o_ref, os_ref,
                     *, ni):
    wa = wa_ref[...].astype(BF16)
    wb = wb_ref[...].astype(BF16)
    o_ref[...] = x_ref[...] + gt_ref[...] * (_dot(a_ref[...], wa) + _dot(b_ref[...], wb))

    @pl.when(pl.program_id(0) == ni - 1)
    def _():
        os_ref[...] = xs_ref[...] + gts_ref[...] * (_dot(as_ref[...], wa) + _dot(bs_ref[...], wb))


def _proj_out(mix_a, a_blk, mix_b, b_blk, mix_as, mix_bs, w_all, j, x, xs, mods, mods_s, k):
    m, d = x.shape
    ms = xs.shape[0]
    tm = mods.tm
    kh = w_all.shape[1] // 2
    tn = TN
    ni = m // tm
    assert mods_s.tm == ms and ni >= 2
    s_col = lambda i, c: jnp.where(i == ni - 1, c, 0)
    layer = mods_s.layer
    return pl.pallas_call(
        functools.partial(_proj_out_kernel, ni=ni),
        grid=(ni, d // tn),
        in_specs=[pl.BlockSpec((tm, kh), lambda i, c: (i, a_blk)),
                  pl.BlockSpec((tm, kh), lambda i, c: (i, b_blk)),
                  pl.BlockSpec((ms, kh), lambda i, c: (0, a_blk)),
                  pl.BlockSpec((ms, kh), lambda i, c: (0, b_blk)),
                  pl.BlockSpec((None, kh, tn), lambda i, c: (j, 0, c)),
                  pl.BlockSpec((None, kh, tn), lambda i, c: (j, 1, c)),
                  pl.BlockSpec((tm, tn), lambda i, c: (i, c)),
                  mods.spec(k, tn),
                  pl.BlockSpec((ms, tn), lambda i, c: (0, s_col(i, c))),
                  pl.BlockSpec((None, ms, tn), lambda i, c: (layer, 0, k * (d // tn) + s_col(i, c)))],
        out_specs=[pl.BlockSpec((tm, tn), lambda i, c: (i, c)),
                   pl.BlockSpec((ms, tn), lambda i, c: (0, s_col(i, c)))],
        out_shape=[jax.ShapeDtypeStruct((m, d), F32), jax.ShapeDtypeStruct((ms, d), F32)],
        compiler_params=_params(("arbitrary", "arbitrary"), VMEM_BIG),
        name="proj_out",
    )(mix_a, mix_b, mix_as, mix_bs, w_all, w_all, x, mods.arr, xs, mods_s.arr)


def _gates_kernel(p_ref, b_ref, o_ref, *, n_valid):
    s = p_ref.shape[0]
    lane = lax.broadcasted_iota(jnp.int32, (8, LANES), 1)
    carry = jnp.zeros((8, 1), F32)
    for blk in range(s // LANES):
        sl = slice(blk * LANES, (blk + 1) * LANES)
        pre = p_ref[sl, :].T[0:32] + b_ref[...]
        valid = (lane + blk * LANES) < n_valid
        lf_f = jnp.where(valid, _log_sigmoid(pre[0:8]), 0.0)
        ig = jnp.where(valid, pre[8:16], NEG)
        lf_m = jnp.where(valid, _log_sigmoid(pre[16:24]), 0.0)
        c_f = _cumsum_lanes(lf_f) + carry
        carry = c_f[:, LANES - 1:LANES]
        o_ref[0:8, sl] = c_f
        o_ref[8:16, sl] = ig
        o_ref[16:24, sl] = _cumsum_lanes(lf_m)
        o_ref[24:32, sl] = lf_f


def _gates(pre, bias, n_seq, s, n_valid):
    return pl.pallas_call(
        functools.partial(_gates_kernel, n_valid=n_valid),
        grid=(n_seq,),
        in_specs=[pl.BlockSpec((s, LANES), lambda b: (b, 0)),
                  pl.BlockSpec((32, 1), lambda b: (0, 0))],
        out_specs=pl.BlockSpec((None, 32, s), lambda b: (b, 0, 0)),
        out_shape=jax.ShapeDtypeStruct((n_seq, 32, s), F32),
        compiler_params=_params(("parallel",)),
        name="gates",
    )(pre, bias)


def _fox_kernel(q_ref, k_ref, v_ref, c_ref, o_ref, kb_ref, vb_ref, *, tq):
    s = q_ref.shape[0]
    kb_ref[...] = k_ref[...].astype(BF16)
    vb_ref[...] = v_ref[...].astype(BF16)
    row = lax.broadcasted_iota(jnp.int32, (tq, tq), 0)
    col = lax.broadcasted_iota(jnp.int32, (tq, tq), 1)
    for qi in range(s // tq):
        q = (q_ref[qi * tq:(qi + 1) * tq, :] * QK_SCALE).astype(BF16)
        m = l = acc = None
        for kb in range(qi + 1):
            ks = slice(kb * tq, (kb + 1) * tq)
            sc = _dot_nt(q, kb_ref[ks, :]) - c_ref[:, ks]
            if kb == qi:
                sc = jnp.where(col <= row, sc, -jnp.inf)
            bm = jnp.max(sc, axis=-1, keepdims=True)
            if m is None:
                m = bm
                p = jnp.exp(sc - m)
                l = jnp.sum(p, axis=-1, keepdims=True)
                acc = _dot(p.astype(BF16), vb_ref[ks, :])
            else:
                m_new = jnp.maximum(m, bm)
                alpha = jnp.exp(m - m_new)
                p = jnp.exp(sc - m_new)
                l = alpha * l + jnp.sum(p, axis=-1, keepdims=True)
                acc = alpha * acc + _dot(p.astype(BF16), vb_ref[ks, :])
                m = m_new
        o_ref[qi * tq:(qi + 1) * tq, :] = (acc / l).astype(o_ref.dtype)


def _fox_prompt(proj, slot, gt4, n_seq, s):
    col = lambda base: pl.BlockSpec((None, s, HEAD_DIM), lambda b, h: (slot, b, base + h))
    return pl.pallas_call(
        functools.partial(_fox_kernel, tq=TQ),
        grid=(n_seq, H_FOX),
        in_specs=[col(0), col(H_FOX), col(2 * H_FOX),
                  pl.BlockSpec((None, None, 1, s), lambda b, h: (b, h, 0, 0))],
        out_specs=pl.BlockSpec((s, HEAD_DIM), lambda b, h: (b, h)),
        out_shape=jax.ShapeDtypeStruct((n_seq * s, H_FOX * HEAD_DIM), BF16),
        scratch_shapes=[pltpu.VMEM((s, HEAD_DIM), BF16), pltpu.VMEM((s, HEAD_DIM), BF16)],
        compiler_params=_params(("parallel", "parallel")),
        name="fox_prompt",
    )(proj, proj, proj, gt4)


def _page_cumsum_kernel(x_ref, o_ref):
    o_ref[...] = _cumsum_lanes(x_ref[...])


def _page_cumsum(lf_t):
    rows, page = lf_t.shape
    tr = 1024
    assert rows % tr == 0 and page == LANES
    return pl.pallas_call(
        _page_cumsum_kernel,
        grid=(rows // tr,),
        in_specs=[pl.BlockSpec((tr, page), lambda i: (i, 0))],
        out_specs=pl.BlockSpec((tr, page), lambda i: (i, 0)),
        out_shape=jax.ShapeDtypeStruct((rows, page), F32),
        compiler_params=_params(("parallel",)),
        name="page_cumsum",
    )(lf_t)


def _fox_decode_kernel(pt_ref, q_ref, *refs, g, n_steps):
    del pt_ref
    k_refs, v_refs, cs_refs = refs[:g], refs[g:2 * g], refs[2 * g:3 * g]
    kn_ref, vn_ref, cn_ref, o_ref, m_ref, l_ref, acc_ref, carry_ref = refs[3 * g:]
    p = pl.program_id(1)
    rows = q_ref.shape[0]
    hbits = H_FOX.bit_length() - 1

    @pl.when(p == 0)
    def _():
        m_ref[...] = jnp.full(m_ref.shape, -jnp.inf, F32)
        l_ref[...] = jnp.zeros(l_ref.shape, F32)
        acc_ref[...] = jnp.zeros(acc_ref.shape, F32)
        carry_ref[...] = jnp.zeros(carry_ref.shape, F32)

    def iotas(ncols):
        return (lax.broadcasted_iota(jnp.int32, (rows, ncols), 0),
                lax.broadcasted_iota(jnp.int32, (rows, ncols), 1))

    def attend(blocks, mask):
        q = (q_ref[...] * QK_SCALE).astype(BF16)
        scores = [jnp.where(mask, _dot_nt(q, k.astype(BF16)) - bias, -jnp.inf) for k, _, bias in blocks]
        m_old = m_ref[...]
        m_new = m_old
        for sc in scores:
            m_new = jnp.maximum(m_new, jnp.max(sc, axis=-1, keepdims=True))
        alpha = jnp.exp(m_old - m_new)
        l = alpha * l_ref[...]
        acc = alpha * acc_ref[...]
        for sc, (_, v, _) in zip(scores, blocks):
            pr = jnp.exp(sc - m_new)
            l = l + jnp.sum(pr, axis=-1, keepdims=True)
            acc = acc + _dot(pr.astype(BF16), v.astype(BF16))
        m_ref[...] = m_new
        l_ref[...] = l
        acc_ref[...] = acc

    @pl.when(p < n_steps)
    def _():
        r, c = iotas(k_refs[0].shape[0])
        mask = (r & (H_FOX - 1)) == (c & (H_FOX - 1))
        carry = carry_ref[...]
        blocks = []
        for i in range(g):
            cs = cs_refs[i][...]
            blocks.append((k_refs[i][...], v_refs[i][...], carry + cs[0:1]))
            carry = carry + cs[1:2]
        carry_ref[...] = carry
        attend(blocks, mask)

    @pl.when(p == n_steps)
    def _():
        ncol = kn_ref.shape[0]
        r, c = iotas(ncol)
        same_head = (r & (H_FOX - 1)) == (c & (H_FOX - 1))
        causal = lax.shift_right_logical(c, hbits) <= lax.shift_right_logical(r, hbits)
        bias = carry_ref[:, :ncol] + cn_ref[...]
        attend([(kn_ref[...], vn_ref[...], bias)], jnp.logical_and(same_head, causal))
        o_ref[...] = acc_ref[...] / l_ref[...]


def _fox_decode(page_table, q, k_pool, v_pool, cs_pool, j, k_new, v_new, c_new):
    db, rows, hd = q.shape
    n_pages = page_table.shape[1]
    prow = k_pool.shape[2]
    g = PAGES_PER_STEP
    assert n_pages % g == 0
    n_steps = n_pages // g
    nrow = k_new.shape[1]

    def pool_map(i):
        return lambda b, p, pt: (j, pt[b, jnp.minimum(p * g + i, n_pages - 1)], 0, 0)

    seq_map = lambda b, p, pt: (b, 0, 0)
    in_specs = [pl.BlockSpec((None, rows, hd), seq_map)]
    in_specs += [pl.BlockSpec((None, None, prow, hd), pool_map(i)) for i in range(g)]
    in_specs += [pl.BlockSpec((None, None, prow, hd), pool_map(i)) for i in range(g)]
    in_specs += [pl.BlockSpec((None, None, 2, prow), pool_map(i)) for i in range(g)]
    in_specs += [pl.BlockSpec((None, nrow, hd), seq_map), pl.BlockSpec((None, nrow, hd), seq_map),
                 pl.BlockSpec((None, 1, nrow), seq_map)]
    grid_spec = pltpu.PrefetchScalarGridSpec(
        num_scalar_prefetch=1,
        grid=(db, n_steps + 1),
        in_specs=in_specs,
        out_specs=pl.BlockSpec((None, rows, hd), seq_map),
        scratch_shapes=[pltpu.VMEM((rows, 1), F32), pltpu.VMEM((rows, 1), F32),
                        pltpu.VMEM((rows, hd), F32), pltpu.VMEM((1, prow), F32)])
    return pl.pallas_call(
        functools.partial(_fox_decode_kernel, g=g, n_steps=n_steps),
        grid_spec=grid_spec,
        out_shape=jax.ShapeDtypeStruct((db, rows, hd), F32),
        compiler_params=_params(("parallel", "arbitrary")),
        name="fox_decode",
    )(page_table, q, *([k_pool] * g), *([v_pool] * g), *([cs_pool] * g), k_new, v_new, c_new)


def _mlstm_kernel(q_ref, k_ref, v_ref, og_ref, *refs, nh):
    ig_refs, b_refs = refs[:nh], refs[nh:2 * nh]
    c0_ref, n0_ref, m0_ref, g_ref, hn_ref, c_ref, n_ref, m_ref = refs[2 * nh:]
    s = q_ref.shape[0]
    ln = CHUNK
    row = lax.broadcasted_iota(jnp.int32, (ln, ln), 0)
    col = lax.broadcasted_iota(jnp.int32, (ln, ln), 1)
    causal = col <= row

    last_lane = lax.broadcasted_iota(jnp.int32, (1, ln), 1) == ln - 1
    assert ln == HEAD_DIM == LANES
    for hh in range(nh):
        hs = slice(hh * HEAD_DIM, (hh + 1) * HEAD_DIM)
        c_st = c0_ref[hh]
        n_st = jnp.broadcast_to(n0_ref[hh], (HEAD_DIM, LANES))
        m_st = jnp.broadcast_to(m0_ref[hh], (1, LANES))
        for ci in range(s // ln):
            ts = slice(ci * ln, (ci + 1) * ln)
            qb = q_ref[ts, hs].astype(BF16)
            k_t = (k_ref[ts, hs] * QK_SCALE).T
            vb = v_ref[ts, hs].astype(BF16)
            b_row = b_refs[hh][:, ts]
            ig_row = ig_refs[hh][:, ts]
            b_mat = jnp.broadcast_to(b_row, (ln, ln))
            b_col = b_mat.T
            dmat = jnp.where(causal, b_col - b_mat + ig_row, NEG)
            inter = b_col + m_st
            m_tok = jnp.maximum(inter, jnp.max(dmat, axis=-1, keepdims=True))
            w_inter = jnp.exp(inter - m_tok)
            qk = _dot(qb, k_t.astype(BF16)) * jnp.exp(dmat - m_tok)
            num = w_inter * _dot(qb, c_st.astype(BF16)) + _dot(qk.astype(BF16), vb)
            den = w_inter * _dot(qb, n_st.astype(BF16)) + jnp.sum(qk, axis=-1, keepdims=True)
            h = num / jnp.maximum(jnp.abs(den), jnp.exp(-m_tok))
            hn_ref[ts, hs] = (_rms(h) * g_ref[hh] * _sigmoid(og_ref[ts, hs])).astype(hn_ref.dtype)
            b_last = jnp.sum(jnp.where(last_lane, b_row, 0.0), axis=-1, keepdims=True)
            g_row = b_last - b_row + ig_row
            m_new = jnp.maximum(b_last + m_st, jnp.max(g_row, axis=-1, keepdims=True))
            a_prev = jnp.exp(b_last + m_st - m_new)
            kw = k_t * jnp.exp(g_row - m_new)
            c_st = a_prev * c_st + _dot(kw.astype(BF16), vb)
            n_st = a_prev * n_st + jnp.sum(kw, axis=-1, keepdims=True)
            m_st = m_new
        c_ref[hh] = c_st
        n_ref[hh] = n_st[:, 0:1]
        m_ref[hh] = m_st[:, 0:1]


def _mlstm(proj, slot, gt4, c0, n0, m0, g_all, j, n_seq, s):
    h, nh = H_MLSTM, MLSTM_HEADS
    w = nh * HEAD_DIM
    col = lambda base: pl.BlockSpec((None, s, w), lambda b, c: (slot, b, base // nh + c))
    row = lambda base, hh: pl.BlockSpec((None, None, 1, s), lambda b, c: (b, base + c * nh + hh, 0, 0))
    st = lambda shape: pl.BlockSpec((None, nh) + shape, lambda b, c: (b, c, 0, 0))
    return pl.pallas_call(
        functools.partial(_mlstm_kernel, nh=nh),
        grid=(n_seq, h // nh),
        in_specs=[col(24), col(32), col(40), col(48)]
        + [row(8, hh) for hh in range(nh)] + [row(16, hh) for hh in range(nh)]
        + [st((HEAD_DIM, HEAD_DIM)), st((HEAD_DIM, 1)), st((1, 1)),
           pl.BlockSpec((None, nh, 1, HEAD_DIM), lambda b, c: (j, c, 0, 0))],
        out_specs=[pl.BlockSpec((s, w), lambda b, c: (b, c)),
                   st((HEAD_DIM, HEAD_DIM)), st((HEAD_DIM, 1)), st((1, 1))],
        out_shape=[jax.ShapeDtypeStruct((n_seq * s, h * HEAD_DIM), BF16),
                   jax.ShapeDtypeStruct((n_seq, h, HEAD_DIM, HEAD_DIM), F32),
                   jax.ShapeDtypeStruct((n_seq, h, HEAD_DIM, 1), F32),
                   jax.ShapeDtypeStruct((n_seq, h, 1, 1), F32)],
        compiler_params=_params(("parallel", "parallel")),
        name="mlstm",
    )(proj, proj, proj, proj, *([gt4] * (2 * nh)), c0, n0, m0, g_all)


def _dil_kernel(q_ref, k_ref, v_ref, o_ref, m_ref, l_ref, acc_ref):
    s = q_ref.shape[0]
    blk = CHUNK
    row = lax.broadcasted_iota(jnp.int32, (blk, blk), 0)
    col = lax.broadcasted_iota(jnp.int32, (blk, blk), 1)
    cur_mask = col <= row
    dist = blk + lax.broadcasted_iota(jnp.int32, (blk, 2 * blk), 0) - lax.broadcasted_iota(jnp.int32, (blk, 2 * blk), 1)
    band_mask = jnp.logical_and(dist >= 0, dist <= blk)

    def rows(start, size, d):
        return pl.ds(start, size) if d == 1 else pl.ds(start, size, stride=d)

    for pi, (window, d) in enumerate(DIL_PATTERNS):
        assert window // d == blk and s % (d * blk) == 0
        for r in range(d):
            for n in range(s // (d * blk)):
                start = r + d * blk * n
                q = (q_ref[rows(start, blk, d), :] * QK_SCALE).astype(BF16)
                if n == 0:
                    kv_rows, mask = rows(start, blk, d), cur_mask
                else:
                    kv_rows, mask = rows(start - d * blk, 2 * blk, d), band_mask
                sc = _dot_nt(q, k_ref[kv_rows, :].astype(BF16))
                sc = jnp.where(mask, sc, -jnp.inf)
                m = jnp.max(sc, axis=-1, keepdims=True)
                p = jnp.exp(sc - m)
                out_rows = rows(pi * s + start, blk, d)
                m_ref[out_rows, :] = jnp.broadcast_to(m, (blk, LANES))
                l_ref[out_rows, :] = jnp.broadcast_to(jnp.sum(p, axis=-1, keepdims=True), (blk, LANES))
                acc_ref[out_rows, :] = _dot(p.astype(BF16), v_ref[kv_rows, :].astype(BF16))

    n_pat = len(DIL_PATTERNS)
    ms = [m_ref[pi * s:(pi + 1) * s, :] for pi in range(n_pat)]
    m_all = functools.reduce(jnp.maximum, ms)
    den = jnp.zeros((s, LANES), F32)
    num = jnp.zeros((s, HEAD_DIM), F32)
    for pi in range(n_pat):
        w = jnp.exp(ms[pi] - m_all)
        den = den + w * l_ref[pi * s:(pi + 1) * s, :]
        num = num + w * acc_ref[pi * s:(pi + 1) * s, :]
    o_ref[...] = (num / den).astype(o_ref.dtype)


def _dil_prompt(proj, slot, n_seq, s):
    h = H_DIL
    n_pat = len(DIL_PATTERNS)
    col = lambda base: pl.BlockSpec((None, s, HEAD_DIM), lambda b, c: (slot, b, base + c))
    return pl.pallas_call(
        _dil_kernel,
        grid=(n_seq, h),
        in_specs=[col(0), col(h), col(2 * h)],
        out_specs=pl.BlockSpec((s, HEAD_DIM), lambda b, c: (b, c)),
        out_shape=jax.ShapeDtypeStruct((n_seq * s, h * HEAD_DIM), BF16),
        scratch_shapes=[pltpu.VMEM((n_pat * s, LANES), F32), pltpu.VMEM((n_pat * s, LANES), F32),
                        pltpu.VMEM((n_pat * s, HEAD_DIM), F32)],
        compiler_params=_params(("parallel", "parallel")),
        name="dil_prompt",
    )(proj, proj, proj)


def _dil_decode_kernel(q_ref, k_ref, v_ref, cnt_ref, kn_ref, vn_ref, cntn_ref, o_ref, m_ref, l_ref, acc_ref,
                       *, n_tiles):
    p = pl.program_id(1)

    @pl.when(p == 0)
    def _():
        m_ref[...] = jnp.full(m_ref.shape, -jnp.inf, F32)
        l_ref[...] = jnp.zeros(l_ref.shape, F32)
        acc_ref[...] = jnp.zeros(acc_ref.shape, F32)

    def process(k, v, cnt):
        cnt = cnt.astype(F32)
        q = (q_ref[...] * QK_SCALE).astype(BF16)
        s = jnp.where(cnt > 0.0, _dot_nt(q, k.astype(BF16)), -jnp.inf)
        m_old = m_ref[...]
        m_new = jnp.maximum(m_old, jnp.max(s, axis=-1, keepdims=True))
        m_safe = jnp.where(m_new == -jnp.inf, 0.0, m_new)
        alpha = jnp.exp(m_old - m_safe)
        pr = cnt * jnp.exp(s - m_safe)
        l_ref[...] = alpha * l_ref[...] + jnp.sum(pr, axis=-1, keepdims=True)
        acc_ref[...] = alpha * acc_ref[...] + _dot(pr.astype(BF16), v.astype(BF16))
        m_ref[...] = m_new

    @pl.when(p < n_tiles)
    def _():
        process(k_ref[...], v_ref[...], cnt_ref[...])

    @pl.when(p == n_tiles)
    def _():
        process(kn_ref[...], vn_ref[...], cntn_ref[...])
        o_ref[...] = acc_ref[...] / l_ref[...]


def _dil_counts(t, cb, n_keys, key0):
    r = jnp.arange(t * H_DIL, dtype=jnp.int32)[:, None]
    c = jnp.arange(n_keys * H_DIL, dtype=jnp.int32)[None, :]
    delta = cb + r // H_DIL - (key0 + c // H_DIL)
    cnt = jnp.zeros(delta.shape, F32)
    for window, d in DIL_PATTERNS:
        cnt = cnt + ((delta >= 0) & (delta % d == 0) & (delta <= window)).astype(F32)
    return jnp.where(r % H_DIL == c % H_DIL, cnt, 0.0).astype(BF16)


def _dil_decode(q, k_cache, v_cache, j, k_new, v_new, t):
    db, rows, hd = q.shape
    cb = k_cache.shape[2] // H_DIL
    tk = DIL_TK
    n_tiles = cb // tk
    nrow = k_new.shape[1]
    cnt = _dil_counts(t, cb, cb, 0)
    cnt_new = _dil_counts(t, cb, nrow // H_DIL, cb)
    seq_map = lambda b, p: (b, 0, 0)
    tile_map = lambda b, p: (j, b, jnp.minimum(p, n_tiles - 1), 0)
    return pl.pallas_call(
        functools.partial(_dil_decode_kernel, n_tiles=n_tiles),
        grid=(db, n_tiles + 1),
        in_specs=[pl.BlockSpec((None, rows, hd), seq_map),
                  pl.BlockSpec((None, None, tk * H_DIL, hd), tile_map),
                  pl.BlockSpec((None, None, tk * H_DIL, hd), tile_map),
                  pl.BlockSpec((rows, tk * H_DIL), lambda b, p: (0, jnp.minimum(p, n_tiles - 1))),
                  pl.BlockSpec((None, nrow, hd), seq_map),
                  pl.BlockSpec((None, nrow, hd), seq_map),
                  pl.BlockSpec((rows, nrow), lambda b, p: (0, 0))],
        out_specs=pl.BlockSpec((None, rows, hd), seq_map),
        out_shape=jax.ShapeDtypeStruct((db, rows, hd), F32),
        scratch_shapes=[pltpu.VMEM((rows, 1), F32), pltpu.VMEM((rows, 1), F32),
                        pltpu.VMEM((rows, hd), F32)],
        compiler_params=_params(("parallel", "arbitrary")),
        name="dil_decode",
    )(q, k_cache, v_cache, cnt, k_new, v_new, cnt_new)


def _pad_rows(a, n_seq, rows, pad):
    w = a.shape[-1]
    return jnp.pad(a.reshape(n_seq, rows, w), ((0, 0), (0, pad - rows), (0, 0)))


def _head_rows(a, n_seq, t, n_heads):
    return a.reshape(n_seq, t * n_heads, HEAD_DIM)


def kernel(x_prompt, x_sample, c_prompt, c_sample, page_table, cache_fox_k, cache_fox_v, cache_fox_logf, state_mlstm_c, state_mlstm_n, state_mlstm_m, cache_win_k, cache_win_v, w_ada, b_ada, norm_pre, norm_mix, norm_post, norm_final, ffn_pre_w13, ffn_pre_w2, ffn_post_w13, ffn_post_w2, ab_w_in, ab_w_out, fox_fgate_b, mlstm_igate_b, mlstm_fgate_b, mlstm_norm_g, c_w_in, c_w_out):
    nb, s, d = x_prompt.shape
    db, t, _ = x_sample.shape
    depth = w_ada.shape[0]
    n_ab, n_c = ab_w_in.shape[0], c_w_in.shape[0]
    ms = db * t
    fw = H_FOX * HEAD_DIM
    mw = H_MLSTM * HEAD_DIM
    dw = H_DIL * HEAD_DIM
    assert s % TM == 0 and ms % SUBLANES == 0 and t <= NEW_KEYS

    xp = x_prompt.reshape(nb * s, d)
    xs = x_sample.reshape(ms, d)

    pad_rows = -(ms + nb) % SUBLANES
    c_rows = jnp.concatenate([jnp.repeat(c_sample, t, axis=0), c_prompt, jnp.zeros((pad_rows, d), F32)], axis=0)
    mods_all = _adaln_all(c_rows, w_ada, b_ada)
    mods_p = mods_all[:, ms:ms + nb].reshape(depth * nb * N_MOD, 1, d)
    mods_s = mods_all[:, :ms]

    norm_pre3, norm_mix3, norm_post3 = (a.reshape(depth, 1, d) for a in (norm_pre, norm_mix, norm_post))

    o1, o2, o3 = 3 * fw, 3 * fw + H_FOX, 3 * fw + H_FOX + 3 * mw
    o4 = o3 + 2 * H_MLSTM
    ab_main = jnp.concatenate([ab_w_in[:, :, :o1], ab_w_in[:, :, o2:o3], ab_w_in[:, :, o4:]], axis=2)
    ab_gate = jnp.concatenate([ab_w_in[:, :, o1:o2], ab_w_in[:, :, o3:o4],
                               jnp.zeros((n_ab, d, LANES - H_FOX - 2 * H_MLSTM), F32)], axis=2)
    gate_bias = jnp.concatenate([fox_fgate_b, mlstm_igate_b, mlstm_fgate_b, jnp.zeros((n_ab, 8), F32)],
                                axis=1).reshape(n_ab, 32, 1)
    head_gain = mlstm_norm_g.reshape(n_ab, H_MLSTM, 1, HEAD_DIM)

    n_pool, page = cache_fox_k.shape[1], cache_fox_k.shape[2]
    prow = page * H_FOX
    k_pool = cache_fox_k.reshape(n_ab, n_pool, prow, HEAD_DIM)
    v_pool = cache_fox_v.reshape(n_ab, n_pool, prow, HEAD_DIM)
    lf_t = cache_fox_logf.transpose(0, 1, 3, 2).reshape(n_ab * n_pool * H_FOX, page)
    cl_t = _page_cumsum(lf_t).reshape(n_ab, n_pool, H_FOX, page)
    cl = cl_t.transpose(0, 1, 3, 2)
    tot = jnp.broadcast_to(cl[:, :, page - 1:page, :], cl.shape)
    cs_pool = jnp.stack([cl.reshape(n_ab, n_pool, prow), tot.reshape(n_ab, n_pool, prow)], axis=2)

    cb = cache_win_k.shape[2]
    k_win = cache_win_k.reshape(n_c, db, cb * H_DIL, HEAD_DIM)
    v_win = cache_win_v.reshape(n_c, db, cb * H_DIL, HEAD_DIM)

    zeros_c = jnp.zeros((nb, H_MLSTM, HEAD_DIM, HEAD_DIM), F32)
    zeros_n = jnp.zeros((nb, H_MLSTM, HEAD_DIM, 1), F32)
    zeros_m = jnp.zeros((nb, H_MLSTM, 1, 1), F32)

    outs = {name: [] for name in ("fl_p", "fk_s", "fv_s", "fl_s", "mc_p", "mn_p", "mm_p",
                                  "mc_s", "mn_s", "mm_s", "wk_new", "wv_new")}
    pp_ab = pp_c = None

    for l in range(depth):
        j = l // 2
        last = l == depth - 1
        mp = _Mods(mods_p, l, True, TM, nb, s)
        md = _Mods(mods_s, l, False, ms)

        xp, xs = _ffn(xp, xs, mp, md, (0, 1, 2), norm_pre3, ffn_pre_w13, ffn_pre_w2, l)

        if l % 2 == 0:
            pp_ab, ps, gp, gs = _proj_in(xp, xs, mp, md, (3, 4), norm_mix3, l, ab_main, ab_gate, j, j, n_ab, pp_ab)
            gt = _gates(gp, gate_bias[j], nb, s, s)
            gt4 = gt.reshape(nb, 32, 1, s)
            o_fox = _fox_prompt(pp_ab, j, gt4, nb, s)
            hn, c_p, n_p, m_p = _mlstm(pp_ab, j, gt4, zeros_c, zeros_n, zeros_m, head_gain, j, nb, s)
            outs["fl_p"].append(gt[:, 24:32, :].transpose(0, 2, 1))
            outs["mc_p"].append(c_p)
            outs["mn_p"].append(n_p.reshape(nb, H_MLSTM, HEAD_DIM))
            outs["mm_p"].append(m_p.reshape(nb, H_MLSTM))

            gts = _gates(_pad_rows(gs, db, t, CHUNK).reshape(db * CHUNK, LANES), gate_bias[j], db, CHUNK, t)
            gts4 = gts.reshape(db, 32, 1, CHUNK)
            nk = NEW_KEYS
            c_new = gts[:, 0:8, :nk].transpose(0, 2, 1).reshape(db, 1, nk * H_FOX)
            o_dec = _fox_decode(page_table, _head_rows(ps[:, :fw], db, t, H_FOX), k_pool, v_pool, cs_pool, j,
                                _pad_rows(_head_rows(ps[:, fw:2 * fw], db, t, H_FOX).reshape(db * t * H_FOX, HEAD_DIM),
                                          db, t * H_FOX, nk * H_FOX),
                                _pad_rows(_head_rows(ps[:, 2 * fw:3 * fw], db, t, H_FOX).reshape(db * t * H_FOX, HEAD_DIM),
                                          db, t * H_FOX, nk * H_FOX),
                                c_new)
            o_fox_s = o_dec.reshape(ms, fw)
            ps_pad = _pad_rows(ps, db, t, CHUNK).reshape(1, db * CHUNK, -1)
            hn_s, c_s, n_s, m_s = _mlstm(ps_pad, 0, gts4, state_mlstm_c[j],
                                         state_mlstm_n[j].reshape(db, H_MLSTM, HEAD_DIM, 1),
                                         state_mlstm_m[j].reshape(db, H_MLSTM, 1, 1),
                                         head_gain, j, db, CHUNK)
            hn_s = hn_s.reshape(db, CHUNK, mw)[:, :t].reshape(ms, mw)
            xp, xs = _proj_out(o_fox, 0, hn, 0, o_fox_s.astype(BF16), hn_s, ab_w_out, j, xp, xs, mp, md, 5)
            outs["fk_s"].append(ps[:, fw:2 * fw].reshape(db, t, H_FOX, HEAD_DIM))
            outs["fv_s"].append(ps[:, 2 * fw:3 * fw].reshape(db, t, H_FOX, HEAD_DIM))
            outs["fl_s"].append(gts[:, 24:32, :t].transpose(0, 2, 1))
            outs["mc_s"].append(c_s)
            outs["mn_s"].append(n_s.reshape(db, H_MLSTM, HEAD_DIM))
            outs["mm_s"].append(m_s.reshape(db, H_MLSTM))
        else:
            pp_c, ps = _proj_in(xp, xs, mp, md, (3, 4), norm_mix3, l, c_w_in, None, j, j, n_c, pp_c)
            o_dil = _dil_prompt(pp_c, j, nb, s)

            k_new = _head_rows(ps[:, dw:2 * dw], db, t, H_DIL)
            v_new = _head_rows(ps[:, 2 * dw:], db, t, H_DIL)
            nrow = NEW_KEYS * H_DIL
            o_dec = _dil_decode(_head_rows(ps[:, :dw], db, t, H_DIL), k_win, v_win, j,
                                jnp.pad(k_new, ((0, 0), (0, nrow - t * H_DIL), (0, 0))),
                                jnp.pad(v_new, ((0, 0), (0, nrow - t * H_DIL), (0, 0))), t)
            o_dil_s = o_dec.reshape(ms, dw)
            o_dil_s = o_dil_s.astype(BF16)
            xp, xs = _proj_out(o_dil, 0, o_dil, 1, o_dil_s, o_dil_s, c_w_out, j, xp, xs, mp, md, 5)
            outs["wk_new"].append(k_new.reshape(db, t, H_DIL, HEAD_DIM))
            outs["wv_new"].append(v_new.reshape(db, t, H_DIL, HEAD_DIM))

        fin = norm_final if last else None
        xp, xs = _ffn(xp, xs, mp, md, (6, 7, 8), norm_post3, ffn_post_w13, ffn_post_w2, l, fin)

    st = lambda name: jnp.stack(outs[name])
    keep = min(DIL_PATTERNS[-1][0], cb + t)
    wk_s = jnp.concatenate([cache_win_k, st("wk_new")], axis=2)[:, :, cb + t - keep:]
    wv_s = jnp.concatenate([cache_win_v, st("wv_new")], axis=2)[:, :, cb + t - keep:]
    keep_p = min(DIL_PATTERNS[-1][0], s)
    heads = lambda a, n_heads: a.reshape(a.shape[0], nb, s, n_heads, HEAD_DIM)
    return (xp.reshape(nb, s, d), xs.reshape(db, t, d),
            heads(pp_ab[:, :, fw:2 * fw], H_FOX), heads(pp_ab[:, :, 2 * fw:3 * fw], H_FOX),
            st("fl_p"), st("fk_s"), st("fv_s"), st("fl_s"),
            st("mc_p"), st("mn_p"), st("mm_p"), st("mc_s"), st("mn_s"), st("mm_s"),
            heads(pp_c[:, :, dw:2 * dw], H_DIL)[:, :, s - keep_p:], heads(pp_c[:, :, 2 * dw:], H_DIL)[:, :, s - keep_p:],
            wk_s, wv_s)
```

```python
import functools

import jax
import jax.numpy as jnp
from jax import lax
from jax.experimental import pallas as pl
from jax.experimental.pallas import tpu as pltpu

F32 = jnp.float32
BF16 = jnp.bfloat16

HEAD_DIM = 128
H_FOX = 8
H_MLSTM = 8
H_DIL = 16
N_MOD = 9
EPS = 1e-6
FFN_RES = 0.5
QK_SCALE = HEAD_DIM ** -0.5
DIL_PATTERNS = ((128, 1), (512, 4), (2048, 16))
CHUNK = 128
NEG = -1e30

LANES = 128
SUBLANES = 8
VMEM_BIG = 60 * 1024 * 1024
VMEM_MID = 48 * 1024 * 1024

TM = 1024
TF = 256
TN = 1024
TN_IN = 1024
ADALN_TN = 2304
ADALN_STREAMS = 4
TQ = 256
MLSTM_HEADS = 2
PAGES_PER_STEP = 16
DIL_TK = 512
NEW_KEYS = 16


def _params(sem, vmem=VMEM_MID):
    return pltpu.CompilerParams(dimension_semantics=sem, vmem_limit_bytes=vmem)


def _sigmoid(x):
    return 1.0 / (1.0 + jnp.exp(-x))


def _log_sigmoid(x):
    return jnp.minimum(x, 0.0) - jnp.log1p(jnp.exp(-jnp.abs(x)))


def _dot(a, b):
    return jnp.dot(a, b, preferred_element_type=F32)


def _dot_nt(a, b):
    return lax.dot_general(a, b, (((1,), (1,)), ((), ())), preferred_element_type=F32)


def _rms(x):
    return x * lax.rsqrt(jnp.mean(x * x, axis=-1, keepdims=True) + EPS)


def _norm_mod(x, g, shift, scale):
    return _rms(x) * g * (1.0 + scale) + shift


def _cumsum_lanes(x):
    lane = lax.broadcasted_iota(jnp.int32, x.shape, 1)
    sh = 1
    while sh < x.shape[1]:
        x = x + jnp.where(lane >= sh, pltpu.roll(x, sh, 1), 0.0)
        sh *= 2
    return x


class _Mods:
    def __init__(self, arr, layer, prompt, tm, n_seq=None, seq_rows=None):
        self.arr, self.layer, self.prompt, self.tm, self.n_seq, self.seq_rows = arr, layer, prompt, tm, n_seq, seq_rows

    def spec(self, k, tn=None):
        d = self.arr.shape[-1] if self.prompt else self.arr.shape[-1] // N_MOD
        l, tm = self.layer, self.tm
        if self.prompt:
            n_seq, seq_rows = self.n_seq, self.seq_rows
            row = lambda i: (l * n_seq + (i * tm) // seq_rows) * N_MOD + k
            if tn is None:
                return pl.BlockSpec((None, 1, d), lambda i, *_: (row(i), 0, 0))
            return pl.BlockSpec((None, 1, tn), lambda i, j: (row(i), 0, j))
        if tn is None:
            return pl.BlockSpec((None, tm, d), lambda i, *_: (l, 0, k))
        return pl.BlockSpec((None, tm, tn), lambda i, j: (l, 0, k * (d // tn) + j))


def _adaln_kernel(c_ref, *refs):
    w_refs, b_ref, o_ref = refs[:-2], refs[-2], refs[-1]
    c = c_ref[...]
    s = (c * _sigmoid(c)).astype(BF16)
    dk = w_refs[0].shape[0]
    acc = b_ref[...]
    for r, w_ref in enumerate(w_refs):
        acc = acc + _dot(s[:, r * dk:(r + 1) * dk], w_ref[...].astype(BF16))
    o_ref[...] = acc


def _adaln_all(c_rows, w_ada, b_ada):
    depth, d, n = w_ada.shape
    rows = c_rows.shape[0]
    tn = ADALN_TN
    ks = ADALN_STREAMS
    assert n % tn == 0 and d % ks == 0
    chunk = lambda r: pl.BlockSpec((None, d // ks, tn), lambda l, j: (l, r, j))
    return pl.pallas_call(
        _adaln_kernel,
        grid=(depth, n // tn),
        in_specs=[pl.BlockSpec((rows, d), lambda l, j: (0, 0))] + [chunk(r) for r in range(ks)]
        + [pl.BlockSpec((None, 1, tn), lambda l, j: (l, 0, j))],
        out_specs=pl.BlockSpec((None, rows, tn), lambda l, j: (l, 0, j)),
        out_shape=jax.ShapeDtypeStruct((depth, rows, n), F32),
        compiler_params=_params(("parallel", "parallel")),
        name="adaln",
    )(c_rows, *([w_ada] * ks), b_ada.reshape(depth, 1, n))


def _ffn_kernel(x_ref, sh_ref, sc_ref, gt_ref, xs_ref, shs_ref, scs_ref, gts_ref, g_ref, w13_hbm, w2_hbm,
                *rest, layer, d_ff, nf, tf, rem, final, tm, ni):
    if final:
        gf_ref, o_ref, os_ref, h_ref, w1_buf, w3_buf, w2_buf, sem = rest
    else:
        o_ref, os_ref, h_ref, w1_buf, w3_buf, w2_buf, sem = rest
    i = pl.program_id(0)

    def tile_copies(f, slot, width):
        c0 = f * tf if isinstance(f, int) else pl.multiple_of(f * tf, tf)
        return (pltpu.make_async_copy(w13_hbm.at[layer, :, pl.ds(c0, width)],
                                      w1_buf.at[slot, :, pl.ds(0, width)], sem.at[slot, 0]),
                pltpu.make_async_copy(w13_hbm.at[layer, :, pl.ds(d_ff + c0, width)],
                                      w3_buf.at[slot, :, pl.ds(0, width)], sem.at[slot, 1]),
                pltpu.make_async_copy(w2_hbm.at[layer, pl.ds(c0, width), :],
                                      w2_buf.at[slot, pl.ds(0, width), :], sem.at[slot, 2]))

    def start(f, slot, width):
        for c in tile_copies(f, slot, width):
            c.start()

    def wait(f, slot, width):
        for c in tile_copies(f, slot, width):
            c.wait()

    def finish(x, gate, acc):
        y = x + (FFN_RES * gate) * acc
        return _rms(y) * gf_ref[...] if final else y

    def run(rows):
        with_sample = rows > tm

        def part(w1, w3, w2):
            h = h_ref[:rows, :]
            a = _dot(h, w1.astype(BF16))
            g = _dot(h, w3.astype(BF16))
            u = (a * _sigmoid(a) * g).astype(BF16)
            return _dot(u, w2.astype(BF16))

        start(0, 0, tf)
        h_ref[:tm, :] = _norm_mod(x_ref[...], g_ref[...], sh_ref[...], sc_ref[...]).astype(BF16)
        if with_sample:
            h_ref[tm:, :] = _norm_mod(xs_ref[...], g_ref[...], shs_ref[...], scs_ref[...]).astype(BF16)

        wait(0, 0, tf)
        start(1, 1, tf)
        p = part(w1_buf[0], w3_buf[0], w2_buf[0])
        o_ref[...] = p[:tm]
        if with_sample:
            os_ref[...] = p[tm:]

        def body(f, carry):
            slot = lax.rem(f, 2)
            wait(f, slot, tf)

            @pl.when(f + 1 < nf - 1)
            def _():
                start(f + 1, 1 - slot, tf)

            @pl.when(f + 1 == nf - 1)
            def _():
                start(nf - 1, 1 - slot, rem)

            p = part(w1_buf[slot], w3_buf[slot], w2_buf[slot])
            o_ref[...] += p[:tm]
            if with_sample:
                os_ref[...] += p[tm:]
            return carry

        lax.fori_loop(1, nf - 1, body, 0)

        slot = (nf - 1) % 2
        wait(nf - 1, slot, rem)
        p = part(w1_buf[slot, :, :rem], w3_buf[slot, :, :rem], w2_buf[slot, :rem, :])
        o_ref[...] = finish(x_ref[...], gt_ref[...], o_ref[...] + p[:tm])
        if with_sample:
            os_ref[...] = finish(xs_ref[...], gts_ref[...], os_ref[...] + p[tm:])

    @pl.when(i < ni - 1)
    def _():
        run(tm)

    @pl.when(i == ni - 1)
    def _():
        run(h_ref.shape[0])


def _ffn(x, xs, mods, mods_s, ks, g_all, w13_all, w2_all, layer, final_g=None):
    m, d = x.shape
    ms = xs.shape[0]
    tm = mods.tm
    d_ff = w2_all.shape[1]
    tf = TF
    nf = pl.cdiv(d_ff, tf)
    rem = d_ff - (nf - 1) * tf
    ni = m // tm
    assert nf >= 3 and m % tm == 0 and d_ff % LANES == 0 and ni >= 2 and mods_s.tm == ms
    final = final_g is not None
    in_specs = [pl.BlockSpec((tm, d), lambda i: (i, 0)),
                mods.spec(ks[0]), mods.spec(ks[1]), mods.spec(ks[2]),
                pl.BlockSpec((ms, d), lambda i: (0, 0)),
                mods_s.spec(ks[0]), mods_s.spec(ks[1]), mods_s.spec(ks[2]),
                pl.BlockSpec((None, 1, d), lambda i: (layer, 0, 0)),
                pl.BlockSpec(memory_space=pl.ANY),
                pl.BlockSpec(memory_space=pl.ANY)]
    args = [x, mods.arr, mods.arr, mods.arr, xs, mods_s.arr, mods_s.arr, mods_s.arr, g_all, w13_all, w2_all]
    if final:
        in_specs.append(pl.BlockSpec((1, d), lambda i: (0, 0)))
        args.append(final_g.reshape(1, d))
    return pl.pallas_call(
        functools.partial(_ffn_kernel, layer=layer, d_ff=d_ff, nf=nf, tf=tf, rem=rem, final=final, tm=tm, ni=ni),
        grid=(ni,),
        in_specs=in_specs,
        out_specs=[pl.BlockSpec((tm, d), lambda i: (i, 0)), pl.BlockSpec((ms, d), lambda i: (0, 0))],
        out_shape=[jax.ShapeDtypeStruct((m, d), F32), jax.ShapeDtypeStruct((ms, d), F32)],
        scratch_shapes=[pltpu.VMEM((tm + ms, d), BF16),
                        pltpu.VMEM((2, d, tf), F32), pltpu.VMEM((2, d, tf), F32), pltpu.VMEM((2, tf, d), F32),
                        pltpu.SemaphoreType.DMA((2, 3))],
        compiler_params=_params(("arbitrary",), VMEM_BIG),
        name="ffn",
    )(*args)


def _proj_in_kernel(x_ref, sh_ref, sc_ref, xs_ref, shs_ref, scs_ref, g_ref, w_ref, *rest, gates, aliased, tm, ni):
    rest = list(rest)
    wg_ref = rest.pop(0) if gates else None
    if aliased:
        rest.pop(0)
    if gates:
        o_ref, os_ref, og_ref, ogs_ref, h_ref = rest
    else:
        o_ref, os_ref, h_ref = rest
    i = pl.program_id(0)
    n = pl.program_id(1)

    @pl.when(n == 0)
    def _():
        h = _norm_mod(x_ref[...], g_ref[...], sh_ref[...], sc_ref[...]).astype(BF16)
        h_ref[:tm, :] = h
        if gates:
            og_ref[...] = _dot(h, wg_ref[...].astype(BF16))

    @pl.when(jnp.logical_and(n == 0, i == ni - 1))
    def _():
        hs = _norm_mod(xs_ref[...], g_ref[...], shs_ref[...], scs_ref[...]).astype(BF16)
        h_ref[tm:, :] = hs
        if gates:
            ogs_ref[...] = _dot(hs, wg_ref[...].astype(BF16))

    @pl.when(i < ni - 1)
    def _():
        o_ref[...] = _dot(h_ref[:tm, :], w_ref[...].astype(BF16))

    @pl.when(i == ni - 1)
    def _():
        y = _dot(h_ref[...], w_ref[...].astype(BF16))
        o_ref[...] = y[:tm]
        os_ref[...] = y[tm:]


def _proj_in(x, xs, mods, mods_s, ks, g_all, layer, w_all, wg_all, j, slot=0, n_slots=1, stacked=None):
    m, d = x.shape
    ms = xs.shape[0]
    tm = mods.tm
    n = w_all.shape[2]
    tn = TN_IN
    ni = m // tm
    assert n % tn == 0 and m % tm == 0 and (stacked is None) == (slot == 0) and ni >= 2 and mods_s.tm == ms
    gates = wg_all is not None
    aliased = stacked is not None
    s_col = lambda i, c: jnp.where(i == ni - 1, c, 0)
    in_specs = [pl.BlockSpec((tm, d), lambda i, c: (i, 0)), mods.spec(ks[0]), mods.spec(ks[1]),
                pl.BlockSpec((ms, d), lambda i, c: (0, 0)), mods_s.spec(ks[0]), mods_s.spec(ks[1]),
                pl.BlockSpec((None, 1, d), lambda i, c: (layer, 0, 0)),
                pl.BlockSpec((None, d, tn), lambda i, c: (j, 0, c))]
    out_specs = [pl.BlockSpec((None, tm, tn), lambda i, c: (slot, i, c)),
                 pl.BlockSpec((ms, tn), lambda i, c: (0, s_col(i, c)))]
    out_shape = [jax.ShapeDtypeStruct((n_slots, m, n), F32), jax.ShapeDtypeStruct((ms, n), F32)]
    args = [x, mods.arr, mods.arr, xs, mods_s.arr, mods_s.arr, g_all, w_all]
    if gates:
        in_specs.append(pl.BlockSpec((None, d, LANES), lambda i, c: (j, 0, 0)))
        out_specs += [pl.BlockSpec((tm, LANES), lambda i, c: (i, 0)), pl.BlockSpec((ms, LANES), lambda i, c: (0, 0))]
        out_shape += [jax.ShapeDtypeStruct((m, LANES), F32), jax.ShapeDtypeStruct((ms, LANES), F32)]
        args.append(wg_all)
    if aliased:
        in_specs.append(pl.BlockSpec(memory_space=pl.ANY))
        args.append(stacked)
    return pl.pallas_call(
        functools.partial(_proj_in_kernel, gates=gates, aliased=aliased, tm=tm, ni=ni),
        grid=(ni, n // tn),
        in_specs=in_specs, out_specs=out_specs, out_shape=out_shape,
        input_output_aliases={len(args) - 1: 0} if aliased else {},
        scratch_shapes=[pltpu.VMEM((tm + ms, d), BF16)],
        compiler_params=_params(("arbitrary", "arbitrary"), VMEM_BIG),
        name="proj_in",
    )(*args)


def _proj_out_kernel(a_ref, b_ref, as_ref, bs_ref, wa_ref, wb_ref, x_ref, gt_ref, xs_ref, gts_ref, o_ref, os_ref,
                     *, ni):
    wa = wa_ref[...].astype(BF16)
    wb = wb_ref[...].astype(BF16)
    o_ref[...] = x_ref[...] + gt_ref[...] * (_dot(a_ref[...], wa) + _dot(b_ref[...], wb))

    @pl.when(pl.program_id(0) == ni - 1)
    def _():
        os_ref[...] = xs_ref[...] + gts_ref[...] * (_dot(as_ref[...], wa) + _dot(bs_ref[...], wb))


def _proj_out(mix_a, a_blk, mix_b, b_blk, mix_as, mix_bs, w_all, j, x, xs, mods, mods_s, k):
    m, d = x.shape
    ms = xs.shape[0]
    tm = mods.tm
    kh = w_all.shape[1] // 2
    tn = TN
    ni = m // tm
    assert mods_s.tm == ms and ni >= 2
    s_col = lambda i, c: jnp.where(i == ni - 1, c, 0)
    layer = mods_s.layer
    return pl.pallas_call(
        functools.partial(_proj_out_kernel, ni=ni),
        grid=(ni, d // tn),
        in_specs=[pl.BlockSpec((tm, kh), lambda i, c: (i, a_blk)),
                  pl.BlockSpec((tm, kh), lambda i, c: (i, b_blk)),
                  pl.BlockSpec((ms, kh), lambda i, c: (0, a_blk)),
                  pl.BlockSpec((ms, kh), lambda i, c: (0, b_blk)),
                  pl.BlockSpec((None, kh, tn), lambda i, c: (j, 0, c)),
                  pl.BlockSpec((None, kh, tn), lambda i, c: (j, 1, c)),
                  pl.BlockSpec((tm, tn), lambda i, c: (i, c)),
                  mods.spec(k, tn),
                  pl.BlockSpec((ms, tn), lambda i, c: (0, s_col(i, c))),
                  pl.BlockSpec((None, ms, tn), lambda i, c: (layer, 0, k * (d // tn) + s_col(i, c)))],
        out_specs=[pl.BlockSpec((tm, tn), lambda i, c: (i, c)),
                   pl.BlockSpec((ms, tn), lambda i, c: (0, s_col(i, c)))],
        out_shape=[jax.ShapeDtypeStruct((m, d), F32), jax.ShapeDtypeStruct((ms, d), F32)],
        compiler_params=_params(("arbitrary", "arbitrary"), VMEM_BIG),
        name="proj_out",
    )(mix_a, mix_b, mix_as, mix_bs, w_all, w_all, x, mods.arr, xs, mods_s.arr)


def _gates_kernel(p_ref, b_ref, o_ref, *, n_valid):
    s = p_ref.shape[0]
    lane = lax.broadcasted_iota(jnp.int32, (8, LANES), 1)
    carry = jnp.zeros((8, 1), F32)
    for blk in range(s // LANES):
        sl = slice(blk * LANES, (blk + 1) * LANES)
        pre = p_ref[sl, :].T[0:32] + b_ref[...]
        valid = (lane + blk * LANES) < n_valid
        lf_f = jnp.where(valid, _log_sigmoid(pre[0:8]), 0.0)
        ig = jnp.where(valid, pre[8:16], NEG)
        lf_m = jnp.where(valid, _log_sigmoid(pre[16:24]), 0.0)
        c_f = _cumsum_lanes(lf_f) + carry
        carry = c_f[:, LANES - 1:LANES]
        o_ref[0:8, sl] = c_f
        o_ref[8:16, sl] = ig
        o_ref[16:24, sl] = _cumsum_lanes(lf_m)
        o_ref[24:32, sl] = lf_f


def _gates(pre, bias, n_seq, s, n_valid):
    return pl.pallas_call(
        functools.partial(_gates_kernel, n_valid=n_valid),
        grid=(n_seq,),
        in_specs=[pl.BlockSpec((s, LANES), lambda b: (b, 0)),
                  pl.BlockSpec((32, 1), lambda b: (0, 0))],
        out_specs=pl.BlockSpec((None, 32, s), lambda b: (b, 0, 0)),
        out_shape=jax.ShapeDtypeStruct((n_seq, 32, s), F32),
        compiler_params=_params(("parallel",)),
        name="gates",
    )(pre, bias)


def _fox_kernel(q_ref, k_ref, v_ref, c_ref, o_ref, kb_ref, vb_ref, *, tq):
    s = q_ref.shape[0]
    kb_ref[...] = k_ref[...].astype(BF16)
    vb_ref[...] = v_ref[...].astype(BF16)
    row = lax.broadcasted_iota(jnp.int32, (tq, tq), 0)
    col = lax.broadcasted_iota(jnp.int32, (tq, tq), 1)
    for qi in range(s // tq):
        q = (q_ref[qi * tq:(qi + 1) * tq, :] * QK_SCALE).astype(BF16)
        m = l = acc = None
        for kb in range(qi + 1):
            ks = slice(kb * tq, (kb + 1) * tq)
            sc = _dot_nt(q, kb_ref[ks, :]) - c_ref[:, ks]
            if kb == qi:
                sc = jnp.where(col <= row, sc, -jnp.inf)
            bm = jnp.max(sc, axis=-1, keepdims=True)
            if m is None:
                m = bm
                p = jnp.exp(sc - m)
                l = jnp.sum(p, axis=-1, keepdims=True)
                acc = _dot(p.astype(BF16), vb_ref[ks, :])
            else:
                m_new = jnp.maximum(m, bm)
                alpha = jnp.exp(m - m_new)
                p = jnp.exp(sc - m_new)
                l = alpha * l + jnp.sum(p, axis=-1, keepdims=True)
                acc = alpha * acc + _dot(p.astype(BF16), vb_ref[ks, :])
                m = m_new
        o_ref[qi * tq:(qi + 1) * tq, :] = (acc / l).astype(o_ref.dtype)


def _fox_prompt(proj, slot, gt4, n_seq, s):
    col = lambda base: pl.BlockSpec((None, s, HEAD_DIM), lambda b, h: (slot, b, base + h))
    return pl.pallas_call(
        functools.partial(_fox_kernel, tq=TQ),
        grid=(n_seq, H_FOX),
        in_specs=[col(0), col(H_FOX), col(2 * H_FOX),
                  pl.BlockSpec((None, None, 1, s), lambda b, h: (b, h, 0, 0))],
        out_specs=pl.BlockSpec((s, HEAD_DIM), lambda b, h: (b, h)),
        out_shape=jax.ShapeDtypeStruct((n_seq * s, H_FOX * HEAD_DIM), BF16),
        scratch_shapes=[pltpu.VMEM((s, HEAD_DIM), BF16), pltpu.VMEM((s, HEAD_DIM), BF16)],
        compiler_params=_params(("parallel", "parallel")),
        name="fox_prompt",
    )(proj, proj, proj, gt4)


def _page_cumsum_kernel(x_ref, o_ref):
    o_ref[...] = _cumsum_lanes(x_ref[...])


def _page_cumsum(lf_t):
    rows, page = lf_t.shape
    tr = 1024
    assert rows % tr == 0 and page == LANES
    return pl.pallas_call(
        _page_cumsum_kernel,
        grid=(rows // tr,),
        in_specs=[pl.BlockSpec((tr, page), lambda i: (i, 0))],
        out_specs=pl.BlockSpec((tr, page), lambda i: (i, 0)),
        out_shape=jax.ShapeDtypeStruct((rows, page), F32),
        compiler_params=_params(("parallel",)),
        name="page_cumsum",
    )(lf_t)


def _fox_decode_kernel(pt_ref, q_ref, *refs, g, n_steps):
    del pt_ref
    k_refs, v_refs, cs_refs = refs[:g], refs[g:2 * g], refs[2 * g:3 * g]
    kn_ref, vn_ref, cn_ref, o_ref, m_ref, l_ref, acc_ref, carry_ref = refs[3 * g:]
    p = pl.program_id(1)
    rows = q_ref.shape[0]
    hbits = H_FOX.bit_length() - 1

    @pl.when(p == 0)
    def _():
        m_ref[...] = jnp.full(m_ref.shape, -jnp.inf, F32)
        l_ref[...] = jnp.zeros(l_ref.shape, F32)
        acc_ref[...] = jnp.zeros(acc_ref.shape, F32)
        carry_ref[...] = jnp.zeros(carry_ref.shape, F32)

    def iotas(ncols):
        return (lax.broadcasted_iota(jnp.int32, (rows, ncols), 0),
                lax.broadcasted_iota(jnp.int32, (rows, ncols), 1))

    def attend(blocks, mask):
        q = (q_ref[...] * QK_SCALE).astype(BF16)
        scores = [jnp.where(mask, _dot_nt(q, k.astype(BF16)) - bias, -jnp.inf) for k, _, bias in blocks]
        m_old = m_ref[...]
        m_new = m_old
        for sc in scores:
            m_new = jnp.maximum(m_new, jnp.max(sc, axis=-1, keepdims=True))
        alpha = jnp.exp(m_old - m_new)
        l = alpha * l_ref[...]
        acc = alpha * acc_ref[...]
        for sc, (_, v, _) in zip(scores, blocks):
            pr = jnp.exp(sc - m_new)
            l = l + jnp.sum(pr, axis=-1, keepdims=True)
            acc = acc + _dot(pr.astype(BF16), v.astype(BF16))
        m_ref[...] = m_new
        l_ref[...] = l
        acc_ref[...] = acc

    @pl.when(p < n_steps)
    def _():
        r, c = iotas(k_refs[0].shape[0])
        mask = (r & (H_FOX - 1)) == (c & (H_FOX - 1))
        carry = carry_ref[...]
        blocks = []
        for i in range(g):
            cs = cs_refs[i][...]
            blocks.append((k_refs[i][...], v_refs[i][...], carry + cs[0:1]))
            carry = carry + cs[1:2]
        carry_ref[...] = carry
        attend(blocks, mask)

    @pl.when(p == n_steps)
    def _():
        ncol = kn_ref.shape[0]
        r, c = iotas(ncol)
        same_head = (r & (H_FOX - 1)) == (c & (H_FOX - 1))
        causal = lax.shift_right_logical(c, hbits) <= lax.shift_right_logical(r, hbits)
        bias = carry_ref[:, :ncol] + cn_ref[...]
        attend([(kn_ref[...], vn_ref[...], bias)], jnp.logical_and(same_head, causal))
        o_ref[...] = acc_ref[...] / l_ref[...]


def _fox_decode(page_table, q, k_pool, v_pool, cs_pool, j, k_new, v_new, c_new):
    db, rows, hd = q.shape
    n_pages = page_table.shape[1]
    prow = k_pool.shape[2]
    g = PAGES_PER_STEP
    assert n_pages % g == 0
    n_steps = n_pages // g
    nrow = k_new.shape[1]

    def pool_map(i):
        return lambda b, p, pt: (j, pt[b, jnp.minimum(p * g + i, n_pages - 1)], 0, 0)

    seq_map = lambda b, p, pt: (b, 0, 0)
    in_specs = [pl.BlockSpec((None, rows, hd), seq_map)]
    in_specs += [pl.BlockSpec((None, None, prow, hd), pool_map(i)) for i in range(g)]
    in_specs += [pl.BlockSpec((None, None, prow, hd), pool_map(i)) for i in range(g)]
    in_specs += [pl.BlockSpec((None, None, 2, prow), pool_map(i)) for i in range(g)]
    in_specs += [pl.BlockSpec((None, nrow, hd), seq_map), pl.BlockSpec((None, nrow, hd), seq_map),
                 pl.BlockSpec((None, 1, nrow), seq_map)]
    grid_spec = pltpu.PrefetchScalarGridSpec(
        num_scalar_prefetch=1,
        grid=(db, n_steps + 1),
        in_specs=in_specs,
        out_specs=pl.BlockSpec((None, rows, hd), seq_map),
        scratch_shapes=[pltpu.VMEM((rows, 1), F32), pltpu.VMEM((rows, 1), F32),
                        pltpu.VMEM((rows, hd), F32), pltpu.VMEM((1, prow), F32)])
    return pl.pallas_call(
        functools.partial(_fox_decode_kernel, g=g, n_steps=n_steps),
        grid_spec=grid_spec,
        out_shape=jax.ShapeDtypeStruct((db, rows, hd), F32),
        compiler_params=_params(("parallel", "arbitrary")),
        name="fox_decode",
    )(page_table, q, *([k_pool] * g), *([v_pool] * g), *([cs_pool] * g), k_new, v_new, c_new)


def _mlstm_kernel(q_ref, k_ref, v_ref, og_ref, *refs, nh):
    ig_refs, b_refs = refs[:nh], refs[nh:2 * nh]
    c0_ref, n0_ref, m0_ref, g_ref, hn_ref, c_ref, n_ref, m_ref = refs[2 * nh:]
    s = q_ref.shape[0]
    ln = CHUNK
    row = lax.broadcasted_iota(jnp.int32, (ln, ln), 0)
    col = lax.broadcasted_iota(jnp.int32, (ln, ln), 1)
    causal = col <= row

    last_lane = lax.broadcasted_iota(jnp.int32, (1, ln), 1) == ln - 1
    assert ln == HEAD_DIM == LANES
    for hh in range(nh):
        hs = slice(hh * HEAD_DIM, (hh + 1) * HEAD_DIM)
        c_st = c0_ref[hh]
        n_st = jnp.broadcast_to(n0_ref[hh], (HEAD_DIM, LANES))
        m_st = jnp.broadcast_to(m0_ref[hh], (1, LANES))
        for ci in range(s // ln):
            ts = slice(ci * ln, (ci + 1) * ln)
            qb = q_ref[ts, hs].astype(BF16)
            k_t = (k_ref[ts, hs] * QK_SCALE).T
            vb = v_ref[ts, hs].astype(BF16)
            b_row = b_refs[hh][:, ts]
            ig_row = ig_refs[hh][:, ts]
            b_mat = jnp.broadcast_to(b_row, (ln, ln))
            b_col = b_mat.T
            dmat = jnp.where(causal, b_col - b_mat + ig_row, NEG)
            inter = b_col + m_st
            m_tok = jnp.maximum(inter, jnp.max(dmat, axis=-1, keepdims=True))
            w_inter = jnp.exp(inter - m_tok)
            qk = _dot(qb, k_t.astype(BF16)) * jnp.exp(dmat - m_tok)
            num = w_inter * _dot(qb, c_st.astype(BF16)) + _dot(qk.astype(BF16), vb)
            den = w_inter * _dot(qb, n_st.astype(BF16)) + jnp.sum(qk, axis=-1, keepdims=True)
            h = num / jnp.maximum(jnp.abs(den), jnp.exp(-m_tok))
            hn_ref[ts, hs] = (_rms(h) * g_ref[hh] * _sigmoid(og_ref[ts, hs])).astype(hn_ref.dtype)
            b_last = jnp.sum(jnp.where(last_lane, b_row, 0.0), axis=-1, keepdims=True)
            g_row = b_last - b_row + ig_row
            m_new = jnp.maximum(b_last + m_st, jnp.max(g_row, axis=-1, keepdims=True))
            a_prev = jnp.exp(b_last + m_st - m_new)
            kw = k_t * jnp.exp(g_row - m_new)
            c_st = a_prev * c_st + _dot(kw.astype(BF16), vb)
            n_st = a_prev * n_st + jnp.sum(kw, axis=-1, keepdims=True)
            m_st = m_new
        c_ref[hh] = c_st
        n_ref[hh] = n_st[:, 0:1]
        m_ref[hh] = m_st[:, 0:1]


def _mlstm(proj, slot, gt4, c0, n0, m0, g_all, j, n_seq, s):
    h, nh = H_MLSTM, MLSTM_HEADS
    w = nh * HEAD_DIM
    col = lambda base: pl.BlockSpec((None, s, w), lambda b, c: (slot, b, base // nh + c))
    row = lambda base, hh: pl.BlockSpec((None, None, 1, s), lambda b, c: (b, base + c * nh + hh, 0, 0))
    st = lambda shape: pl.BlockSpec((None, nh) + shape, lambda b, c: (b, c, 0, 0))
    return pl.pallas_call(
        functools.partial(_mlstm_kernel, nh=nh),
        grid=(n_seq, h // nh),
        in_specs=[col(24), col(32), col(40), col(48)]
        + [row(8, hh) for hh in range(nh)] + [row(16, hh) for hh in range(nh)]
        + [st((HEAD_DIM, HEAD_DIM)), st((HEAD_DIM, 1)), st((1, 1)),
           pl.BlockSpec((None, nh, 1, HEAD_DIM), lambda b, c: (j, c, 0, 0))],
        out_specs=[pl.BlockSpec((s, w), lambda b, c: (b, c)),
                   st((HEAD_DIM, HEAD_DIM)), st((HEAD_DIM, 1)), st((1, 1))],
        out_shape=[jax.ShapeDtypeStruct((n_seq * s, h * HEAD_DIM), BF16),
                   jax.ShapeDtypeStruct((n_seq, h, HEAD_DIM, HEAD_DIM), F32),
                   jax.ShapeDtypeStruct((n_seq, h, HEAD_DIM, 1), F32),
                   jax.ShapeDtypeStruct((n_seq, h, 1, 1), F32)],
        compiler_params=_params(("parallel", "parallel")),
        name="mlstm",
    )(proj, proj, proj, proj, *([gt4] * (2 * nh)), c0, n0, m0, g_all)


def _dil_kernel(q_ref, k_ref, v_ref, o_ref, m_ref, l_ref, acc_ref):
    s = q_ref.shape[0]
    blk = CHUNK
    row = lax.broadcasted_iota(jnp.int32, (blk, blk), 0)
    col = lax.broadcasted_iota(jnp.int32, (blk, blk), 1)
    cur_mask = col <= row
    dist = blk + lax.broadcasted_iota(jnp.int32, (blk, 2 * blk), 0) - lax.broadcasted_iota(jnp.int32, (blk, 2 * blk), 1)
    band_mask = jnp.logical_and(dist >= 0, dist <= blk)

    def rows(start, size, d):
        return pl.ds(start, size) if d == 1 else pl.ds(start, size, stride=d)

    for pi, (window, d) in enumerate(DIL_PATTERNS):
        assert window // d == blk and s % (d * blk) == 0
        for r in range(d):
            for n in range(s // (d * blk)):
                start = r + d * blk * n
                q = (q_ref[rows(start, blk, d), :] * QK_SCALE).astype(BF16)
                if n == 0:
                    kv_rows, mask = rows(start, blk, d), cur_mask
                else:
                    kv_rows, mask = rows(start - d * blk, 2 * blk, d), band_mask
                sc = _dot_nt(q, k_ref[kv_rows, :].astype(BF16))
                sc = jnp.where(mask, sc, -jnp.inf)
                m = jnp.max(sc, axis=-1, keepdims=True)
                p = jnp.exp(sc - m)
                out_rows = rows(pi * s + start, blk, d)
                m_ref[out_rows, :] = jnp.broadcast_to(m, (blk, LANES))
                l_ref[out_rows, :] = jnp.broadcast_to(jnp.sum(p, axis=-1, keepdims=True), (blk, LANES))
                acc_ref[out_rows, :] = _dot(p.astype(BF16), v_ref[kv_rows, :].astype(BF16))

    n_pat = len(DIL_PATTERNS)
    ms = [m_ref[pi * s:(pi + 1) * s, :] for pi in range(n_pat)]
    m_all = functools.reduce(jnp.maximum, ms)
    den = jnp.zeros((s, LANES), F32)
    num = jnp.zeros((s, HEAD_DIM), F32)
    for pi in range(n_pat):
        w = jnp.exp(ms[pi] - m_all)
        den = den + w * l_ref[pi * s:(pi + 1) * s, :]
        num = num + w * acc_ref[pi * s:(pi + 1) * s, :]
    o_ref[...] = (num / den).astype(o_ref.dtype)


def _dil_prompt(proj, slot, n_seq, s):
    h = H_DIL
    n_pat = len(DIL_PATTERNS)
    col = lambda base: pl.BlockSpec((None, s, HEAD_DIM), lambda b, c: (slot, b, base + c))
    return pl.pallas_call(
        _dil_kernel,
        grid=(n_seq, h),
        in_specs=[col(0), col(h), col(2 * h)],
        out_specs=pl.BlockSpec((s, HEAD_DIM), lambda b, c: (b, c)),
        out_shape=jax.ShapeDtypeStruct((n_seq * s, h * HEAD_DIM), BF16),
        scratch_shapes=[pltpu.VMEM((n_pat * s, LANES), F32), pltpu.VMEM((n_pat * s, LANES), F32),
                        pltpu.VMEM((n_pat * s, HEAD_DIM), F32)],
        compiler_params=_params(("parallel", "parallel")),
        name="dil_prompt",
    )(proj, proj, proj)


def _dil_decode_kernel(q_ref, k_ref, v_ref, cnt_ref, kn_ref, vn_ref, cntn_ref, o_ref, m_ref, l_ref, acc_ref,
                       *, n_tiles):
    p = pl.program_id(1)

    @pl.when(p == 0)
    def _():
        m_ref[...] = jnp.full(m_ref.shape, -jnp.inf, F32)
        l_ref[...] = jnp.zeros(l_ref.shape, F32)
        acc_ref[...] = jnp.zeros(acc_ref.shape, F32)

    def process(k, v, cnt):
        cnt = cnt.astype(F32)
        q = (q_ref[...] * QK_SCALE).astype(BF16)
        s = jnp.where(cnt > 0.0, _dot_nt(q, k.astype(BF16)), -jnp.inf)
        m_old = m_ref[...]
        m_new = jnp.maximum(m_old, jnp.max(s, axis=-1, keepdims=True))
        m_safe = jnp.where(m_new == -jnp.inf, 0.0, m_new)
        alpha = jnp.exp(m_old - m_safe)
        pr = cnt * jnp.exp(s - m_safe)
        l_ref[...] = alpha * l_ref[...] + jnp.sum(pr, axis=-1, keepdims=True)
        acc_ref[...] = alpha * acc_ref[...] + _dot(pr.astype(BF16), v.astype(BF16))
        m_ref[...] = m_new

    @pl.when(p < n_tiles)
    def _():
        process(k_ref[...], v_ref[...], cnt_ref[...])

    @pl.when(p == n_tiles)
    def _():
        process(kn_ref[...], vn_ref[...], cntn_ref[...])
        o_ref[...] = acc_ref[...] / l_ref[...]


def _dil_counts(t, cb, n_keys, key0):
    r = jnp.arange(t * H_DIL, dtype=jnp.int32)[:, None]
    c = jnp.arange(n_keys * H_DIL, dtype=jnp.int32)[None, :]
    delta = cb + r // H_DIL - (key0 + c // H_DIL)
    cnt = jnp.zeros(delta.shape, F32)
    for window, d in DIL_PATTERNS:
        cnt = cnt + ((delta >= 0) & (delta % d == 0) & (delta <= window)).astype(F32)
    return jnp.where(r % H_DIL == c % H_DIL, cnt, 0.0).astype(BF16)


def _dil_decode(q, k_cache, v_cache, j, k_new, v_new, t):
    db, rows, hd = q.shape
    cb = k_cache.shape[2] // H_DIL
    tk = DIL_TK
    n_tiles = cb // tk
    nrow = k_new.shape[1]
    cnt = _dil_counts(t, cb, cb, 0)
    cnt_new = _dil_counts(t, cb, nrow // H_DIL, cb)
    seq_map = lambda b, p: (b, 0, 0)
    tile_map = lambda b, p: (j, b, jnp.minimum(p, n_tiles - 1), 0)
    return pl.pallas_call(
        functools.partial(_dil_decode_kernel, n_tiles=n_tiles),
        grid=(db, n_tiles + 1),
        in_specs=[pl.BlockSpec((None, rows, hd), seq_map),
                  pl.BlockSpec((None, None, tk * H_DIL, hd), tile_map),
                  pl.BlockSpec((None, None, tk * H_DIL, hd), tile_map),
                  pl.BlockSpec((rows, tk * H_DIL), lambda b, p: (0, jnp.minimum(p, n_tiles - 1))),
                  pl.BlockSpec((None, nrow, hd), seq_map),
                  pl.BlockSpec((None, nrow, hd), seq_map),
                  pl.BlockSpec((rows, nrow), lambda b, p: (0, 0))],
        out_specs=pl.BlockSpec((None, rows, hd), seq_map),
        out_shape=jax.ShapeDtypeStruct((db, rows, hd), F32),
        scratch_shapes=[pltpu.VMEM((rows, 1), F32), pltpu.VMEM((rows, 1), F32),
                        pltpu.VMEM((rows, hd), F32)],
        compiler_params=_params(("parallel", "arbitrary")),
        name="dil_decode",
    )(q, k_cache, v_cache, cnt, k_new, v_new, cnt_new)


def _pad_rows(a, n_seq, rows, pad):
    w = a.shape[-1]
    return jnp.pad(a.reshape(n_seq, rows, w), ((0, 0), (0, pad - rows), (0, 0)))


def _head_rows(a, n_seq, t, n_heads):
    return a.reshape(n_seq, t * n_heads, HEAD_DIM)


def kernel(x_prompt, x_sample, c_prompt, c_sample, page_table, cache_fox_k, cache_fox_v, cache_fox_logf, state_mlstm_c, state_mlstm_n, state_mlstm_m, cache_win_k, cache_win_v, w_ada, b_ada, norm_pre, norm_mix, norm_post, norm_final, ffn_pre_w13, ffn_pre_w2, ffn_post_w13, ffn_post_w2, ab_w_in, ab_w_out, fox_fgate_b, mlstm_igate_b, mlstm_fgate_b, mlstm_norm_g, c_w_in, c_w_out):
    nb, s, d = x_prompt.shape
    db, t, _ = x_sample.shape
    depth = w_ada.shape[0]
    n_ab, n_c = ab_w_in.shape[0], c_w_in.shape[0]
    ms = db * t
    fw = H_FOX * HEAD_DIM
    mw = H_MLSTM * HEAD_DIM
    dw = H_DIL * HEAD_DIM
    assert s % TM == 0 and ms % SUBLANES == 0 and t <= NEW_KEYS

    xp = x_prompt.reshape(nb * s, d)
    xs = x_sample.reshape(ms, d)

    pad_rows = -(ms + nb) % SUBLANES
    c_rows = jnp.concatenate([jnp.repeat(c_sample, t, axis=0), c_prompt, jnp.zeros((pad_rows, d), F32)], axis=0)
    mods_all = _adaln_all(c_rows, w_ada, b_ada)
    mods_p = mods_all[:, ms:ms + nb].reshape(depth * nb * N_MOD, 1, d)
    mods_s = mods_all[:, :ms]

    norm_pre3, norm_mix3, norm_post3 = (a.reshape(depth, 1, d) for a in (norm_pre, norm_mix, norm_post))

    o1, o2, o3 = 3 * fw, 3 * fw + H_FOX, 3 * fw + H_FOX + 3 * mw
    o4 = o3 + 2 * H_MLSTM
    ab_main = jnp.concatenate([ab_w_in[:, :, :o1], ab_w_in[:, :, o2:o3], ab_w_in[:, :, o4:]], axis=2)
    ab_gate = jnp.concatenate([ab_w_in[:, :, o1:o2], ab_w_in[:, :, o3:o4],
                               jnp.zeros((n_ab, d, LANES - H_FOX - 2 * H_MLSTM), F32)], axis=2)
    gate_bias = jnp.concatenate([fox_fgate_b, mlstm_igate_b, mlstm_fgate_b, jnp.zeros((n_ab, 8), F32)],
                                axis=1).reshape(n_ab, 32, 1)
    head_gain = mlstm_norm_g.reshape(n_ab, H_MLSTM, 1, HEAD_DIM)

    n_pool, page = cache_fox_k.shape[1], cache_fox_k.shape[2]
    prow = page * H_FOX
    k_pool = cache_fox_k.reshape(n_ab, n_pool, prow, HEAD_DIM)
    v_pool = cache_fox_v.reshape(n_ab, n_pool, prow, HEAD_DIM)
    lf_t = cache_fox_logf.transpose(0, 1, 3, 2).reshape(n_ab * n_pool * H_FOX, page)
    cl_t = _page_cumsum(lf_t).reshape(n_ab, n_pool, H_FOX, page)
    cl = cl_t.transpose(0, 1, 3, 2)
    tot = jnp.broadcast_to(cl[:, :, page - 1:page, :], cl.shape)
    cs_pool = jnp.stack([cl.reshape(n_ab, n_pool, prow), tot.reshape(n_ab, n_pool, prow)], axis=2)

    cb = cache_win_k.shape[2]
    k_win = cache_win_k.reshape(n_c, db, cb * H_DIL, HEAD_DIM)
    v_win = cache_win_v.reshape(n_c, db, cb * H_DIL, HEAD_DIM)

    zeros_c = jnp.zeros((nb, H_MLSTM, HEAD_DIM, HEAD_DIM), F32)
    zeros_n = jnp.zeros((nb, H_MLSTM, HEAD_DIM, 1), F32)
    zeros_m = jnp.zeros((nb, H_MLSTM, 1, 1), F32)

    outs = {name: [] for name in ("fl_p", "fk_s", "fv_s", "fl_s", "mc_p", "mn_p", "mm_p",
                                  "mc_s", "mn_s", "mm_s", "wk_new", "wv_new")}
    pp_ab = pp_c = None

    for l in range(depth):
        j = l // 2
        last = l == depth - 1
        mp = _Mods(mods_p, l, True, TM, nb, s)
        md = _Mods(mods_s, l, False, ms)

        xp, xs = _ffn(xp, xs, mp, md, (0, 1, 2), norm_pre3, ffn_pre_w13, ffn_pre_w2, l)

        if l % 2 == 0:
            pp_ab, ps, gp, gs = _proj_in(xp, xs, mp, md, (3, 4), norm_mix3, l, ab_main, ab_gate, j, j, n_ab, pp_ab)
            gt = _gates(gp, gate_bias[j], nb, s, s)
            gt4 = gt.reshape(nb, 32, 1, s)
            o_fox = _fox_prompt(pp_ab, j, gt4, nb, s)
            hn, c_p, n_p, m_p = _mlstm(pp_ab, j, gt4, zeros_c, zeros_n, zeros_m, head_gain, j, nb, s)
            outs["fl_p"].append(gt[:, 24:32, :].transpose(0, 2, 1))
            outs["mc_p"].append(c_p)
            outs["mn_p"].append(n_p.reshape(nb, H_MLSTM, HEAD_DIM))
            outs["mm_p"].append(m_p.reshape(nb, H_MLSTM))

            gts = _gates(_pad_rows(gs, db, t, CHUNK).reshape(db * CHUNK, LANES), gate_bias[j], db, CHUNK, t)
            gts4 = gts.reshape(db, 32, 1, CHUNK)
            nk = NEW_KEYS
            c_new = gts[:, 0:8, :nk].transpose(0, 2, 1).reshape(db, 1, nk * H_FOX)
            o_dec = _fox_decode(page_table, _head_rows(ps[:, :fw], db, t, H_FOX), k_pool, v_pool, cs_pool, j,
                                _pad_rows(_head_rows(ps[:, fw:2 * fw], db, t, H_FOX).reshape(db * t * H_FOX, HEAD_DIM),
                                          db, t * H_FOX, nk * H_FOX),
                                _pad_rows(_head_rows(ps[:, 2 * fw:3 * fw], db, t, H_FOX).reshape(db * t * H_FOX, HEAD_DIM),
                                          db, t * H_FOX, nk * H_FOX),
                                c_new)
            o_fox_s = o_dec.reshape(ms, fw)
            ps_pad = _pad_rows(ps, db, t, CHUNK).reshape(1, db * CHUNK, -1)
            hn_s, c_s, n_s, m_s = _mlstm(ps_pad, 0, gts4, state_mlstm_c[j],
                                         state_mlstm_n[j].reshape(db, H_MLSTM, HEAD_DIM, 1),
                                         state_mlstm_m[j].reshape(db, H_MLSTM, 1, 1),
                                         head_gain, j, db, CHUNK)
            hn_s = hn_s.reshape(db, CHUNK, mw)[:, :t].reshape(ms, mw)
            xp, xs = _proj_out(o_fox, 0, hn, 0, o_fox_s.astype(BF16), hn_s, ab_w_out, j, xp, xs, mp, md, 5)
            outs["fk_s"].append(ps[:, fw:2 * fw].reshape(db, t, H_FOX, HEAD_DIM))
            outs["fv_s"].append(ps[:, 2 * fw:3 * fw].reshape(db, t, H_FOX, HEAD_DIM))
            outs["fl_s"].append(gts[:, 24:32, :t].transpose(0, 2, 1))
            outs["mc_s"].append(c_s)
            outs["mn_s"].append(n_s.reshape(db, H_MLSTM, HEAD_DIM))
            outs["mm_s"].append(m_s.reshape(db, H_MLSTM))
        else:
            pp_c, ps = _proj_in(xp, xs, mp, md, (3, 4), norm_mix3, l, c_w_in, None, j, j, n_c, pp_c)
            o_dil = _dil_prompt(pp_c, j, nb, s)

            k_new = _head_rows(ps[:, dw:2 * dw], db, t, H_DIL)
            v_new = _head_rows(ps[:, 2 * dw:], db, t, H_DIL)
            nrow = NEW_KEYS * H_DIL
            o_dec = _dil_decode(_head_rows(ps[:, :dw], db, t, H_DIL), k_win, v_win, j,
                                jnp.pad(k_new, ((0, 0), (0, nrow - t * H_DIL), (0, 0))),
                                jnp.pad(v_new, ((0, 0), (0, nrow - t * H_DIL), (0, 0))), t)
            o_dil_s = o_dec.reshape(ms, dw)
            o_dil_s = o_dil_s.astype(BF16)
            xp, xs = _proj_out(o_dil, 0, o_dil, 1, o_dil_s, o_dil_s, c_w_out, j, xp, xs, mp, md, 5)
            outs["wk_new"].append(k_new.reshape(db, t, H_DIL, HEAD_DIM))
            outs["wv_new"].append(v_new.reshape(db, t, H_DIL, HEAD_DIM))

        fin = norm_final if last else None
        xp, xs = _ffn(xp, xs, mp, md, (6, 7, 8), norm_post3, ffn_post_w13, ffn_post_w2, l, fin)

    st = lambda name: jnp.stack(outs[name])
    keep = min(DIL_PATTERNS[-1][0], cb + t)
    wk_s = jnp.concatenate([cache_win_k, st("wk_new")], axis=2)[:, :, cb + t - keep:]
    wv_s = jnp.concatenate([cache_win_v, st("wv_new")], axis=2)[:, :, cb + t - keep:]
    keep_p = min(DIL_PATTERNS[-1][0], s)
    heads = lambda a, n_heads: a.reshape(a.shape[0], nb, s, n_heads, HEAD_DIM)
    return (xp.reshape(nb, s, d), xs.reshape(db, t, d),
            heads(pp_ab[:, :, fw:2 * fw], H_FOX), heads(pp_ab[:, :, 2 * fw:3 * fw], H_FOX),
            st("fl_p"), st("fk_s"), st("fv_s"), st("fl_s"),
            st("mc_p"), st("mn_p"), st("mm_p"), st("mc_s"), st("mn_s"), st("mm_s"),
            heads(pp_c[:, :, dw:2 * dw], H_DIL)[:, :, s - keep_p:], heads(pp_c[:, :, 2 * dw:], H_DIL)[:, :, s - keep_p:],
            wk_s, wv_s)
```

```python
import functools

import jax
import jax.numpy as jnp
from jax import lax
from jax.experimental import pallas as pl
from jax.experimental.pallas import tpu as pltpu

F32 = jnp.float32
BF16 = jnp.bfloat16

HEAD_DIM = 128
H_FOX = 8
H_MLSTM = 8
H_DIL = 16
N_MOD = 9
EPS = 1e-6
FFN_RES = 0.5
QK_SCALE = HEAD_DIM ** -0.5
DIL_PATTERNS = ((128, 1), (512, 4), (2048, 16))
CHUNK = 128
NEG = -1e30

LANES = 128
SUBLANES = 8
VMEM_BIG = 60 * 1024 * 1024
VMEM_MID = 48 * 1024 * 1024

TM = 1024
TF = 256
TN = 1024
TN_IN = 1024
ADALN_TN = 2304
ADALN_STREAMS = 4
TQ = 256
MLSTM_HEADS = 2
PAGES_PER_STEP = 16
DIL_TK = 512
NEW_KEYS = 16


def _params(sem, vmem=VMEM_MID):
    return pltpu.CompilerParams(dimension_semantics=sem, vmem_limit_bytes=vmem)


def _sigmoid(x):
    return 1.0 / (1.0 + jnp.exp(-x))


def _log_sigmoid(x):
    return jnp.minimum(x, 0.0) - jnp.log1p(jnp.exp(-jnp.abs(x)))


def _dot(a, b):
    return jnp.dot(a, b, preferred_element_type=F32)


def _dot_nt(a, b):
    return lax.dot_general(a, b, (((1,), (1,)), ((), ())), preferred_element_type=F32)


def _rms(x):
    return x * lax.rsqrt(jnp.mean(x * x, axis=-1, keepdims=True) + EPS)


def _norm_mod(x, g, shift, scale):
    return _rms(x) * g * (1.0 + scale) + shift


def _cumsum_lanes(x):
    lane = lax.broadcasted_iota(jnp.int32, x.shape, 1)
    sh = 1
    while sh < x.shape[1]:
        x = x + jnp.where(lane >= sh, pltpu.roll(x, sh, 1), 0.0)
        sh *= 2
    return x


class _Mods:
    def __init__(self, arr, layer, prompt, tm, n_seq=None, seq_rows=None):
        self.arr, self.layer, self.prompt, self.tm, self.n_seq, self.seq_rows = arr, layer, prompt, tm, n_seq, seq_rows

    def spec(self, k, tn=None):
        d = self.arr.shape[-1] if self.prompt else self.arr.shape[-1] // N_MOD
        l, tm = self.layer, self.tm
        if self.prompt:
            n_seq, seq_rows = self.n_seq, self.seq_rows
            row = lambda i: (l * n_seq + (i * tm) // seq_rows) * N_MOD + k
            if tn is None:
                return pl.BlockSpec((None, 1, d), lambda i, *_: (row(i), 0, 0))
            return pl.BlockSpec((None, 1, tn), lambda i, j: (row(i), 0, j))
        if tn is None:
            return pl.BlockSpec((None, tm, d), lambda i, *_: (l, 0, k))
        return pl.BlockSpec((None, tm, tn), lambda i, j: (l, 0, k * (d // tn) + j))


def _adaln_kernel(c_ref, *refs):
    w_refs, b_ref, o_ref = refs[:-2], refs[-2], refs[-1]
    c = c_ref[...]
    s = (c * _sigmoid(c)).astype(BF16)
    dk = w_refs[0].shape[0]
    acc = b_ref[...]
    for r, w_ref in enumerate(w_refs):
        acc = acc + _dot(s[:, r * dk:(r + 1) * dk], w_ref[...].astype(BF16))
    o_ref[...] = acc


def _adaln_all(c_rows, w_ada, b_ada):
    depth, d, n = w_ada.shape
    rows = c_rows.shape[0]
    tn = ADALN_TN
    ks = ADALN_STREAMS
    assert n % tn == 0 and d % ks == 0
    chunk = lambda r: pl.BlockSpec((None, d // ks, tn), lambda l, j: (l, r, j))
    return pl.pallas_call(
        _adaln_kernel,
        grid=(depth, n // tn),
        in_specs=[pl.BlockSpec((rows, d), lambda l, j: (0, 0))] + [chunk(r) for r in range(ks)]
        + [pl.BlockSpec((None, 1, tn), lambda l, j: (l, 0, j))],
        out_specs=pl.BlockSpec((None, rows, tn), lambda l, j: (l, 0, j)),
        out_shape=jax.ShapeDtypeStruct((depth, rows, n), F32),
        compiler_params=_params(("parallel", "parallel")),
        name="adaln",
    )(c_rows, *([w_ada] * ks), b_ada.reshape(depth, 1, n))


def _ffn_kernel(x_ref, sh_ref, sc_ref, gt_ref, xs_ref, shs_ref, scs_ref, gts_ref, g_ref, w13_hbm, w2_hbm,
                *rest, layer, d_ff, nf, tf, rem, final, tm, ni):
    if final:
        gf_ref, o_ref, os_ref, h_ref, w1_buf, w3_buf, w2_buf, sem = rest
    else:
        o_ref, os_ref, h_ref, w1_buf, w3_buf, w2_buf, sem = rest
    i = pl.program_id(0)

    def tile_copies(f, slot, width):
        c0 = f * tf if isinstance(f, int) else pl.multiple_of(f * tf, tf)
        return (pltpu.make_async_copy(w13_hbm.at[layer, :, pl.ds(c0, width)],
                                      w1_buf.at[slot, :, pl.ds(0, width)], sem.at[slot, 0]),
                pltpu.make_async_copy(w13_hbm.at[layer, :, pl.ds(d_ff + c0, width)],
                                      w3_buf.at[slot, :, pl.ds(0, width)], sem.at[slot, 1]),
                pltpu.make_async_copy(w2_hbm.at[layer, pl.ds(c0, width), :],
                                      w2_buf.at[slot, pl.ds(0, width), :], sem.at[slot, 2]))

    def start(f, slot, width):
        for c in tile_copies(f, slot, width):
            c.start()

    def wait(f, slot, width):
        for c in tile_copies(f, slot, width):
            c.wait()

    def finish(x, gate, acc):
        y = x + (FFN_RES * gate) * acc
        return _rms(y) * gf_ref[...] if final else y

    def run(rows):
        with_sample = rows > tm

        def part(w1, w3, w2):
            h = h_ref[:rows, :]
            a = _dot(h, w1.astype(BF16))
            g = _dot(h, w3.astype(BF16))
            u = (a * _sigmoid(a) * g).astype(BF16)
            return _dot(u, w2.astype(BF16))

        start(0, 0, tf)
        h_ref[:tm, :] = _norm_mod(x_ref[...], g_ref[...], sh_ref[...], sc_ref[...]).astype(BF16)
        if with_sample:
            h_ref[tm:, :] = _norm_mod(xs_ref[...], g_ref[...], shs_ref[...], scs_ref[...]).astype(BF16)

        wait(0, 0, tf)
        start(1, 1, tf)
        p = part(w1_buf[0], w3_buf[0], w2_buf[0])
        o_ref[...] = p[:tm]
        if with_sample:
            os_ref[...] = p[tm:]

        def body(f, carry):
            slot = lax.rem(f, 2)
            wait(f, slot, tf)

            @pl.when(f + 1 < nf - 1)
            def _():
                start(f + 1, 1 - slot, tf)

            @pl.when(f + 1 == nf - 1)
            def _():
                start(nf - 1, 1 - slot, rem)

            p = part(w1_buf[slot], w3_buf[slot], w2_buf[slot])
            o_ref[...] += p[:tm]
            if with_sample:
                os_ref[...] += p[tm:]
            return carry

        lax.fori_loop(1, nf - 1, body, 0)

        slot = (nf - 1) % 2
        wait(nf - 1, slot, rem)
        p = part(w1_buf[slot, :, :rem], w3_buf[slot, :, :rem], w2_buf[slot, :rem, :])
        o_ref[...] = finish(x_ref[...], gt_ref[...], o_ref[...] + p[:tm])
        if with_sample:
            os_ref[...] = finish(xs_ref[...], gts_ref[...], os_ref[...] + p[tm:])

    @pl.when(i < ni - 1)
    def _():
        run(tm)

    @pl.when(i == ni - 1)
    def _():
        run(h_ref.shape[0])


def _ffn(x, xs, mods, mods_s, ks, g_all, w13_all, w2_all, layer, final_g=None):
    m, d = x.shape
    ms = xs.shape[0]
    tm = mods.tm
    d_ff = w2_all.shape[1]
    tf = TF
    nf = pl.cdiv(d_ff, tf)
    rem = d_ff - (nf - 1) * tf
    ni = m // tm
    assert nf >= 3 and m % tm == 0 and d_ff % LANES == 0 and ni >= 2 and mods_s.tm == ms
    final = final_g is not None
    in_specs = [pl.BlockSpec((tm, d), lambda i: (i, 0)),
                mods.spec(ks[0]), mods.spec(ks[1]), mods.spec(ks[2]),
                pl.BlockSpec((ms, d), lambda i: (0, 0)),
                mods_s.spec(ks[0]), mods_s.spec(ks[1]), mods_s.spec(ks[2]),
                pl.BlockSpec((None, 1, d), lambda i: (layer, 0, 0)),
                pl.BlockSpec(memory_space=pl.ANY),
                pl.BlockSpec(memory_space=pl.ANY)]
    args = [x, mods.arr, mods.arr, mods.arr, xs, mods_s.arr, mods_s.arr, mods_s.arr, g_all, w13_all, w2_all]
    if final:
        in_specs.append(pl.BlockSpec((1, d), lambda i: (0, 0)))
        args.append(final_g.reshape(1, d))
    return pl.pallas_call(
        functools.partial(_ffn_kernel, layer=layer, d_ff=d_ff, nf=nf, tf=tf, rem=rem, final=final, tm=tm, ni=ni),
        grid=(ni,),
        in_specs=in_specs,
        out_specs=[pl.BlockSpec((tm, d), lambda i: (i, 0)), pl.BlockSpec((ms, d), lambda i: (0, 0))],
        out_shape=[jax.ShapeDtypeStruct((m, d), F32), jax.ShapeDtypeStruct((ms, d), F32)],
        scratch_shapes=[pltpu.VMEM((tm + ms, d), BF16),
                        pltpu.VMEM((2, d, tf), F32), pltpu.VMEM((2, d, tf), F32), pltpu.VMEM((2, tf, d), F32),
                        pltpu.SemaphoreType.DMA((2, 3))],
        compiler_params=_params(("arbitrary",), VMEM_BIG),
        name="ffn",
    )(*args)


def _proj_in_kernel(x_ref, sh_ref, sc_ref, xs_ref, shs_ref, scs_ref, g_ref, w_ref, *rest, gates, aliased, tm, ni):
    rest = list(rest)
    wg_ref = rest.pop(0) if gates else None
    if aliased:
        rest.pop(0)
    if gates:
        o_ref, os_ref, og_ref, ogs_ref, h_ref = rest
    else:
        o_ref, os_ref, h_ref = rest
    i = pl.program_id(0)
    n = pl.program_id(1)

    @pl.when(n == 0)
    def _():
        h = _norm_mod(x_ref[...], g_ref[...], sh_ref[...], sc_ref[...]).astype(BF16)
        h_ref[:tm, :] = h
        if gates:
            og_ref[...] = _dot(h, wg_ref[...].astype(BF16))

    @pl.when(jnp.logical_and(n == 0, i == ni - 1))
    def _():
        hs = _norm_mod(xs_ref[...], g_ref[...], shs_ref[...], scs_ref[...]).astype(BF16)
        h_ref[tm:, :] = hs
        if gates:
            ogs_ref[...] = _dot(hs, wg_ref[...].astype(BF16))

    @pl.when(i < ni - 1)
    def _():
        o_ref[...] = _dot(h_ref[:tm, :], w_ref[...].astype(BF16))

    @pl.when(i == ni - 1)
    def _():
        y = _dot(h_ref[...], w_ref[...].astype(BF16))
        o_ref[...] = y[:tm]
        os_ref[...] = y[tm:]


def _proj_in(x, xs, mods, mods_s, ks, g_all, layer, w_all, wg_all, j, slot=0, n_slots=1, stacked=None):
    m, d = x.shape
    ms = xs.shape[0]
    tm = mods.tm
    n = w_all.shape[2]
    tn = TN_IN
    ni = m // tm
    assert n % tn == 0 and m % tm == 0 and (stacked is None) == (slot == 0) and ni >= 2 and mods_s.tm == ms
    gates = wg_all is not None
    aliased = stacked is not None
    s_col = lambda i, c: jnp.where(i == ni - 1, c, 0)
    in_specs = [pl.BlockSpec((tm, d), lambda i, c: (i, 0)), mods.spec(ks[0]), mods.spec(ks[1]),
                pl.BlockSpec((ms, d), lambda i, c: (0, 0)), mods_s.spec(ks[0]), mods_s.spec(ks[1]),
                pl.BlockSpec((None, 1, d), lambda i, c: (layer, 0, 0)),
                pl.BlockSpec((None, d, tn), lambda i, c: (j, 0, c))]
    out_specs = [pl.BlockSpec((None, tm, tn), lambda i, c: (slot, i, c)),
                 pl.BlockSpec((ms, tn), lambda i, c: (0, s_col(i, c)))]
    out_shape = [jax.ShapeDtypeStruct((n_slots, m, n), F32), jax.ShapeDtypeStruct((ms, n), F32)]
    args = [x, mods.arr, mods.arr, xs, mods_s.arr, mods_s.arr, g_all, w_all]
    if gates:
        in_specs.append(pl.BlockSpec((None, d, LANES), lambda i, c: (j, 0, 0)))
        out_specs += [pl.BlockSpec((tm, LANES), lambda i, c: (i, 0)), pl.BlockSpec((ms, LANES), lambda i, c: (0, 0))]
        out_shape += [jax.ShapeDtypeStruct((m, LANES), F32), jax.ShapeDtypeStruct((ms, LANES), F32)]
        args.append(wg_all)
    if aliased:
        in_specs.append(pl.BlockSpec(memory_space=pl.ANY))
        args.append(stacked)
    return pl.pallas_call(
        functools.partial(_proj_in_kernel, gates=gates, aliased=aliased, tm=tm, ni=ni),
        grid=(ni, n // tn),
        in_specs=in_specs, out_specs=out_specs, out_shape=out_shape,
        input_output_aliases={len(args) - 1: 0} if aliased else {},
        scratch_shapes=[pltpu.VMEM((tm + ms, d), BF16)],
        compiler_params=_params(("arbitrary", "arbitrary"), VMEM_BIG),
        name="proj_in",
    )(*args)


def _proj_out_kernel(a_ref, b_ref, as_ref, bs_ref, wa_ref, wb_ref, x_ref, gt_ref, xs_ref, gts_ref, o_ref, os_ref,
                     *, ni):
    wa = wa_ref[...].astype(BF16)
    wb = wb_ref[...].astype(BF16)
    o_ref[...] = x_ref[...] + gt_ref[...] * (_dot(a_ref[...], wa) + _dot(b_ref[...], wb))

    @pl.when(pl.program_id(0) == ni - 1)
    def _():
        os_ref[...] = xs_ref[...] + gts_ref[...] * (_dot(as_ref[...], wa) + _dot(bs_ref[...], wb))


def _proj_out(mix_a, a_blk, mix_b, b_blk, mix_as, mix_bs, w_all, j, x, xs, mods, mods_s, k):
    m, d = x.shape
    ms = xs.shape[0]
    tm = mods.tm
    kh = w_all.shape[1] // 2
    tn = TN
    ni = m // tm
    assert mods_s.tm == ms and ni >= 2
    s_col = lambda i, c: jnp.where(i == ni - 1, c, 0)
    layer = mods_s.layer
    return pl.pallas_call(
        functools.partial(_proj_out_kernel, ni=ni),
        grid=(ni, d // tn),
        in_specs=[pl.BlockSpec((tm, kh), lambda i, c: (i, a_blk)),
                  pl.BlockSpec((tm, kh), lambda i, c: (i, b_blk)),
                  pl.BlockSpec((ms, kh), lambda i, c: (0, a_blk)),
                  pl.BlockSpec((ms, kh), lambda i, c: (0, b_blk)),
                  pl.BlockSpec((None, kh, tn), lambda i, c: (j, 0, c)),
                  pl.BlockSpec((None, kh, tn), lambda i, c: (j, 1, c)),
                  pl.BlockSpec((tm, tn), lambda i, c: (i, c)),
                  mods.spec(k, tn),
                  pl.BlockSpec((ms, tn), lambda i, c: (0, s_col(i, c))),
                  pl.BlockSpec((None, ms, tn), lambda i, c: (layer, 0, k * (d // tn) + s_col(i, c)))],
        out_specs=[pl.BlockSpec((tm, tn), lambda i, c: (i, c)),
                   pl.BlockSpec((ms, tn), lambda i, c: (0, s_col(i, c)))],
        out_shape=[jax.ShapeDtypeStruct((m, d), F32), jax.ShapeDtypeStruct((ms, d), F32)],
        compiler_params=_params(("arbitrary", "arbitrary"), VMEM_BIG),
        name="proj_out",
    )(mix_a, mix_b, mix_as, mix_bs, w_all, w_all, x, mods.arr, xs, mods_s.arr)


def _gates_kernel(p_ref, b_ref, o_ref, *, n_valid):
    s = p_ref.shape[0]
    lane = lax.broadcasted_iota(jnp.int32, (8, LANES), 1)
    carry = jnp.zeros((8, 1), F32)
    for blk in range(s // LANES):
        sl = slice(blk * LANES, (blk + 1) * LANES)
        pre = p_ref[sl, :].T[0:32] + b_ref[...]
        valid = (lane + blk * LANES) < n_valid
        lf_f = jnp.where(valid, _log_sigmoid(pre[0:8]), 0.0)
        ig = jnp.where(valid, pre[8:16], NEG)
        lf_m = jnp.where(valid, _log_sigmoid(pre[16:24]), 0.0)
        c_f = _cumsum_lanes(lf_f) + carry
        carry = c_f[:, LANES - 1:LANES]
        o_ref[0:8, sl] = c_f
        o_ref[8:16, sl] = ig
        o_ref[16:24, sl] = _cumsum_lanes(lf_m)
        o_ref[24:32, sl] = lf_f


def _gates(pre, bias, n_seq, s, n_valid):
    return pl.pallas_call(
        functools.partial(_gates_kernel, n_valid=n_valid),
        grid=(n_seq,),
        in_specs=[pl.BlockSpec((s, LANES), lambda b: (b, 0)),
                  pl.BlockSpec((32, 1), lambda b: (0, 0))],
        out_specs=pl.BlockSpec((None, 32, s), lambda b: (b, 0, 0)),
        out_shape=jax.ShapeDtypeStruct((n_seq, 32, s), F32),
        compiler_params=_params(("parallel",)),
        name="gates",
    )(pre, bias)


def _fox_kernel(q_ref, k_ref, v_ref, c_ref, o_ref, kb_ref, vb_ref, *, tq):
    s = q_ref.shape[0]
    kb_ref[...] = k_ref[...].astype(BF16)
    vb_ref[...] = v_ref[...].astype(BF16)
    row = lax.broadcasted_iota(jnp.int32, (tq, tq), 0)
    col = lax.broadcasted_iota(jnp.int32, (tq, tq), 1)
    for qi in range(s // tq):
        q = (q_ref[qi * tq:(qi + 1) * tq, :] * QK_SCALE).astype(BF16)
        m = l = acc = None
        for kb in range(qi + 1):
            ks = slice(kb * tq, (kb + 1) * tq)
            sc = _dot_nt(q, kb_ref[ks, :]) - c_ref[:, ks]
            if kb == qi:
                sc = jnp.where(col <= row, sc, -jnp.inf)
            bm = jnp.max(sc, axis=-1, keepdims=True)
            if m is None:
                m = bm
                p = jnp.exp(sc - m)
                l = jnp.sum(p, axis=-1, keepdims=True)
                acc = _dot(p.astype(BF16), vb_ref[ks, :])
            else:
                m_new = jnp.maximum(m, bm)
                alpha = jnp.exp(m - m_new)
                p = jnp.exp(sc - m_new)
                l = alpha * l + jnp.sum(p, axis=-1, keepdims=True)
                acc = alpha * acc + _dot(p.astype(BF16), vb_ref[ks, :])
                m = m_new
        o_ref[qi * tq:(qi + 1) * tq, :] = (acc / l).astype(o_ref.dtype)


def _fox_prompt(proj, slot, gt4, n_seq, s):
    col = lambda base: pl.BlockSpec((None, s, HEAD_DIM), lambda b, h: (slot, b, base + h))
    return pl.pallas_call(
        functools.partial(_fox_kernel, tq=TQ),
        grid=(n_seq, H_FOX),
        in_specs=[col(0), col(H_FOX), col(2 * H_FOX),
                  pl.BlockSpec((None, None, 1, s), lambda b, h: (b, h, 0, 0))],
        out_specs=pl.BlockSpec((s, HEAD_DIM), lambda b, h: (b, h)),
        out_shape=jax.ShapeDtypeStruct((n_seq * s, H_FOX * HEAD_DIM), BF16),
        scratch_shapes=[pltpu.VMEM((s, HEAD_DIM), BF16), pltpu.VMEM((s, HEAD_DIM), BF16)],
        compiler_params=_params(("parallel", "parallel")),
        name="fox_prompt",
    )(proj, proj, proj, gt4)


def _page_cumsum_kernel(x_ref, o_ref):
    o_ref[...] = _cumsum_lanes(x_ref[...])


def _page_cumsum(lf_t):
    rows, page = lf_t.shape
    tr = 1024
    assert rows % tr == 0 and page == LANES
    return pl.pallas_call(
        _page_cumsum_kernel,
        grid=(rows // tr,),
        in_specs=[pl.BlockSpec((tr, page), lambda i: (i, 0))],
        out_specs=pl.BlockSpec((tr, page), lambda i: (i, 0)),
        out_shape=jax.ShapeDtypeStruct((rows, page), F32),
        compiler_params=_params(("parallel",)),
        name="page_cumsum",
    )(lf_t)


def _fox_decode_kernel(pt_ref, q_ref, *refs, g, n_steps):
    del pt_ref
    k_refs, v_refs, cs_refs = refs[:g], refs[g:2 * g], refs[2 * g:3 * g]
    kn_ref, vn_ref, cn_ref, o_ref, m_ref, l_ref, acc_ref, carry_ref = refs[3 * g:]
    p = pl.program_id(1)
    rows = q_ref.shape[0]
    hbits = H_FOX.bit_length() - 1

    @pl.when(p == 0)
    def _():
        m_ref[...] = jnp.full(m_ref.shape, -jnp.inf, F32)
        l_ref[...] = jnp.zeros(l_ref.shape, F32)
        acc_ref[...] = jnp.zeros(acc_ref.shape, F32)
        carry_ref[...] = jnp.zeros(carry_ref.shape, F32)

    def iotas(ncols):
        return (lax.broadcasted_iota(jnp.int32, (rows, ncols), 0),
                lax.broadcasted_iota(jnp.int32, (rows, ncols), 1))

    def attend(blocks, mask):
        q = (q_ref[...] * QK_SCALE).astype(BF16)
        scores = [jnp.where(mask, _dot_nt(q, k.astype(BF16)) - bias, -jnp.inf) for k, _, bias in blocks]
        m_old = m_ref[...]
        m_new = m_old
        for sc in scores:
            m_new = jnp.maximum(m_new, jnp.max(sc, axis=-1, keepdims=True))
        alpha = jnp.exp(m_old - m_new)
        l = alpha * l_ref[...]
        acc = alpha * acc_ref[...]
        for sc, (_, v, _) in zip(scores, blocks):
            pr = jnp.exp(sc - m_new)
            l = l + jnp.sum(pr, axis=-1, keepdims=True)
            acc = acc + _dot(pr.astype(BF16), v.astype(BF16))
        m_ref[...] = m_new
        l_ref[...] = l
        acc_ref[...] = acc

    @pl.when(p < n_steps)
    def _():
        r, c = iotas(k_refs[0].shape[0])
        mask = (r & (H_FOX - 1)) == (c & (H_FOX - 1))
        carry = carry_ref[...]
        blocks = []
        for i in range(g):
            cs = cs_refs[i][...]
            blocks.append((k_refs[i][...], v_refs[i][...], carry + cs[0:1]))
            carry = carry + cs[1:2]
        carry_ref[...] = carry
        attend(blocks, mask)

    @pl.when(p == n_steps)
    def _():
        ncol = kn_ref.shape[0]
        r, c = iotas(ncol)
        same_head = (r & (H_FOX - 1)) == (c & (H_FOX - 1))
        causal = lax.shift_right_logical(c, hbits) <= lax.shift_right_logical(r, hbits)
        bias = carry_ref[:, :ncol] + cn_ref[...]
        attend([(kn_ref[...], vn_ref[...], bias)], jnp.logical_and(same_head, causal))
        o_ref[...] = acc_ref[...] / l_ref[...]


def _fox_decode(page_table, q, k_pool, v_pool, cs_pool, j, k_new, v_new, c_new):
    db, rows, hd = q.shape
    n_pages = page_table.shape[1]
    prow = k_pool.shape[2]
    g = PAGES_PER_STEP
    assert n_pages % g == 0
    n_steps = n_pages // g
    nrow = k_new.shape[1]

    def pool_map(i):
        return lambda b, p, pt: (j, pt[b, jnp.minimum(p * g + i, n_pages - 1)], 0, 0)

    seq_map = lambda b, p, pt: (b, 0, 0)
    in_specs = [pl.BlockSpec((None, rows, hd), seq_map)]
    in_specs += [pl.BlockSpec((None, None, prow, hd), pool_map(i)) for i in range(g)]
    in_specs += [pl.BlockSpec((None, None, prow, hd), pool_map(i)) for i in range(g)]
    in_specs += [pl.BlockSpec((None, None, 2, prow), pool_map(i)) for i in range(g)]
    in_specs += [pl.BlockSpec((None, nrow, hd), seq_map), pl.BlockSpec((None, nrow, hd), seq_map),
                 pl.BlockSpec((None, 1, nrow), seq_map)]
    grid_spec = pltpu.PrefetchScalarGridSpec(
        num_scalar_prefetch=1,
        grid=(db, n_steps + 1),
        in_specs=in_specs,
        out_specs=pl.BlockSpec((None, rows, hd), seq_map),
        scratch_shapes=[pltpu.VMEM((rows, 1), F32), pltpu.VMEM((rows, 1), F32),
                        pltpu.VMEM((rows, hd), F32), pltpu.VMEM((1, prow), F32)])
    return pl.pallas_call(
        functools.partial(_fox_decode_kernel, g=g, n_steps=n_steps),
        grid_spec=grid_spec,
        out_shape=jax.ShapeDtypeStruct((db, rows, hd), F32),
        compiler_params=_params(("parallel", "arbitrary")),
        name="fox_decode",
    )(page_table, q, *([k_pool] * g), *([v_pool] * g), *([cs_pool] * g), k_new, v_new, c_new)


def _mlstm_kernel(q_ref, k_ref, v_ref, og_ref, *refs, nh):
    ig_refs, b_refs = refs[:nh], refs[nh:2 * nh]
    c0_ref, n0_ref, m0_ref, g_ref, hn_ref, c_ref, n_ref, m_ref = refs[2 * nh:]
    s = q_ref.shape[0]
    ln = CHUNK
    row = lax.broadcasted_iota(jnp.int32, (ln, ln), 0)
    col = lax.broadcasted_iota(jnp.int32, (ln, ln), 1)
    causal = col <= row

    last_lane = lax.broadcasted_iota(jnp.int32, (1, ln), 1) == ln - 1
    assert ln == HEAD_DIM == LANES
    for hh in range(nh):
        hs = slice(hh * HEAD_DIM, (hh + 1) * HEAD_DIM)
        c_st = c0_ref[hh]
        n_st = jnp.broadcast_to(n0_ref[hh], (HEAD_DIM, LANES))
        m_st = jnp.broadcast_to(m0_ref[hh], (1, LANES))
        for ci in range(s // ln):
            ts = slice(ci * ln, (ci + 1) * ln)
            qb = q_ref[ts, hs].astype(BF16)
            k_t = (k_ref[ts, hs] * QK_SCALE).T
            vb = v_ref[ts, hs].astype(BF16)
            b_row = b_refs[hh][:, ts]
            ig_row = ig_refs[hh][:, ts]
            b_mat = jnp.broadcast_to(b_row, (ln, ln))
            b_col = b_mat.T
            dmat = jnp.where(causal, b_col - b_mat + ig_row, NEG)
            inter = b_col + m_st
            m_tok = jnp.maximum(inter, jnp.max(dmat, axis=-1, keepdims=True))
            w_inter = jnp.exp(inter - m_tok)
            qk = _dot(qb, k_t.astype(BF16)) * jnp.exp(dmat - m_tok)
            num = w_inter * _dot(qb, c_st.astype(BF16)) + _dot(qk.astype(BF16), vb)
            den = w_inter * _dot(qb, n_st.astype(BF16)) + jnp.sum(qk, axis=-1, keepdims=True)
            h = num / jnp.maximum(jnp.abs(den), jnp.exp(-m_tok))
            hn_ref[ts, hs] = (_rms(h) * g_ref[hh] * _sigmoid(og_ref[ts, hs])).astype(hn_ref.dtype)
            b_last = jnp.sum(jnp.where(last_lane, b_row, 0.0), axis=-1, keepdims=True)
            g_row = b_last - b_row + ig_row
            m_new = jnp.maximum(b_last + m_st, jnp.max(g_row, axis=-1, keepdims=True))
            a_prev = jnp.exp(b_last + m_st - m_new)
            kw = k_t * jnp.exp(g_row - m_new)
            c_st = a_prev * c_st + _dot(kw.astype(BF16), vb)
            n_st = a_prev * n_st + jnp.sum(kw, axis=-1, keepdims=True)
            m_st = m_new
        c_ref[hh] = c_st
        n_ref[hh] = n_st[:, 0:1]
        m_ref[hh] = m_st[:, 0:1]


def _mlstm(proj, slot, gt4, c0, n0, m0, g_all, j, n_seq, s):
    h, nh = H_MLSTM, MLSTM_HEADS
    w = nh * HEAD_DIM
    col = lambda base: pl.BlockSpec((None, s, w), lambda b, c: (slot, b, base // nh + c))
    row = lambda base, hh: pl.BlockSpec((None, None, 1, s), lambda b, c: (b, base + c * nh + hh, 0, 0))
    st = lambda shape: pl.BlockSpec((None, nh) + shape, lambda b, c: (b, c, 0, 0))
    return pl.pallas_call(
        functools.partial(_mlstm_kernel, nh=nh),
        grid=(n_seq, h // nh),
        in_specs=[col(24), col(32), col(40), col(48)]
        + [row(8, hh) for hh in range(nh)] + [row(16, hh) for hh in range(nh)]
        + [st((HEAD_DIM, HEAD_DIM)), st((HEAD_DIM, 1)), st((1, 1)),
           pl.BlockSpec((None, nh, 1, HEAD_DIM), lambda b, c: (j, c, 0, 0))],
        out_specs=[pl.BlockSpec((s, w), lambda b, c: (b, c)),
                   st((HEAD_DIM, HEAD_DIM)), st((HEAD_DIM, 1)), st((1, 1))],
        out_shape=[jax.ShapeDtypeStruct((n_seq * s, h * HEAD_DIM), BF16),
                   jax.ShapeDtypeStruct((n_seq, h, HEAD_DIM, HEAD_DIM), F32),
                   jax.ShapeDtypeStruct((n_seq, h, HEAD_DIM, 1), F32),
                   jax.ShapeDtypeStruct((n_seq, h, 1, 1), F32)],
        compiler_params=_params(("parallel", "parallel")),
        name="mlstm",
    )(proj, proj, proj, proj, *([gt4] * (2 * nh)), c0, n0, m0, g_all)


def _dil_kernel(q_ref, k_ref, v_ref, o_ref, m_ref, l_ref, acc_ref):
    s = q_ref.shape[0]
    blk = CHUNK
    row = lax.broadcasted_iota(jnp.int32, (blk, blk), 0)
    col = lax.broadcasted_iota(jnp.int32, (blk, blk), 1)
    cur_mask = col <= row
    dist = blk + lax.broadcasted_iota(jnp.int32, (blk, 2 * blk), 0) - lax.broadcasted_iota(jnp.int32, (blk, 2 * blk), 1)
    band_mask = jnp.logical_and(dist >= 0, dist <= blk)

    def rows(start, size, d):
        return pl.ds(start, size) if d == 1 else pl.ds(start, size, stride=d)

    for pi, (window, d) in enumerate(DIL_PATTERNS):
        assert window // d == blk and s % (d * blk) == 0
        for r in range(d):
            for n in range(s // (d * blk)):
                start = r + d * blk * n
                q = (q_ref[rows(start, blk, d), :] * QK_SCALE).astype(BF16)
                if n == 0:
                    kv_rows, mask = rows(start, blk, d), cur_mask
                else:
                    kv_rows, mask = rows(start - d * blk, 2 * blk, d), band_mask
                sc = _dot_nt(q, k_ref[kv_rows, :].astype(BF16))
                sc = jnp.where(mask, sc, -jnp.inf)
                m = jnp.max(sc, axis=-1, keepdims=True)
                p = jnp.exp(sc - m)
                out_rows = rows(pi * s + start, blk, d)
                m_ref[out_rows, :] = jnp.broadcast_to(m, (blk, LANES))
                l_ref[out_rows, :] = jnp.broadcast_to(jnp.sum(p, axis=-1, keepdims=True), (blk, LANES))
                acc_ref[out_rows, :] = _dot(p.astype(BF16), v_ref[kv_rows, :].astype(BF16))

    n_pat = len(DIL_PATTERNS)
    ms = [m_ref[pi * s:(pi + 1) * s, :] for pi in range(n_pat)]
    m_all = functools.reduce(jnp.maximum, ms)
    den = jnp.zeros((s, LANES), F32)
    num = jnp.zeros((s, HEAD_DIM), F32)
    for pi in range(n_pat):
        w = jnp.exp(ms[pi] - m_all)
        den = den + w * l_ref[pi * s:(pi + 1) * s, :]
        num = num + w * acc_ref[pi * s:(pi + 1) * s, :]
    o_ref[...] = (num / den).astype(o_ref.dtype)


def _dil_prompt(proj, slot, n_seq, s):
    h = H_DIL
    n_pat = len(DIL_PATTERNS)
    col = lambda base: pl.BlockSpec((None, s, HEAD_DIM), lambda b, c: (slot, b, base + c))
    return pl.pallas_call(
        _dil_kernel,
        grid=(n_seq, h),
        in_specs=[col(0), col(h), col(2 * h)],
        out_specs=pl.BlockSpec((s, HEAD_DIM), lambda b, c: (b, c)),
        out_shape=jax.ShapeDtypeStruct((n_seq * s, h * HEAD_DIM), BF16),
        scratch_shapes=[pltpu.VMEM((n_pat * s, LANES), F32), pltpu.VMEM((n_pat * s, LANES), F32),
                        pltpu.VMEM((n_pat * s, HEAD_DIM), F32)],
        compiler_params=_params(("parallel", "parallel")),
        name="dil_prompt",
    )(proj, proj, proj)


def _dil_decode_kernel(q_ref, k_ref, v_ref, cnt_ref, kn_ref, vn_ref, cntn_ref, o_ref, m_ref, l_ref, acc_ref,
                       *, n_tiles):
    p = pl.program_id(0)
    b = pl.program_id(1)

    @pl.when(p == 0)
    def _():
        m_ref[b] = jnp.full(m_ref.shape[1:], -jnp.inf, F32)
        l_ref[b] = jnp.zeros(l_ref.shape[1:], F32)
        acc_ref[b] = jnp.zeros(acc_ref.shape[1:], F32)

    def process(k, v, cnt):
        cnt = cnt.astype(F32)
        q = (q_ref[...] * QK_SCALE).astype(BF16)
        s = jnp.where(cnt > 0.0, _dot_nt(q, k.astype(BF16)), -jnp.inf)
        m_old = m_ref[b]
        m_new = jnp.maximum(m_old, jnp.max(s, axis=-1, keepdims=True))
        m_safe = jnp.where(m_new == -jnp.inf, 0.0, m_new)
        alpha = jnp.exp(m_old - m_safe)
        pr = cnt * jnp.exp(s - m_safe)
        l_ref[b] = alpha * l_ref[b] + jnp.sum(pr, axis=-1, keepdims=True)
        acc_ref[b] = alpha * acc_ref[b] + _dot(pr.astype(BF16), v.astype(BF16))
        m_ref[b] = m_new

    @pl.when(p < n_tiles)
    def _():
        process(k_ref[...], v_ref[...], cnt_ref[...])

    @pl.when(p == n_tiles)
    def _():
        process(kn_ref[...], vn_ref[...], cntn_ref[...])
        o_ref[...] = acc_ref[b] / l_ref[b]


def _dil_counts(t, cb, n_keys, key0):
    r = jnp.arange(t * H_DIL, dtype=jnp.int32)[:, None]
    c = jnp.arange(n_keys * H_DIL, dtype=jnp.int32)[None, :]
    delta = cb + r // H_DIL - (key0 + c // H_DIL)
    cnt = jnp.zeros(delta.shape, F32)
    for window, d in DIL_PATTERNS:
        cnt = cnt + ((delta >= 0) & (delta % d == 0) & (delta <= window)).astype(F32)
    return jnp.where(r % H_DIL == c % H_DIL, cnt, 0.0).astype(BF16)


def _dil_decode(q, k_cache, v_cache, j, k_new, v_new, t):
    db, rows, hd = q.shape
    cb = k_cache.shape[2] // H_DIL
    tk = DIL_TK
    n_tiles = cb // tk
    nrow = k_new.shape[1]
    cnt = _dil_counts(t, cb, cb, 0)
    cnt_new = _dil_counts(t, cb, nrow // H_DIL, cb)
    seq_map = lambda p, b: (b, 0, 0)
    last_map = lambda p, b: (jnp.where(p == n_tiles, b, 0), 0, 0)
    tile_map = lambda p, b: (j, jnp.where(p < n_tiles, b, 0), jnp.where(p < n_tiles, p, 0), 0)
    return pl.pallas_call(
        functools.partial(_dil_decode_kernel, n_tiles=n_tiles),
        grid=(n_tiles + 1, db),
        in_specs=[pl.BlockSpec((None, rows, hd), seq_map),
                  pl.BlockSpec((None, None, tk * H_DIL, hd), tile_map),
                  pl.BlockSpec((None, None, tk * H_DIL, hd), tile_map),
                  pl.BlockSpec((rows, tk * H_DIL), lambda p, b: (0, jnp.minimum(p, n_tiles - 1))),
                  pl.BlockSpec((None, nrow, hd), last_map),
                  pl.BlockSpec((None, nrow, hd), last_map),
                  pl.BlockSpec((rows, nrow), lambda p, b: (0, 0))],
        out_specs=pl.BlockSpec((None, rows, hd), last_map),
        out_shape=jax.ShapeDtypeStruct((db, rows, hd), F32),
        scratch_shapes=[pltpu.VMEM((db, rows, 1), F32), pltpu.VMEM((db, rows, 1), F32),
                        pltpu.VMEM((db, rows, hd), F32)],
        compiler_params=_params(("arbitrary", "arbitrary")),
        name="dil_decode",
    )(q, k_cache, v_cache, cnt, k_new, v_new, cnt_new)


def _pad_rows(a, n_seq, rows, pad):
    w = a.shape[-1]
    return jnp.pad(a.reshape(n_seq, rows, w), ((0, 0), (0, pad - rows), (0, 0)))


def _head_rows(a, n_seq, t, n_heads):
    return a.reshape(n_seq, t * n_heads, HEAD_DIM)


def kernel(x_prompt, x_sample, c_prompt, c_sample, page_table, cache_fox_k, cache_fox_v, cache_fox_logf, state_mlstm_c, state_mlstm_n, state_mlstm_m, cache_win_k, cache_win_v, w_ada, b_ada, norm_pre, norm_mix, norm_post, norm_final, ffn_pre_w13, ffn_pre_w2, ffn_post_w13, ffn_post_w2, ab_w_in, ab_w_out, fox_fgate_b, mlstm_igate_b, mlstm_fgate_b, mlstm_norm_g, c_w_in, c_w_out):
    nb, s, d = x_prompt.shape
    db, t, _ = x_sample.shape
    depth = w_ada.shape[0]
    n_ab, n_c = ab_w_in.shape[0], c_w_in.shape[0]
    ms = db * t
    fw = H_FOX * HEAD_DIM
    mw = H_MLSTM * HEAD_DIM
    dw = H_DIL * HEAD_DIM
    assert s % TM == 0 and ms % SUBLANES == 0 and t <= NEW_KEYS

    xp = x_prompt.reshape(nb * s, d)
    xs = x_sample.reshape(ms, d)

    pad_rows = -(ms + nb) % SUBLANES
    c_rows = jnp.concatenate([jnp.repeat(c_sample, t, axis=0), c_prompt, jnp.zeros((pad_rows, d), F32)], axis=0)
    mods_all = _adaln_all(c_rows, w_ada, b_ada)
    mods_p = mods_all[:, ms:ms + nb].reshape(depth * nb * N_MOD, 1, d)
    mods_s = mods_all[:, :ms]

    norm_pre3, norm_mix3, norm_post3 = (a.reshape(depth, 1, d) for a in (norm_pre, norm_mix, norm_post))

    o1, o2, o3 = 3 * fw, 3 * fw + H_FOX, 3 * fw + H_FOX + 3 * mw
    o4 = o3 + 2 * H_MLSTM
    ab_main = jnp.concatenate([ab_w_in[:, :, :o1], ab_w_in[:, :, o2:o3], ab_w_in[:, :, o4:]], axis=2)
    ab_gate = jnp.concatenate([ab_w_in[:, :, o1:o2], ab_w_in[:, :, o3:o4],
                               jnp.zeros((n_ab, d, LANES - H_FOX - 2 * H_MLSTM), F32)], axis=2)
    gate_bias = jnp.concatenate([fox_fgate_b, mlstm_igate_b, mlstm_fgate_b, jnp.zeros((n_ab, 8), F32)],
                                axis=1).reshape(n_ab, 32, 1)
    head_gain = mlstm_norm_g.reshape(n_ab, H_MLSTM, 1, HEAD_DIM)

    n_pool, page = cache_fox_k.shape[1], cache_fox_k.shape[2]
    prow = page * H_FOX
    k_pool = cache_fox_k.reshape(n_ab, n_pool, prow, HEAD_DIM)
    v_pool = cache_fox_v.reshape(n_ab, n_pool, prow, HEAD_DIM)
    lf_t = cache_fox_logf.transpose(0, 1, 3, 2).reshape(n_ab * n_pool * H_FOX, page)
    cl_t = _page_cumsum(lf_t).reshape(n_ab, n_pool, H_FOX, page)
    cl = cl_t.transpose(0, 1, 3, 2)
    tot = jnp.broadcast_to(cl[:, :, page - 1:page, :], cl.shape)
    cs_pool = jnp.stack([cl.reshape(n_ab, n_pool, prow), tot.reshape(n_ab, n_pool, prow)], axis=2)

    cb = cache_win_k.shape[2]
    k_win = cache_win_k.reshape(n_c, db, cb * H_DIL, HEAD_DIM)
    v_win = cache_win_v.reshape(n_c, db, cb * H_DIL, HEAD_DIM)

    zeros_c = jnp.zeros((nb, H_MLSTM, HEAD_DIM, HEAD_DIM), F32)
    zeros_n = jnp.zeros((nb, H_MLSTM, HEAD_DIM, 1), F32)
    zeros_m = jnp.zeros((nb, H_MLSTM, 1, 1), F32)

    outs = {name: [] for name in ("fl_p", "fk_s", "fv_s", "fl_s", "mc_p", "mn_p", "mm_p",
                                  "mc_s", "mn_s", "mm_s", "wk_new", "wv_new")}
    pp_ab = pp_c = None

    for l in range(depth):
        j = l // 2
        last = l == depth - 1
        mp = _Mods(mods_p, l, True, TM, nb, s)
        md = _Mods(mods_s, l, False, ms)

        xp, xs = _ffn(xp, xs, mp, md, (0, 1, 2), norm_pre3, ffn_pre_w13, ffn_pre_w2, l)

        if l % 2 == 0:
            pp_ab, ps, gp, gs = _proj_in(xp, xs, mp, md, (3, 4), norm_mix3, l, ab_main, ab_gate, j, j, n_ab, pp_ab)
            gt = _gates(gp, gate_bias[j], nb, s, s)
            gt4 = gt.reshape(nb, 32, 1, s)
            o_fox = _fox_prompt(pp_ab, j, gt4, nb, s)
            hn, c_p, n_p, m_p = _mlstm(pp_ab, j, gt4, zeros_c, zeros_n, zeros_m, head_gain, j, nb, s)
            outs["fl_p"].append(gt[:, 24:32, :].transpose(0, 2, 1))
            outs["mc_p"].append(c_p)
            outs["mn_p"].append(n_p.reshape(nb, H_MLSTM, HEAD_DIM))
            outs["mm_p"].append(m_p.reshape(nb, H_MLSTM))

            gts = _gates(_pad_rows(gs, db, t, CHUNK).reshape(db * CHUNK, LANES), gate_bias[j], db, CHUNK, t)
            gts4 = gts.reshape(db, 32, 1, CHUNK)
            nk = NEW_KEYS
            c_new = gts[:, 0:8, :nk].transpose(0, 2, 1).reshape(db, 1, nk * H_FOX)
            o_dec = _fox_decode(page_table, _head_rows(ps[:, :fw], db, t, H_FOX), k_pool, v_pool, cs_pool, j,
                                _pad_rows(_head_rows(ps[:, fw:2 * fw], db, t, H_FOX).reshape(db * t * H_FOX, HEAD_DIM),
                                          db, t * H_FOX, nk * H_FOX),
                                _pad_rows(_head_rows(ps[:, 2 * fw:3 * fw], db, t, H_FOX).reshape(db * t * H_FOX, HEAD_DIM),
                                          db, t * H_FOX, nk * H_FOX),
                                c_new)
            o_fox_s = o_dec.reshape(ms, fw)
            ps_pad = _pad_rows(ps, db, t, CHUNK).reshape(1, db * CHUNK, -1)
            hn_s, c_s, n_s, m_s = _mlstm(ps_pad, 0, gts4, state_mlstm_c[j],
                                         state_mlstm_n[j].reshape(db, H_MLSTM, HEAD_DIM, 1),
                                         state_mlstm_m[j].reshape(db, H_MLSTM, 1, 1),
                                         head_gain, j, db, CHUNK)
            hn_s = hn_s.reshape(db, CHUNK, mw)[:, :t].reshape(ms, mw)
            xp, xs = _proj_out(o_fox, 0, hn, 0, o_fox_s.astype(BF16), hn_s, ab_w_out, j, xp, xs, mp, md, 5)
            outs["fk_s"].append(ps[:, fw:2 * fw].reshape(db, t, H_FOX, HEAD_DIM))
            outs["fv_s"].append(ps[:, 2 * fw:3 * fw].reshape(db, t, H_FOX, HEAD_DIM))
            outs["fl_s"].append(gts[:, 24:32, :t].transpose(0, 2, 1))
            outs["mc_s"].append(c_s)
            outs["mn_s"].append(n_s.reshape(db, H_MLSTM, HEAD_DIM))
            outs["mm_s"].append(m_s.reshape(db, H_MLSTM))
        else:
            pp_c, ps = _proj_in(xp, xs, mp, md, (3, 4), norm_mix3, l, c_w_in, None, j, j, n_c, pp_c)
            o_dil = _dil_prompt(pp_c, j, nb, s)

            k_new = _head_rows(ps[:, dw:2 * dw], db, t, H_DIL)
            v_new = _head_rows(ps[:, 2 * dw:], db, t, H_DIL)
            nrow = NEW_KEYS * H_DIL
            o_dec = _dil_decode(_head_rows(ps[:, :dw], db, t, H_DIL), k_win, v_win, j,
                                jnp.pad(k_new, ((0, 0), (0, nrow - t * H_DIL), (0, 0))),
                                jnp.pad(v_new, ((0, 0), (0, nrow - t * H_DIL), (0, 0))), t)
            o_dil_s = o_dec.reshape(ms, dw)
            o_dil_s = o_dil_s.astype(BF16)
            xp, xs = _proj_out(o_dil, 0, o_dil, 1, o_dil_s, o_dil_s, c_w_out, j, xp, xs, mp, md, 5)
            outs["wk_new"].append(k_new.reshape(db, t, H_DIL, HEAD_DIM))
            outs["wv_new"].append(v_new.reshape(db, t, H_DIL, HEAD_DIM))

        fin = norm_final if last else None
        xp, xs = _ffn(xp, xs, mp, md, (6, 7, 8), norm_post3, ffn_post_w13, ffn_post_w2, l, fin)

    st = lambda name: jnp.stack(outs[name])
    keep = min(DIL_PATTERNS[-1][0], cb + t)
    wk_s = jnp.concatenate([cache_win_k, st("wk_new")], axis=2)[:, :, cb + t - keep:]
    wv_s = jnp.concatenate([cache_win_v, st("wv_new")], axis=2)[:, :, cb + t - keep:]
    keep_p = min(DIL_PATTERNS[-1][0], s)
    heads = lambda a, n_heads: a.reshape(a.shape[0], nb, s, n_heads, HEAD_DIM)
    return (xp.reshape(nb, s, d), xs.reshape(db, t, d),
            heads(pp_ab[:, :, fw:2 * fw], H_FOX), heads(pp_ab[:, :, 2 * fw:3 * fw], H_FOX),
            st("fl_p"), st("fk_s"), st("fv_s"), st("fl_s"),
            st("mc_p"), st("mn_p"), st("mm_p"), st("mc_s"), st("mn_s"), st("mm_s"),
            heads(pp_c[:, :, dw:2 * dw], H_DIL)[:, :, s - keep_p:], heads(pp_c[:, :, 2 * dw:], H_DIL)[:, :, s - keep_p:],
            wk_s, wv_s)
```
